```python
import jax, jax.numpy as jnp
from jax import lax
import numpy as np

D_MODEL = 1024
BATCH = 2
SEQ = 8192
DEPTH = 1
DEC_BATCH = 1
DEC_SEQ = 16384
PAST_LEN = 128

HG_HEADS = 4
HG_KDIM = 128
HG_VDIM = 128
HG_WIDTH = HG_HEADS * HG_KDIM
HG_CHUNK = 64
ATT_HEADS = 8
ATT_HDIM = 64
ATT_WIDTH = ATT_HEADS * ATT_HDIM
DILATION_PATTERNS = ((128, 1), (512, 4), (2048, 16))
ROPE_THETA = 10000.0
FFN_HIDDEN = ((8 * D_MODEL // 3 + 255) // 256) * 256
NORM_EPS = 1e-6
IN_SIZES = (HG_WIDTH, HG_WIDTH, HG_WIDTH, HG_WIDTH, HG_WIDTH,
            ATT_WIDTH, ATT_WIDTH, ATT_WIDTH, D_MODEL, D_MODEL)
IN_WIDTH = 5 * HG_WIDTH + 3 * ATT_WIDTH + 2 * D_MODEL

kernel_name = "hgrn2_dilated_attn_adaln_encoder"


def _split_points():
    pts, acc = [], 0
    for s in IN_SIZES[:-1]:
        acc += s
        pts.append(acc)
    return pts


def rms_norm(x, gain):
    xf = x.astype(jnp.float32)
    y = xf * lax.rsqrt(jnp.mean(xf * xf, axis=-1, keepdims=True) + NORM_EPS)
    return (y * gain.astype(jnp.float32)).astype(x.dtype)


def rope(x, pos):
    half = x.shape[-1] // 2
    inv = ROPE_THETA ** (-jnp.arange(half, dtype=jnp.float32) / half)
    ang = pos[:, None] * inv[None, :]
    cos = jnp.cos(ang)[None, :, None, :]
    sin = jnp.sin(ang)[None, :, None, :]
    x1, x2 = x[..., :half], x[..., half:]
    return jnp.concatenate([x1 * cos - x2 * sin, x2 * cos + x1 * sin], axis=-1)


def hgrn2_scan(q, k, v, log_f):
    B, L, H, dk = q.shape
    dv = v.shape[-1]
    n = L // HG_CHUNK

    def to_chunks(t):
        return t.reshape(B, n, HG_CHUNK, H, t.shape[-1]).transpose(1, 0, 3, 2, 4)

    qc, kc, vc, gc = to_chunks(q), to_chunks(k), to_chunks(v), to_chunks(log_f)
    causal = jnp.tril(jnp.ones((HG_CHUNK, HG_CHUNK), dtype=bool))[:, :, None]

    def step(S, inp):
        qb, kb, vb, gb = inp
        b = jnp.cumsum(gb, axis=2)
        diff = b[:, :, :, None, :] - b[:, :, None, :, :]
        decay = jnp.exp(jnp.where(causal, diff, -jnp.inf))
        scores = jnp.einsum('bhtk,bhsk,bhtsk->bhts', qb, kb, decay)
        o_intra = jnp.einsum('bhts,bhsv->bhtv', scores, vb)
        o_inter = jnp.einsum('bhtk,bhkv->bhtv', qb * jnp.exp(b), S)
        b_last = b[:, :, -1:, :]
        S_new = S * jnp.exp(b_last[:, :, 0, :])[..., None] + jnp.einsum(
            'bhsk,bhsv->bhkv', kb * jnp.exp(b_last - b), vb)
        return S_new, o_intra + o_inter

    S0 = jnp.zeros((B, H, dk, dv), jnp.float32)
    _, o = lax.scan(step, S0, (qc, kc, vc, gc))
    return o.transpose(1, 0, 3, 2, 4).reshape(B, L, H, dv)


def hgrn2_mixer(q_raw, f_fwd_raw, f_bwd_raw, i_raw, g_raw, lb_fwd, lb_bwd, gn_gain):
    B, L, _ = q_raw.shape

    def heads(t, d):
        return t.reshape(B, L, HG_HEADS, d)

    q = heads(jax.nn.silu(q_raw.astype(jnp.float32)) * HG_KDIM ** -0.5, HG_KDIM)
    v = heads(i_raw.astype(jnp.float32), HG_VDIM)

    def gates(f_raw, lb):
        lb = lb.astype(jnp.float32)
        f = lb + (1.0 - lb) * jax.nn.sigmoid(f_raw.astype(jnp.float32))
        return heads(jnp.log(f), HG_KDIM), heads(1.0 - f, HG_KDIM)

    logf_f, k_f = gates(f_fwd_raw, lb_fwd)
    logf_b, k_b = gates(f_bwd_raw, lb_bwd)
    o_fwd = hgrn2_scan(q, k_f, v, logf_f)
    flip = lambda t: jnp.flip(t, axis=1)
    o_bwd = flip(hgrn2_scan(flip(q), flip(k_b), flip(v), flip(logf_b)))
    o = o_fwd + o_bwd
    o = o * lax.rsqrt(jnp.mean(o * o, axis=-1, keepdims=True) + NORM_EPS)
    o = o.reshape(B, L, HG_WIDTH) * gn_gain.astype(jnp.float32)
    return o * jax.nn.silu(g_raw.astype(jnp.float32))


def dilated_branch(q, k, v, window, dilation):
    B, L, H, D = q.shape
    r = dilation
    radius = window // (2 * r)
    blk = radius
    Lr = L // r
    nb = -(-Lr // blk)
    pad = nb * blk - Lr

    def sub(t):
        return t.reshape(B, Lr, r, H, D).transpose(0, 2, 1, 3, 4)

    qs = jnp.pad(sub(q), ((0, 0), (0, 0), (0, pad), (0, 0), (0, 0))).reshape(B, r, nb, blk, H, D)

    def band(t):
        tp = jnp.pad(sub(t), ((0, 0), (0, 0), (blk, pad + blk), (0, 0), (0, 0)))
        tp = tp.reshape(B, r, nb + 2, blk, H, D)
        return jnp.concatenate([tp[:, :, :-2], tp[:, :, 1:-1], tp[:, :, 2:]], axis=3)

    kb, vb = band(k), band(v)
    i = jnp.arange(blk)[:, None]
    j = jnp.arange(3 * blk)[None, :]
    kpos = jnp.arange(nb)[:, None, None] * blk - blk + j[None]
    mask = (jnp.abs(i - (j - blk)) <= radius)[None] & (kpos >= 0) & (kpos < Lr)
    s = jnp.einsum('brnqhd,brnkhd->brnhqk', qs, kb)
    s = jnp.where(mask[:, None], s, -jnp.inf)
    m = jnp.max(s, axis=-1)
    m = jnp.where(jnp.isfinite(m), m, 0.0)
    p = jnp.exp(s - m[..., None])
    den = jnp.sum(p, axis=-1)
    num = jnp.einsum('brnhqk,brnkhd->brnqhd', p, vb)

    num = num.reshape(B, r, nb * blk, H, D)[:, :, :Lr].transpose(0, 2, 1, 3, 4).reshape(B, L, H, D)

    def back(t):
        t = t.transpose(0, 1, 2, 4, 3).reshape(B, r, nb * blk, H)[:, :, :Lr]
        return t.transpose(0, 2, 1, 3).reshape(B, L, H)

    return num, back(m), back(den)


def dilated_attention(q_raw, k_raw, v_raw):
    B, L, _ = q_raw.shape
    pos = jnp.arange(L, dtype=jnp.float32)
    heads = lambda t: t.astype(jnp.float32).reshape(B, L, ATT_HEADS, ATT_HDIM)
    q = rope(heads(q_raw), pos) * ATT_HDIM ** -0.5
    k = rope(heads(k_raw), pos)
    v = heads(v_raw)
    parts = [dilated_branch(q, k, v, w, r) for (w, r) in DILATION_PATTERNS]
    M = jnp.max(jnp.stack([pm for (_, pm, _) in parts]), axis=0)
    num = jnp.zeros_like(q)
    den = jnp.zeros_like(M)
    for (n_i, m_i, d_i) in parts:
        wgt = jnp.exp(m_i - M)
        num = num + n_i * wgt[..., None]
        den = den + d_i * wgt
    return (num / den[..., None]).reshape(B, L, ATT_WIDTH)


def encoder_layer(x, c, w_ada, b_ada, norm1_g, w_in, b_in, lb_fwd, lb_bwd, hg_norm_g,
                  w_branch_a, w_branch_b, w_out, norm2_g, w_ffn_in, w_ffn_out):
    mod = jax.nn.silu(c) @ w_ada + b_ada
    shift1, scale1, gate1, shift2, scale2, gate2 = jnp.split(mod[:, None, :], 6, axis=-1)
    h = rms_norm(x, norm1_g) * (1.0 + scale1) + shift1
    proj = h @ w_in + b_in
    (q_hg, f_fw, f_bw, i_hg, g_hg, q_at, k_at, v_at, g_a, g_b) = jnp.split(proj, _split_points(), axis=-1)
    o_a = hgrn2_mixer(q_hg, f_fw, f_bw, i_hg, g_hg, lb_fwd, lb_bwd, hg_norm_g).astype(x.dtype)
    o_b = dilated_attention(q_at, k_at, v_at).astype(x.dtype)
    merged = jax.nn.sigmoid(g_a) * (o_a @ w_branch_a) + jax.nn.sigmoid(g_b) * (o_b @ w_branch_b)
    x = x + gate1 * (merged @ w_out)
    h = rms_norm(x, norm2_g) * (1.0 + scale2) + shift2
    gt, up = jnp.split(h @ w_ffn_in, 2, axis=-1)
    x = x + gate2 * ((jax.nn.silu(gt) * up) @ w_ffn_out)
    return x


def trunk(x, c, w_ada, b_ada, norm1_g, w_in, b_in, lb_logits, hg_norm_g, w_branch_a, w_branch_b,
          w_out, norm2_g, w_ffn_in, w_ffn_out, final_norm_g):
    lb_all = jnp.cumsum(jax.nn.softmax(lb_logits.astype(jnp.float32), axis=0), axis=0)
    for l in range(DEPTH):
        x = encoder_layer(x, c, w_ada[l], b_ada[l], norm1_g[l], w_in[l], b_in[l],
                          lb_all[l, 0], lb_all[l, 1], hg_norm_g[l], w_branch_a[l], w_branch_b[l],
                          w_out[l], norm2_g[l], w_ffn_in[l], w_ffn_out[l])
    return rms_norm(x, final_norm_g)


def setup_inputs(seed: int = 0) -> dict:
    key = jax.random.key(seed)
    ks = jax.random.split(key, 20)
    nrm = lambda k, shape, s: jax.random.normal(k, shape, jnp.float32) * s
    D = D_MODEL
    return {
        "x_prompt": nrm(ks[0], (BATCH, SEQ, D), 1.0),
        "x_sample": nrm(ks[1], (DEC_BATCH, DEC_SEQ, D), 1.0),
        "c_prompt": nrm(ks[2], (BATCH, D), 1.0),
        "c_sample": nrm(ks[3], (DEC_BATCH, D), 1.0),
        "w_ada": nrm(ks[4], (DEPTH, D, 6 * D), 0.5 * D ** -0.5),
        "b_ada": nrm(ks[5], (DEPTH, 6 * D), 0.01),
        "norm1_g": 1.0 + nrm(ks[6], (DEPTH, D), 0.02),
        "w_in": nrm(ks[7], (DEPTH, D, IN_WIDTH), D ** -0.5),
        "b_in": nrm(ks[8], (DEPTH, IN_WIDTH), 0.01),
        "lb_logits": nrm(ks[9], (DEPTH + 1, 2, HG_WIDTH), 0.1),
        "hg_norm_g": 1.0 + nrm(ks[10], (DEPTH, HG_WIDTH), 0.02),
        "w_branch_a": nrm(ks[11], (DEPTH, HG_WIDTH, D), HG_WIDTH ** -0.5),
        "w_branch_b": nrm(ks[12], (DEPTH, ATT_WIDTH, D), ATT_WIDTH ** -0.5),
        "w_out": nrm(ks[13], (DEPTH, D, D), D ** -0.5),
        "norm2_g": 1.0 + nrm(ks[14], (DEPTH, D), 0.02),
        "w_ffn_in": nrm(ks[15], (DEPTH, D, 2 * FFN_HIDDEN), D ** -0.5),
        "w_ffn_out": nrm(ks[16], (DEPTH, FFN_HIDDEN, D), FFN_HIDDEN ** -0.5),
        "final_norm_g": 1.0 + nrm(ks[17], (D,), 0.02),
    }


def reference(x_prompt, x_sample, c_prompt, c_sample, w_ada, b_ada, norm1_g, w_in, b_in, lb_logits,
              hg_norm_g, w_branch_a, w_branch_b, w_out, norm2_g, w_ffn_in, w_ffn_out, final_norm_g):
    y_prompt = trunk(x_prompt, c_prompt, w_ada, b_ada, norm1_g, w_in, b_in, lb_logits, hg_norm_g,
                     w_branch_a, w_branch_b, w_out, norm2_g, w_ffn_in, w_ffn_out, final_norm_g)
    y_sample = trunk(x_sample, c_sample, w_ada, b_ada, norm1_g, w_in, b_in, lb_logits, hg_norm_g,
                     w_branch_a, w_branch_b, w_out, norm2_g, w_ffn_in, w_ffn_out, final_norm_g)
    return (y_prompt, y_sample)
```

```python
import functools

import jax
import jax.numpy as jnp
from jax import lax
from jax.experimental import pallas as pl
from jax.experimental.pallas import tpu as pltpu

F32 = jnp.float32
BF16 = jnp.bfloat16

NORM_EPS = 1e-6
HG_HEADS = 4
HG_DIM = 128
HG_WIDTH = HG_HEADS * HG_DIM
ATT_HEADS = 8
ATT_HDIM = 64
ATT_WIDTH = ATT_HEADS * ATT_HDIM
DILATION_PATTERNS = ((128, 1), (512, 4), (2048, 16))
ATT_RADIUS = 64
ROPE_THETA = 10000.0
PROJ_BLOCK = 512
N_PROJ_BLOCKS = 12

HG_CHUNK = 64
HG_SUB = 16
HG_SUB_SHIFT = 4
HG_EXP_CLAMP = 80.0
VMEM_LIMIT = 56 * 1024 * 1024


def _dot(a, b):
    return jnp.dot(a, b, preferred_element_type=F32)


def _dot_nt(a, b):
    return lax.dot_general(a, b, (((1,), (1,)), ((), ())), preferred_element_type=F32)


def _dot_tn(a, b):
    return lax.dot_general(a, b, (((0,), (0,)), ((), ())), preferred_element_type=F32)


def _sigmoid(x):
    return 1.0 / (1.0 + jnp.exp(-x))


def _rms(x):
    return x * lax.rsqrt(jnp.mean(x * x, axis=-1, keepdims=True) + NORM_EPS)


def _resident(shape):
    nd = len(shape)
    return pl.BlockSpec(shape, lambda *_: (0,) * nd, pipeline_mode=pl.Buffered(1))


def _mod_body(c_ref, w_ref, b_ref, o_ref):
    c = c_ref[...]
    a = c * _sigmoid(c)
    o_ref[...] = jnp.dot(a, w_ref[...], preferred_element_type=F32,
                         precision=lax.Precision.HIGHEST) + b_ref[...]


def _mod_call(c_pad, w_ada, b_ada):
    rows, d = c_pad.shape
    n = w_ada.shape[1]
    bn = 512
    return pl.pallas_call(
        _mod_body,
        grid=(n // bn,),
        in_specs=[pl.BlockSpec((rows, d), lambda j: (0, 0)),
                  pl.BlockSpec((d, bn), lambda j: (0, j)),
                  pl.BlockSpec((1, bn), lambda j: (0, j))],
        out_specs=pl.BlockSpec((rows, bn), lambda j: (0, j)),
        out_shape=jax.ShapeDtypeStruct((rows, n), F32),
        name="adaln_mod",
    )(c_pad, w_ada, b_ada.reshape(1, n))


def _lb_body(l_ref, o_ref):
    l = l_ref[...]
    e = jnp.exp(l - jnp.max(l, axis=0, keepdims=True))
    o_ref[...] = e[0:1] / jnp.sum(e, axis=0, keepdims=True)


def _lb_call(lb_logits):
    n = lb_logits.shape[0]
    flat = lb_logits.reshape(n, -1).astype(F32)
    return pl.pallas_call(
        _lb_body,
        out_shape=jax.ShapeDtypeStruct((1, flat.shape[1]), F32),
        name="hgrn_lower_bounds",
    )(flat)


def _rope(p, cos, sin_signed):
    n = p.shape[-1]
    lane = lax.broadcasted_iota(jnp.int32, p.shape, 1)
    first_half = (lane & 63) < 32
    partner = jnp.where(first_half, pltpu.roll(p, n - 32, 1), pltpu.roll(p, 32, 1))
    reps = n // cos.shape[-1]
    return p * jnp.tile(cos, (1, reps)) + partner * jnp.tile(sin_signed, (1, reps))


def _inproj_body(x_ref, mod_ref, g1_ref, w_ref, b_ref, lb_ref, cos_ref, sin_ref,
                 qh_ref, lff_ref, kf_ref, lfb_ref, kb_ref, vh_ref, sg_ref,
                 qa_ref, ka_ref, va_ref, ga_ref, gb_ref):
    x = x_ref[0]
    mod = mod_ref[0]
    shift1, scale1 = mod[0:1], mod[1:2]
    h = _rms(x) * g1_ref[...]
    h = h * (1.0 + scale1) + shift1
    hb = h.astype(BF16)

    def proj(j):
        cols = slice(j * PROJ_BLOCK, (j + 1) * PROJ_BLOCK)
        return _dot(hb, w_ref[:, cols]) + b_ref[:, cols]

    p = proj(0)
    qh_ref[0] = p * _sigmoid(p) * (HG_DIM ** -0.5)

    def gates(p, lb, lf_ref, k_ref):
        f = lb + (1.0 - lb) * _sigmoid(p)
        lf_ref[0] = jnp.log(f)
        k_ref[0] = 1.0 - f

    gates(proj(1), lb_ref[:, 0:HG_WIDTH], lff_ref, kf_ref)
    gates(proj(2), lb_ref[:, HG_WIDTH:2 * HG_WIDTH], lfb_ref, kb_ref)
    vh_ref[0] = proj(3).astype(BF16)
    p = proj(4)
    sg_ref[0] = p * _sigmoid(p)

    cos, sin_signed = cos_ref[...], sin_ref[...]
    qa_ref[0] = (_rope(proj(5), cos, sin_signed) * (ATT_HDIM ** -0.5)).astype(BF16)
    ka_ref[0] = _rope(proj(6), cos, sin_signed).astype(BF16)
    va_ref[0] = proj(7).astype(BF16)
    ga_ref[0, :, 0:PROJ_BLOCK] = _sigmoid(proj(8))
    ga_ref[0, :, PROJ_BLOCK:2 * PROJ_BLOCK] = _sigmoid(proj(9))
    gb_ref[0, :, 0:PROJ_BLOCK] = _sigmoid(proj(10))
    gb_ref[0, :, PROJ_BLOCK:2 * PROJ_BLOCK] = _sigmoid(proj(11))


def _inproj_call(x, mod, g1, w_in, b_in, lb, cos, sin_signed, tm):
    B, L, D = x.shape
    n_in = w_in.shape[1]
    assert n_in == N_PROJ_BLOCKS * PROJ_BLOCK and L % tm == 0
    row = lambda w: pl.BlockSpec((1, tm, w), lambda b, i: (b, i, 0))
    tab = pl.BlockSpec((tm, cos.shape[1]), lambda b, i: (i, 0))
    shp = lambda w, dt: jax.ShapeDtypeStruct((B, L, w), dt)
    return pl.pallas_call(
        _inproj_body,
        grid=(B, L // tm),
        in_specs=[row(D),
                  pl.BlockSpec((1,) + mod.shape[1:], lambda b, i: (b, 0, 0)),
                  _resident(g1.shape), _resident(w_in.shape), _resident(b_in.shape), _resident(lb.shape),
                  tab, tab],
        out_specs=[row(512)] * 10 + [row(1024)] * 2,
        out_shape=[shp(512, F32), shp(512, F32), shp(512, F32), shp(512, F32), shp(512, F32),
                   shp(512, BF16), shp(512, F32), shp(512, BF16), shp(512, BF16), shp(512, BF16),
                   shp(1024, F32), shp(1024, F32)],
        compiler_params=pltpu.CompilerParams(
            dimension_semantics=("parallel", "parallel"), vmem_limit_bytes=VMEM_LIMIT),
        name="inproj",
    )(x, mod, g1, w_in, b_in, lb, cos, sin_signed)


def _hgrn_chunk(q, k, v_bf, g, s_t, rev):
    C, dk = q.shape
    nsb = C // HG_SUB
    t = lax.broadcasted_iota(jnp.int32, (C, C), 0)
    s = lax.broadcasted_iota(jnp.int32, (C, C), 1)
    order = (s >= t) if rev else (s <= t)
    same_sub = (t >> HG_SUB_SHIFT) == (s >> HG_SUB_SHIFT)
    c = jnp.dot(order.astype(F32), g, preferred_element_type=F32, precision=lax.Precision.HIGHEST)
    rowid = lax.broadcasted_iota(jnp.int32, (C, dk), 0)

    def ref_row(J):
        r = J * HG_SUB if rev else J * HG_SUB + HG_SUB - 1
        return c[r:r + 1]

    def rows_of(J):
        return (rowid >= J * HG_SUB) & (rowid < (J + 1) * HG_SUB)

    bcast = lambda row: jnp.broadcast_to(row, (HG_SUB, dk))
    edge = jnp.concatenate([bcast(ref_row(J)) for J in range(nsb)], axis=0)
    mid = jnp.concatenate([bcast(c[J * HG_SUB + HG_SUB // 2:J * HG_SUB + HG_SUB // 2 + 1])
                           for J in range(nsb)], axis=0)

    k_edge = k * jnp.exp(edge - c)
    q_parts, k_parts = [], []
    key_blocks = range(1, nsb) if rev else range(nsb - 1)
    for J in key_blocks:
        queries = (rowid < J * HG_SUB) if rev else (rowid >= (J + 1) * HG_SUB)
        q_parts.append(jnp.where(queries, q * jnp.exp(jnp.minimum(c - ref_row(J), 0.0)), 0.0).astype(BF16))
        k_parts.append(jnp.where(rows_of(J), k_edge, 0.0).astype(BF16))
    scores = _dot_nt(jnp.concatenate(q_parts, axis=1), jnp.concatenate(k_parts, axis=1))

    q_mid = q * jnp.exp(jnp.clip(c - mid, -HG_EXP_CLAMP, HG_EXP_CLAMP))
    k_mid = k * jnp.exp(jnp.clip(mid - c, -HG_EXP_CLAMP, HG_EXP_CLAMP))
    inner = _dot_nt(q_mid.astype(BF16), k_mid.astype(BF16))
    scores = scores + jnp.where(same_sub & order, inner, 0.0)

    o = _dot(scores.astype(BF16), v_bf) + _dot_nt((q * jnp.exp(c)).astype(BF16), s_t.astype(BF16))
    c_end = c[0:1] if rev else c[C - 1:C]
    k_end = (k * jnp.exp(c_end - c)).astype(BF16)
    s_new = s_t * jnp.exp(c_end) + _dot_tn(v_bf, k_end)
    return o, s_new


def _hgrn_body(qf_ref, lff_ref, kf_ref, vf_ref, qb_ref, lfb_ref, kb_ref, vb_ref,
               of_ref, ob_ref, sf_ref, sb_ref):
    @pl.when(pl.program_id(2) == 0)
    def _():
        sf_ref[...] = jnp.zeros_like(sf_ref)
        sb_ref[...] = jnp.zeros_like(sb_ref)

    T = qf_ref.shape[1]
    nch = T // HG_CHUNK
    s_f, s_b = sf_ref[...], sb_ref[...]
    for ci in range(nch):
        rows = slice(ci * HG_CHUNK, (ci + 1) * HG_CHUNK)
        o, s_f = _hgrn_chunk(qf_ref[0, rows], kf_ref[0, rows], vf_ref[0, rows], lff_ref[0, rows], s_f, False)
        of_ref[0, rows] = o
        rows = slice((nch - 1 - ci) * HG_CHUNK, (nch - ci) * HG_CHUNK)
        o, s_b = _hgrn_chunk(qb_ref[0, rows], kb_ref[0, rows], vb_ref[0, rows], lfb_ref[0, rows], s_b, True)
        ob_ref[0, rows] = o
    sf_ref[...] = s_f
    sb_ref[...] = s_b


def _hgrn_call(q, lf_f, k_f, lf_b, k_b, v, T):
    B, L, W = q.shape
    nb = L // T
    assert L % T == 0 and T % HG_CHUNK == 0 and W == HG_WIDTH
    fwd = pl.BlockSpec((1, T, HG_DIM), lambda b, h, j: (b, j, h))
    bwd = pl.BlockSpec((1, T, HG_DIM), lambda b, h, j: (b, nb - 1 - j, h))
    return pl.pallas_call(
        _hgrn_body,
        grid=(B, HG_HEADS, nb),
        in_specs=[fwd, fwd, fwd, fwd, bwd, bwd, bwd, bwd],
        out_specs=[fwd, bwd],
        out_shape=[jax.ShapeDtypeStruct((B, L, W), F32)] * 2,
        scratch_shapes=[pltpu.VMEM((HG_DIM, HG_DIM), F32)] * 2,
        compiler_params=pltpu.CompilerParams(
            dimension_semantics=("parallel", "parallel", "arbitrary"), vmem_limit_bytes=VMEM_LIMIT),
        name="hgrn_scan",
    )(q, lf_f, k_f, v, q, lf_b, k_b, v)


ATT_QBLK = 2 * ATT_RADIUS


def _attn_body(q_ref, kp_ref, kc_ref, kn_ref, vp_ref, vc_ref, vn_ref, o_ref, lse_ref):
    n = pl.program_id(2)
    last = pl.num_programs(2) - 1
    kband = jnp.concatenate([kp_ref[0], kc_ref[0], kn_ref[0]], axis=0)
    vband = jnp.concatenate([vp_ref[0], vc_ref[0], vn_ref[0]], axis=0)
    nq, nk = ATT_QBLK, ATT_QBLK + 2 * ATT_RADIUS
    i = lax.broadcasted_iota(jnp.int32, (nq, nk), 0)
    j = lax.broadcasted_iota(jnp.int32, (nq, nk), 1)
    valid = (j >= i) & (j <= i + 2 * ATT_RADIUS)
    valid = valid & ((j >= ATT_RADIUS) | (n > 0)) & ((j < nq + ATT_RADIUS) | (n < last))
    q = q_ref[0]
    for h in range(ATT_HEADS):
        cols = slice(h * ATT_HDIM, (h + 1) * ATT_HDIM)
        s = jnp.where(valid, _dot_nt(q[:, cols], kband[:, cols]), -jnp.inf)
        m = jnp.max(s, axis=-1, keepdims=True)
        p = jnp.exp(s - m)
        den = jnp.sum(p, axis=-1, keepdims=True)
        num = _dot(p.astype(BF16), vband[:, cols])
        o_ref[0, :, cols] = num / den
        lse_ref[0, :, cols] = jnp.broadcast_to(m + jnp.log(den), (nq, ATT_HDIM))


def _attn_call(q, k, v, r):
    B, L, W = q.shape
    Lr = L // r
    assert L % r == 0 and Lr % ATT_QBLK == 0 and Lr // ATT_QBLK >= 2
    view = lambda t: t.reshape(B, Lr, r * W)
    nhalf = Lr // ATT_RADIUS
    center = pl.BlockSpec((1, ATT_QBLK, W), lambda b, c, n: (b, n, c))
    prev = pl.BlockSpec((1, ATT_RADIUS, W), lambda b, c, n: (b, jnp.maximum(2 * n - 1, 0), c))
    nxt = pl.BlockSpec((1, ATT_RADIUS, W), lambda b, c, n: (b, jnp.minimum(2 * n + 2, nhalf - 1), c))
    out, lse = pl.pallas_call(
        _attn_body,
        grid=(B, r, Lr // ATT_QBLK),
        in_specs=[center, prev, center, nxt, prev, center, nxt],
        out_specs=[center, center],
        out_shape=[jax.ShapeDtypeStruct((B, Lr, r * W), F32)] * 2,
        compiler_params=pltpu.CompilerParams(
            dimension_semantics=("parallel", "parallel", "parallel"), vmem_limit_bytes=VMEM_LIMIT),
        name=f"dilated_attn_r{r}",
    )(view(q), view(k), view(k), view(k), view(v), view(v), view(v))
    return out.reshape(B, L, W), lse.reshape(B, L, W)


FFN_CHUNK = 256


def _merge_body(x_ref, mod_ref, of_ref, ob_ref, sg_ref, a1_ref, l1_ref, a2_ref, l2_ref, a3_ref, l3_ref,
                ga_ref, gb_ref, gn_ref, n2_ref, fin_ref, wa_ref, wb_ref, wo_ref, wfi_ref, wfo_ref,
                y_ref, act_ref):
    mod = mod_ref[0]
    gate1, shift2, scale2, gate2 = mod[2:3], mod[3:4], mod[4:5], mod[5:6]

    o = of_ref[0] + ob_ref[0]
    o_a = jnp.concatenate([_rms(o[:, h * HG_DIM:(h + 1) * HG_DIM]) for h in range(HG_HEADS)], axis=1)
    o_a = o_a * gn_ref[...] * sg_ref[0]

    l1, l2, l3 = l1_ref[0], l2_ref[0], l3_ref[0]
    top = jnp.maximum(jnp.maximum(l1, l2), l3)
    w1, w2, w3 = jnp.exp(l1 - top), jnp.exp(l2 - top), jnp.exp(l3 - top)
    o_b = (a1_ref[0] * w1 + a2_ref[0] * w2 + a3_ref[0] * w3) / (w1 + w2 + w3)

    merged = (ga_ref[0] * _dot(o_a.astype(BF16), wa_ref[...])
              + gb_ref[0] * _dot(o_b.astype(BF16), wb_ref[...]))
    x1 = x_ref[0] + gate1 * _dot(merged.astype(BF16), wo_ref[...])

    h2 = (_rms(x1) * n2_ref[...] * (1.0 + scale2) + shift2).astype(BF16)
    hidden = wfo_ref.shape[0]
    for c0 in range(0, hidden, FFN_CHUNK):
        gt = _dot(h2, wfi_ref[:, c0:c0 + FFN_CHUNK])
        up = _dot(h2, wfi_ref[:, hidden + c0:hidden + c0 + FFN_CHUNK])
        act_ref[:, c0:c0 + FFN_CHUNK] = (gt * _sigmoid(gt) * up).astype(BF16)
    x2 = x1 + gate2 * _dot(act_ref[...], wfo_ref[...])
    y_ref[0] = _rms(x2) * fin_ref[...]


def _merge_call(x, mod, o_f, o_b, sg, att, ga, gb, gn, n2, fin, wa, wb, wo, wfi, wfo, tm):
    B, L, D = x.shape
    hidden = wfo.shape[0]
    assert L % tm == 0 and hidden % FFN_CHUNK == 0 and wfi.shape[1] == 2 * hidden
    row = lambda w: pl.BlockSpec((1, tm, w), lambda b, i: (b, i, 0))
    (a1, l1), (a2, l2), (a3, l3) = att
    return pl.pallas_call(
        _merge_body,
        grid=(B, L // tm),
        in_specs=[row(D), pl.BlockSpec((1,) + mod.shape[1:], lambda b, i: (b, 0, 0))]
                 + [row(512)] * 9 + [row(D), row(D)]
                 + [_resident(t.shape) for t in (gn, n2, fin, wa, wb, wo, wfi, wfo)],
        out_specs=row(D),
        out_shape=jax.ShapeDtypeStruct((B, L, D), F32),
        scratch_shapes=[pltpu.VMEM((tm, hidden), BF16)],
        compiler_params=pltpu.CompilerParams(
            dimension_semantics=("parallel", "parallel"), vmem_limit_bytes=VMEM_LIMIT),
        name="merge_ffn",
    )(x, mod, o_f, o_b, sg, a1, l1, a2, l2, a3, l3, ga, gb, gn, n2, fin, wa, wb, wo, wfi, wfo)


def _rope_tables(L):
    half = ATT_HDIM // 2
    pos = jnp.arange(L, dtype=F32)
    inv = ROPE_THETA ** (-jnp.arange(half, dtype=F32) / half)
    ang = pos[:, None] * inv[None, :]
    cos, sin = jnp.cos(ang), jnp.sin(ang)
    return jnp.tile(jnp.concatenate([cos, cos], axis=1), (1, 2)), jnp.tile(jnp.concatenate([-sin, sin], axis=1), (1, 2))


def _trunk(x, mod, lb, p, tm_in, tm_out, t_scan):
    B, L, D = x.shape
    cos, sin_signed = _rope_tables(L)
    (q_h, lf_f, k_f, lf_b, k_b, v_h, sg, q_a, k_a, v_a, ga, gb) = _inproj_call(
        x, mod, p["g1"], p["w_in"], p["b_in"], lb, cos, sin_signed, tm_in)
    o_f, o_b = _hgrn_call(q_h, lf_f, k_f, lf_b, k_b, v_h, t_scan)
    att = [_attn_call(q_a, k_a, v_a, r) for (_, r) in DILATION_PATTERNS]
    return _merge_call(x, mod, o_f, o_b, sg, att, ga, gb, p["gn"], p["n2"], p["fin"],
                       p["wa"], p["wb"], p["wo"], p["wfi"], p["wfo"], tm_out)


def kernel(x_prompt, x_sample, c_prompt, c_sample, w_ada, b_ada, norm1_g, w_in, b_in, lb_logits, hg_norm_g, w_branch_a, w_branch_b, w_out, norm2_g, w_ffn_in, w_ffn_out, final_norm_g):
    assert w_ada.shape[0] == 1 and lb_logits.shape[0] == 2, "single-layer trunk"
    D = x_prompt.shape[-1]
    bp, bs = c_prompt.shape[0], c_sample.shape[0]
    c_all = jnp.concatenate([c_prompt, c_sample], axis=0)
    c_pad = jnp.pad(c_all, ((0, -(bp + bs) % 8), (0, 0)))
    mod = _mod_call(c_pad, w_ada[0], b_ada[0]).reshape(c_pad.shape[0], 6, D)
    lb = _lb_call(lb_logits)
    row = lambda v: v.reshape(1, -1).astype(F32)
    p = dict(g1=row(norm1_g[0]), w_in=w_in[0].astype(BF16), b_in=row(b_in[0]), gn=row(hg_norm_g[0]),
             n2=row(norm2_g[0]), fin=row(final_norm_g), wa=w_branch_a[0].astype(BF16),
             wb=w_branch_b[0].astype(BF16), wo=w_out[0].astype(BF16), wfi=w_ffn_in[0].astype(BF16),
             wfo=w_ffn_out[0].astype(BF16))
    y_prompt = _trunk(x_prompt, mod[:bp], lb, p, 256, 256, 256)
    y_sample = _trunk(x_sample, mod[bp:bp + bs], lb, p, 256, 256, 256)
    return (y_prompt, y_sample)
```

```python
import jax
import jax.numpy as jnp
from jax import lax
from jax.experimental import pallas as pl
from jax.experimental.pallas import tpu as pltpu

F32 = jnp.float32
BF16 = jnp.bfloat16

NORM_EPS = 1e-6
HG_HEADS = 4
HG_DIM = 128
HG_WIDTH = HG_HEADS * HG_DIM
ATT_HEADS = 8
ATT_HDIM = 64
ATT_WIDTH = ATT_HEADS * ATT_HDIM
DILATION_PATTERNS = ((128, 1), (512, 4), (2048, 16))
ATT_RADIUS = 64
ROPE_THETA = 10000.0
PROJ_BLOCK = 512
N_PROJ_BLOCKS = 12

HG_CHUNK = 64
HG_SUB = 16
HG_SUB_SHIFT = 4
HG_EXP2_CLAMP = 110.0
EXP2_ZERO = -1e30
VMEM_LIMIT = 56 * 1024 * 1024


def _dot(a, b):
    return jnp.dot(a, b, preferred_element_type=F32)


def _dot_nt(a, b):
    return lax.dot_general(a, b, (((1,), (1,)), ((), ())), preferred_element_type=F32)


def _dot_tn(a, b):
    return lax.dot_general(a, b, (((0,), (0,)), ((), ())), preferred_element_type=F32)


def _sigmoid(x):
    return 1.0 / (1.0 + jnp.exp(-x))


def _rms(x):
    return x * lax.rsqrt(jnp.mean(x * x, axis=-1, keepdims=True) + NORM_EPS)


def _resident(shape):
    nd = len(shape)
    return pl.BlockSpec(shape, lambda *_: (0,) * nd, pipeline_mode=pl.Buffered(1))


def _mod_body(c_ref, w_ref, b_ref, o_ref):
    c = c_ref[...]
    a = c * _sigmoid(c)
    o_ref[...] = jnp.dot(a, w_ref[...], preferred_element_type=F32,
                         precision=lax.Precision.HIGHEST) + b_ref[...]


def _mod_call(c_pad, w_ada, b_ada):
    rows, d = c_pad.shape
    n = w_ada.shape[1]
    bn = 512
    return pl.pallas_call(
        _mod_body,
        grid=(n // bn,),
        in_specs=[pl.BlockSpec((rows, d), lambda j: (0, 0)),
                  pl.BlockSpec((d, bn), lambda j: (0, j)),
                  pl.BlockSpec((1, bn), lambda j: (0, j))],
        out_specs=pl.BlockSpec((rows, bn), lambda j: (0, j)),
        out_shape=jax.ShapeDtypeStruct((rows, n), F32),
        name="adaln_mod",
    )(c_pad, w_ada, b_ada.reshape(1, n))


def _lb_body(l_ref, o_ref):
    l = l_ref[...]
    e = jnp.exp(l - jnp.max(l, axis=0, keepdims=True))
    o_ref[...] = e[0:1] / jnp.sum(e, axis=0, keepdims=True)


def _lb_call(lb_logits):
    n = lb_logits.shape[0]
    flat = lb_logits.reshape(n, -1).astype(F32)
    return pl.pallas_call(
        _lb_body,
        out_shape=jax.ShapeDtypeStruct((1, flat.shape[1]), F32),
        name="hgrn_lower_bounds",
    )(flat)


def _rope(p, cos, sin_signed):
    n = p.shape[-1]
    lane = lax.broadcasted_iota(jnp.int32, p.shape, 1)
    first_half = (lane & 63) < 32
    partner = jnp.where(first_half, pltpu.roll(p, n - 32, 1), pltpu.roll(p, 32, 1))
    reps = n // cos.shape[-1]
    return p * jnp.tile(cos, (1, reps)) + partner * jnp.tile(sin_signed, (1, reps))


def _inproj_body(x_ref, mod_ref, g1_ref, w_ref, b_ref, lb_ref, cos_ref, sin_ref,
                 qh_ref, lff_ref, kf_ref, lfb_ref, kb_ref, vh_ref, sg_ref,
                 qa_ref, ka_ref, va_ref, ga_ref, gb_ref):
    x = x_ref[0]
    mod = mod_ref[0]
    shift1, scale1 = mod[0:1], mod[1:2]
    h = _rms(x) * g1_ref[...]
    h = h * (1.0 + scale1) + shift1
    hb = h.astype(BF16)

    def proj(j):
        cols = slice(j * PROJ_BLOCK, (j + 1) * PROJ_BLOCK)
        return _dot(hb, w_ref[:, cols]) + b_ref[:, cols]

    p = proj(0)
    qh_ref[0] = p * _sigmoid(p) * (HG_DIM ** -0.5)

    def gates(p, lb, lf_ref, k_ref):
        f = lb + (1.0 - lb) * _sigmoid(p)
        lf_ref[0] = jnp.log2(f)
        k_ref[0] = 1.0 - f

    gates(proj(1), lb_ref[:, 0:HG_WIDTH], lff_ref, kf_ref)
    gates(proj(2), lb_ref[:, HG_WIDTH:2 * HG_WIDTH], lfb_ref, kb_ref)
    vh_ref[0] = proj(3).astype(BF16)
    p = proj(4)
    sg_ref[0] = p * _sigmoid(p)

    cos, sin_signed = cos_ref[...], sin_ref[...]
    qa_ref[0] = (_rope(proj(5), cos, sin_signed) * (ATT_HDIM ** -0.5)).astype(BF16)
    ka_ref[0] = _rope(proj(6), cos, sin_signed).astype(BF16)
    va_ref[0] = proj(7).astype(BF16)
    ga_ref[0, :, 0:PROJ_BLOCK] = _sigmoid(proj(8))
    ga_ref[0, :, PROJ_BLOCK:2 * PROJ_BLOCK] = _sigmoid(proj(9))
    gb_ref[0, :, 0:PROJ_BLOCK] = _sigmoid(proj(10))
    gb_ref[0, :, PROJ_BLOCK:2 * PROJ_BLOCK] = _sigmoid(proj(11))


def _inproj_call(x, mod, g1, w_in, b_in, lb, cos, sin_signed, tm):
    B, L, D = x.shape
    n_in = w_in.shape[1]
    assert n_in == N_PROJ_BLOCKS * PROJ_BLOCK and L % tm == 0
    row = lambda w: pl.BlockSpec((1, tm, w), lambda b, i: (b, i, 0))
    tab = pl.BlockSpec((tm, cos.shape[1]), lambda b, i: (i, 0))
    shp = lambda w, dt: jax.ShapeDtypeStruct((B, L, w), dt)
    return pl.pallas_call(
        _inproj_body,
        grid=(B, L // tm),
        in_specs=[row(D),
                  pl.BlockSpec((1,) + mod.shape[1:], lambda b, i: (b, 0, 0)),
                  _resident(g1.shape), _resident(w_in.shape), _resident(b_in.shape), _resident(lb.shape),
                  tab, tab],
        out_specs=[row(512)] * 10 + [row(1024)] * 2,
        out_shape=[shp(512, F32), shp(512, F32), shp(512, F32), shp(512, F32), shp(512, F32),
                   shp(512, BF16), shp(512, F32), shp(512, BF16), shp(512, BF16), shp(512, BF16),
                   shp(1024, F32), shp(1024, F32)],
        compiler_params=pltpu.CompilerParams(
            dimension_semantics=("parallel", "parallel"), vmem_limit_bytes=VMEM_LIMIT),
        name="inproj",
    )(x, mod, g1, w_in, b_in, lb, cos, sin_signed)


def _bcast_row(c, group, r):
    T, w = c.shape
    c3 = c.reshape(T // group, group, w)
    return jnp.broadcast_to(c3[:, r:r + 1, :], c3.shape).reshape(T, w)


def _split3(g):
    g1 = g.astype(BF16)
    r1 = g - g1.astype(F32)
    g2 = r1.astype(BF16)
    return g1, g2, (r1 - g2.astype(F32)).astype(BF16)


def _hgrn_dir(q, k, v_bf, g, s_in, rev):
    T, dk = q.shape
    C, nch, nsb = HG_CHUNK, T // HG_CHUNK, HG_CHUNK // HG_SUB
    t = lax.broadcasted_iota(jnp.int32, (C, C), 0)
    s = lax.broadcasted_iota(jnp.int32, (C, C), 1)
    order = (s >= t) if rev else (s <= t)
    inner_mask = order & ((t >> HG_SUB_SHIFT) == (s >> HG_SUB_SHIFT))
    tri = order.astype(BF16)

    g_terms = jnp.concatenate(_split3(g), axis=1)
    c_parts = []
    for ci in range(nch):
        r = _dot(tri, g_terms[ci * C:(ci + 1) * C])
        c_parts.append(r[:, 0:dk] + r[:, dk:2 * dk] + r[:, 2 * dk:3 * dk])
    c = jnp.concatenate(c_parts, axis=0)

    far = 0 if rev else HG_SUB - 1
    edge = _bcast_row(c, HG_SUB, far)
    mid = _bcast_row(c, HG_SUB, HG_SUB // 2)
    c_end = _bcast_row(c, C, 0 if rev else C - 1)
    rowc = lax.broadcasted_iota(jnp.int32, (T, dk), 0) & (C - 1)

    k_edge = k * jnp.exp2(edge - c)
    q_cat, k_cat = [], []
    for J in (range(1, nsb) if rev else range(nsb - 1)):
        ref_j = _bcast_row(c, C, J * HG_SUB + far)
        queries = (rowc < J * HG_SUB) if rev else (rowc >= (J + 1) * HG_SUB)
        q_cat.append((q * jnp.exp2(jnp.where(queries, c - ref_j, EXP2_ZERO))).astype(BF16))
        in_j = (rowc >= J * HG_SUB) & (rowc < (J + 1) * HG_SUB)
        k_cat.append(jnp.where(in_j, k_edge, 0.0).astype(BF16))
    q_cat = jnp.concatenate(q_cat, axis=1)
    k_cat = jnp.concatenate(k_cat, axis=1)

    d_mid = jnp.clip(c - mid, -HG_EXP2_CLAMP, HG_EXP2_CLAMP)
    q_mid = (q * jnp.exp2(d_mid)).astype(BF16)
    k_mid = (k * jnp.exp2(-d_mid)).astype(BF16)
    q_dec = (q * jnp.exp2(c)).astype(BF16)
    k_end = (k * jnp.exp2(c_end - c)).astype(BF16)
    decay = jnp.exp2(c_end)

    o_intra, upd = [], []
    for ci in range(nch):
        rows = slice(ci * C, (ci + 1) * C)
        sc = _dot_nt(q_cat[rows], k_cat[rows])
        sc = sc + jnp.where(inner_mask, _dot_nt(q_mid[rows], k_mid[rows]), 0.0)
        o_intra.append(_dot(sc.astype(BF16), v_bf[rows]))
        upd.append(_dot_tn(v_bf[rows], k_end[rows]))

    outs = [None] * nch
    s_t = s_in
    for ci in (reversed(range(nch)) if rev else range(nch)):
        rows = slice(ci * C, (ci + 1) * C)
        outs[ci] = o_intra[ci] + _dot_nt(q_dec[rows], s_t.astype(BF16))
        s_t = s_t * decay[ci * C:ci * C + 1] + upd[ci]
    return jnp.concatenate(outs, axis=0), s_t


def _hgrn_body(qf_ref, lff_ref, kf_ref, vf_ref, qb_ref, lfb_ref, kb_ref, vb_ref,
               of_ref, ob_ref, sf_ref, sb_ref):
    @pl.when(pl.program_id(2) == 0)
    def _():
        sf_ref[...] = jnp.zeros_like(sf_ref)
        sb_ref[...] = jnp.zeros_like(sb_ref)

    o, s_f = _hgrn_dir(qf_ref[0], kf_ref[0], vf_ref[0], lff_ref[0], sf_ref[...], False)
    of_ref[0] = o
    sf_ref[...] = s_f
    o, s_b = _hgrn_dir(qb_ref[0], kb_ref[0], vb_ref[0], lfb_ref[0], sb_ref[...], True)
    ob_ref[0] = o
    sb_ref[...] = s_b


def _hgrn_call(q, lf_f, k_f, lf_b, k_b, v, T):
    B, L, W = q.shape
    nb = L // T
    assert L % T == 0 and T % HG_CHUNK == 0 and W == HG_WIDTH
    fwd = pl.BlockSpec((1, T, HG_DIM), lambda b, h, j: (b, j, h))
    bwd = pl.BlockSpec((1, T, HG_DIM), lambda b, h, j: (b, nb - 1 - j, h))
    return pl.pallas_call(
        _hgrn_body,
        grid=(B, HG_HEADS, nb),
        in_specs=[fwd, fwd, fwd, fwd, bwd, bwd, bwd, bwd],
        out_specs=[fwd, bwd],
        out_shape=[jax.ShapeDtypeStruct((B, L, W), F32)] * 2,
        scratch_shapes=[pltpu.VMEM((HG_DIM, HG_DIM), F32)] * 2,
        compiler_params=pltpu.CompilerParams(
            dimension_semantics=("parallel", "parallel", "arbitrary"), vmem_limit_bytes=VMEM_LIMIT),
        name="hgrn_scan",
    )(q, lf_f, k_f, v, q, lf_b, k_b, v)


ATT_QBLK = 2 * ATT_RADIUS


def _attn_body(q_ref, kp_ref, kc_ref, kn_ref, vp_ref, vc_ref, vn_ref, o_ref, lse_ref):
    n = pl.program_id(2)
    last = pl.num_programs(2) - 1
    kband = jnp.concatenate([kp_ref[0], kc_ref[0], kn_ref[0]], axis=0)
    vband = jnp.concatenate([vp_ref[0], vc_ref[0], vn_ref[0]], axis=0)
    nq, nk = ATT_QBLK, ATT_QBLK + 2 * ATT_RADIUS
    i = lax.broadcasted_iota(jnp.int32, (nq, nk), 0)
    j = lax.broadcasted_iota(jnp.int32, (nq, nk), 1)
    valid = (j >= i) & (j <= i + 2 * ATT_RADIUS)
    valid = valid & ((j >= ATT_RADIUS) | (n > 0)) & ((j < nq + ATT_RADIUS) | (n < last))
    q = q_ref[0]
    for h in range(ATT_HEADS):
        cols = slice(h * ATT_HDIM, (h + 1) * ATT_HDIM)
        s = jnp.where(valid, _dot_nt(q[:, cols], kband[:, cols]), -jnp.inf)
        m = jnp.max(s, axis=-1, keepdims=True)
        p = jnp.exp(s - m)
        den = jnp.sum(p, axis=-1, keepdims=True)
        num = _dot(p.astype(BF16), vband[:, cols])
        o_ref[0, :, cols] = num / den
        lse_ref[0, :, cols] = jnp.broadcast_to(m + jnp.log(den), (nq, ATT_HDIM))


def _attn_call(q, k, v, r):
    B, L, W = q.shape
    Lr = L // r
    assert L % r == 0 and Lr % ATT_QBLK == 0 and Lr // ATT_QBLK >= 2
    view = lambda t: t.reshape(B, Lr, r * W)
    nhalf = Lr // ATT_RADIUS
    center = pl.BlockSpec((1, ATT_QBLK, W), lambda b, c, n: (b, n, c))
    prev = pl.BlockSpec((1, ATT_RADIUS, W), lambda b, c, n: (b, jnp.maximum(2 * n - 1, 0), c))
    nxt = pl.BlockSpec((1, ATT_RADIUS, W), lambda b, c, n: (b, jnp.minimum(2 * n + 2, nhalf - 1), c))
    out, lse = pl.pallas_call(
        _attn_body,
        grid=(B, r, Lr // ATT_QBLK),
        in_specs=[center, prev, center, nxt, prev, center, nxt],
        out_specs=[center, center],
        out_shape=[jax.ShapeDtypeStruct((B, Lr, r * W), F32)] * 2,
        compiler_params=pltpu.CompilerParams(
            dimension_semantics=("parallel", "parallel", "parallel"), vmem_limit_bytes=VMEM_LIMIT),
        name=f"dilated_attn_r{r}",
    )(view(q), view(k), view(k), view(k), view(v), view(v), view(v))
    return out.reshape(B, L, W), lse.reshape(B, L, W)


FFN_CHUNK = 256


def _merge_body(x_ref, mod_ref, of_ref, ob_ref, sg_ref, a1_ref, l1_ref, a2_ref, l2_ref, a3_ref, l3_ref,
                ga_ref, gb_ref, gn_ref, n2_ref, fin_ref, wa_ref, wb_ref, wo_ref, wfi_ref, wfo_ref,
                y_ref, act_ref):
    mod = mod_ref[0]
    gate1, shift2, scale2, gate2 = mod[2:3], mod[3:4], mod[4:5], mod[5:6]

    o = of_ref[0] + ob_ref[0]
    o_a = jnp.concatenate([_rms(o[:, h * HG_DIM:(h + 1) * HG_DIM]) for h in range(HG_HEADS)], axis=1)
    o_a = o_a * gn_ref[...] * sg_ref[0]

    l1, l2, l3 = l1_ref[0], l2_ref[0], l3_ref[0]
    top = jnp.maximum(jnp.maximum(l1, l2), l3)
    w1, w2, w3 = jnp.exp(l1 - top), jnp.exp(l2 - top), jnp.exp(l3 - top)
    o_b = (a1_ref[0] * w1 + a2_ref[0] * w2 + a3_ref[0] * w3) / (w1 + w2 + w3)

    merged = (ga_ref[0] * _dot(o_a.astype(BF16), wa_ref[...])
              + gb_ref[0] * _dot(o_b.astype(BF16), wb_ref[...]))
    x1 = x_ref[0] + gate1 * _dot(merged.astype(BF16), wo_ref[...])

    h2 = (_rms(x1) * n2_ref[...] * (1.0 + scale2) + shift2).astype(BF16)
    hidden = wfo_ref.shape[0]
    for c0 in range(0, hidden, FFN_CHUNK):
        gt = _dot(h2, wfi_ref[:, c0:c0 + FFN_CHUNK])
        up = _dot(h2, wfi_ref[:, hidden + c0:hidden + c0 + FFN_CHUNK])
        act_ref[:, c0:c0 + FFN_CHUNK] = (gt * _sigmoid(gt) * up).astype(BF16)
    x2 = x1 + gate2 * _dot(act_ref[...], wfo_ref[...])
    y_ref[0] = _rms(x2) * fin_ref[...]


def _merge_call(x, mod, o_f, o_b, sg, att, ga, gb, gn, n2, fin, wa, wb, wo, wfi, wfo, tm):
    B, L, D = x.shape
    hidden = wfo.shape[0]
    assert L % tm == 0 and hidden % FFN_CHUNK == 0 and wfi.shape[1] == 2 * hidden
    row = lambda w: pl.BlockSpec((1, tm, w), lambda b, i: (b, i, 0))
    (a1, l1), (a2, l2), (a3, l3) = att
    return pl.pallas_call(
        _merge_body,
        grid=(B, L // tm),
        in_specs=[row(D), pl.BlockSpec((1,) + mod.shape[1:], lambda b, i: (b, 0, 0))]
                 + [row(512)] * 9 + [row(D), row(D)]
                 + [_resident(t.shape) for t in (gn, n2, fin, wa, wb, wo, wfi, wfo)],
        out_specs=row(D),
        out_shape=jax.ShapeDtypeStruct((B, L, D), F32),
        scratch_shapes=[pltpu.VMEM((tm, hidden), BF16)],
        compiler_params=pltpu.CompilerParams(
            dimension_semantics=("parallel", "parallel"), vmem_limit_bytes=VMEM_LIMIT),
        name="merge_ffn",
    )(x, mod, o_f, o_b, sg, a1, l1, a2, l2, a3, l3, ga, gb, gn, n2, fin, wa, wb, wo, wfi, wfo)


def _rope_tables(L):
    half = ATT_HDIM // 2
    pos = jnp.arange(L, dtype=F32)
    inv = ROPE_THETA ** (-jnp.arange(half, dtype=F32) / half)
    ang = pos[:, None] * inv[None, :]
    cos, sin = jnp.cos(ang), jnp.sin(ang)
    return jnp.tile(jnp.concatenate([cos, cos], axis=1), (1, 2)), jnp.tile(jnp.concatenate([-sin, sin], axis=1), (1, 2))


def _trunk(x, mod, lb, p, tm_in, tm_out, t_scan):
    B, L, D = x.shape
    cos, sin_signed = _rope_tables(L)
    (q_h, lf_f, k_f, lf_b, k_b, v_h, sg, q_a, k_a, v_a, ga, gb) = _inproj_call(
        x, mod, p["g1"], p["w_in"], p["b_in"], lb, cos, sin_signed, tm_in)
    o_f, o_b = _hgrn_call(q_h, lf_f, k_f, lf_b, k_b, v_h, t_scan)
    att = [_attn_call(q_a, k_a, v_a, r) for (_, r) in DILATION_PATTERNS]
    return _merge_call(x, mod, o_f, o_b, sg, att, ga, gb, p["gn"], p["n2"], p["fin"],
                       p["wa"], p["wb"], p["wo"], p["wfi"], p["wfo"], tm_out)


def kernel(x_prompt, x_sample, c_prompt, c_sample, w_ada, b_ada, norm1_g, w_in, b_in, lb_logits, hg_norm_g, w_branch_a, w_branch_b, w_out, norm2_g, w_ffn_in, w_ffn_out, final_norm_g):
    assert w_ada.shape[0] == 1 and lb_logits.shape[0] == 2, "single-layer trunk"
    D = x_prompt.shape[-1]
    bp, bs = c_prompt.shape[0], c_sample.shape[0]
    c_all = jnp.concatenate([c_prompt, c_sample], axis=0)
    c_pad = jnp.pad(c_all, ((0, -(bp + bs) % 8), (0, 0)))
    mod = _mod_call(c_pad, w_ada[0], b_ada[0]).reshape(c_pad.shape[0], 6, D)
    lb = _lb_call(lb_logits)
    row = lambda v: v.reshape(1, -1).astype(F32)
    p = dict(g1=row(norm1_g[0]), w_in=w_in[0].astype(BF16), b_in=row(b_in[0]), gn=row(hg_norm_g[0]),
             n2=row(norm2_g[0]), fin=row(final_norm_g), wa=w_branch_a[0].astype(BF16),
             wb=w_branch_b[0].astype(BF16), wo=w_out[0].astype(BF16), wfi=w_ffn_in[0].astype(BF16),
             wfo=w_ffn_out[0].astype(BF16))
    y_prompt = _trunk(x_prompt, mod[:bp], lb, p, 256, 256, 512)
    y_sample = _trunk(x_sample, mod[bp:bp + bs], lb, p, 256, 256, 512)
    return (y_prompt, y_sample)
```

```python
import jax
import jax.numpy as jnp
from jax import lax
from jax.experimental import pallas as pl
from jax.experimental.pallas import tpu as pltpu

F32 = jnp.float32
BF16 = jnp.bfloat16

NORM_EPS = 1e-6
HG_HEADS = 4
HG_DIM = 128
HG_WIDTH = HG_HEADS * HG_DIM
ATT_HEADS = 8
ATT_HDIM = 64
ATT_WIDTH = ATT_HEADS * ATT_HDIM
DILATION_PATTERNS = ((128, 1), (512, 4), (2048, 16))
ATT_RADIUS = 64
ROPE_THETA = 10000.0
LOG2E = 1.4426950408889634
PROJ_BLOCK = 512
N_PROJ_BLOCKS = 12

HG_CHUNK = 64
HG_SUB = 16
HG_SUB_SHIFT = 4
HG_EXP2_CLAMP = 110.0
EXP2_ZERO = -1e30
VMEM_LIMIT = 56 * 1024 * 1024


def _dot(a, b):
    return jnp.dot(a, b, preferred_element_type=F32)


def _dot_nt(a, b):
    return lax.dot_general(a, b, (((1,), (1,)), ((), ())), preferred_element_type=F32)


def _dot_tn(a, b):
    return lax.dot_general(a, b, (((0,), (0,)), ((), ())), preferred_element_type=F32)


def _sigmoid(x):
    return 1.0 / (1.0 + jnp.exp(-x))


def _rms(x):
    return x * lax.rsqrt(jnp.mean(x * x, axis=-1, keepdims=True) + NORM_EPS)


def _resident(shape):
    nd = len(shape)
    return pl.BlockSpec(shape, lambda *_: (0,) * nd, pipeline_mode=pl.Buffered(1))


def _mod_body(c_ref, w_ref, b_ref, o_ref):
    c = c_ref[...]
    a = c * _sigmoid(c)
    o_ref[...] = jnp.dot(a, w_ref[...], preferred_element_type=F32,
                         precision=lax.Precision.HIGHEST) + b_ref[...]


def _mod_call(c_pad, w_ada, b_ada):
    rows, d = c_pad.shape
    n = w_ada.shape[1]
    bn = 512
    return pl.pallas_call(
        _mod_body,
        grid=(n // bn,),
        in_specs=[pl.BlockSpec((rows, d), lambda j: (0, 0)),
                  pl.BlockSpec((d, bn), lambda j: (0, j)),
                  pl.BlockSpec((1, bn), lambda j: (0, j))],
        out_specs=pl.BlockSpec((rows, bn), lambda j: (0, j)),
        out_shape=jax.ShapeDtypeStruct((rows, n), F32),
        name="adaln_mod",
    )(c_pad, w_ada, b_ada.reshape(1, n))


def _lb_body(l_ref, o_ref):
    l = l_ref[...]
    e = jnp.exp(l - jnp.max(l, axis=0, keepdims=True))
    o_ref[...] = e[0:1] / jnp.sum(e, axis=0, keepdims=True)


def _lb_call(lb_logits):
    n = lb_logits.shape[0]
    flat = lb_logits.reshape(n, -1).astype(F32)
    return pl.pallas_call(
        _lb_body,
        out_shape=jax.ShapeDtypeStruct((1, flat.shape[1]), F32),
        name="hgrn_lower_bounds",
    )(flat)


def _rope(p, cos, sin_signed):
    n = p.shape[-1]
    lane = lax.broadcasted_iota(jnp.int32, p.shape, 1)
    first_half = (lane & 63) < 32
    partner = jnp.where(first_half, pltpu.roll(p, n - 32, 1), pltpu.roll(p, 32, 1))
    reps = n // cos.shape[-1]
    return p * jnp.tile(cos, (1, reps)) + partner * jnp.tile(sin_signed, (1, reps))


def _inproj_body(x_ref, mod_ref, g1_ref, w_ref, b_ref, lb_ref, cos_ref, sin_ref,
                 qh_ref, lff_ref, kf_ref, lfb_ref, kb_ref, vh_ref, sg_ref,
                 qa_ref, ka_ref, va_ref, ga_ref, gb_ref):
    x = x_ref[0]
    mod = mod_ref[0]
    shift1, scale1 = mod[0:1], mod[1:2]
    h = _rms(x) * g1_ref[...]
    h = h * (1.0 + scale1) + shift1
    hb = h.astype(BF16)

    def proj(j):
        cols = slice(j * PROJ_BLOCK, (j + 1) * PROJ_BLOCK)
        return _dot(hb, w_ref[:, cols]) + b_ref[:, cols]

    p = proj(0)
    qh_ref[0] = p * _sigmoid(p) * (HG_DIM ** -0.5)

    def gates(p, lb, lf_ref, k_ref):
        f = lb + (1.0 - lb) * _sigmoid(p)
        lf_ref[0] = jnp.log2(f)
        k_ref[0] = 1.0 - f

    gates(proj(1), lb_ref[:, 0:HG_WIDTH], lff_ref, kf_ref)
    gates(proj(2), lb_ref[:, HG_WIDTH:2 * HG_WIDTH], lfb_ref, kb_ref)
    vh_ref[0] = proj(3).astype(BF16)
    p = proj(4)
    sg_ref[0] = p * _sigmoid(p)

    cos, sin_signed = cos_ref[...], sin_ref[...]
    qa_ref[0] = (_rope(proj(5), cos, sin_signed) * (ATT_HDIM ** -0.5 * LOG2E)).astype(BF16)
    ka_ref[0] = _rope(proj(6), cos, sin_signed).astype(BF16)
    va_ref[0] = proj(7).astype(BF16)
    ga_ref[0, :, 0:PROJ_BLOCK] = _sigmoid(proj(8))
    ga_ref[0, :, PROJ_BLOCK:2 * PROJ_BLOCK] = _sigmoid(proj(9))
    gb_ref[0, :, 0:PROJ_BLOCK] = _sigmoid(proj(10))
    gb_ref[0, :, PROJ_BLOCK:2 * PROJ_BLOCK] = _sigmoid(proj(11))


def _inproj_call(x, mod, g1, w_in, b_in, lb, cos, sin_signed, tm):
    B, L, D = x.shape
    n_in = w_in.shape[1]
    assert n_in == N_PROJ_BLOCKS * PROJ_BLOCK and L % tm == 0
    row = lambda w: pl.BlockSpec((1, tm, w), lambda b, i: (b, i, 0))
    tab = pl.BlockSpec((tm, cos.shape[1]), lambda b, i: (i, 0))
    shp = lambda w, dt: jax.ShapeDtypeStruct((B, L, w), dt)
    return pl.pallas_call(
        _inproj_body,
        grid=(B, L // tm),
        in_specs=[row(D),
                  pl.BlockSpec((1,) + mod.shape[1:], lambda b, i: (b, 0, 0)),
                  _resident(g1.shape), _resident(w_in.shape), _resident(b_in.shape), _resident(lb.shape),
                  tab, tab],
        out_specs=[row(512)] * 10 + [row(1024)] * 2,
        out_shape=[shp(512, F32), shp(512, F32), shp(512, F32), shp(512, F32), shp(512, F32),
                   shp(512, BF16), shp(512, F32), shp(512, BF16), shp(512, BF16), shp(512, BF16),
                   shp(1024, F32), shp(1024, F32)],
        compiler_params=pltpu.CompilerParams(
            dimension_semantics=("parallel", "parallel"), vmem_limit_bytes=VMEM_LIMIT),
        name="inproj",
    )(x, mod, g1, w_in, b_in, lb, cos, sin_signed)


def _bcast_row(c, group, r):
    T, w = c.shape
    c3 = c.reshape(T // group, group, w)
    return jnp.broadcast_to(c3[:, r:r + 1, :], c3.shape).reshape(T, w)


def _split3(g):
    g1 = g.astype(BF16)
    r1 = g - g1.astype(F32)
    g2 = r1.astype(BF16)
    return g1, g2, (r1 - g2.astype(F32)).astype(BF16)


def _hgrn_dir(q, k, v_bf, g, s_in, rev):
    T, dk = q.shape
    C, nch, nsb = HG_CHUNK, T // HG_CHUNK, HG_CHUNK // HG_SUB
    t = lax.broadcasted_iota(jnp.int32, (C, C), 0)
    s = lax.broadcasted_iota(jnp.int32, (C, C), 1)
    order = (s >= t) if rev else (s <= t)
    inner_mask = order & ((t >> HG_SUB_SHIFT) == (s >> HG_SUB_SHIFT))
    tri = order.astype(BF16)

    g_terms = jnp.concatenate(_split3(g), axis=1)
    c_parts = []
    for ci in range(nch):
        r = _dot(tri, g_terms[ci * C:(ci + 1) * C])
        c_parts.append(r[:, 0:dk] + r[:, dk:2 * dk] + r[:, 2 * dk:3 * dk])
    c = jnp.concatenate(c_parts, axis=0)

    far = 0 if rev else HG_SUB - 1
    edge = _bcast_row(c, HG_SUB, far)
    mid = _bcast_row(c, HG_SUB, HG_SUB // 2)
    c_end = _bcast_row(c, C, 0 if rev else C - 1)
    rowc = lax.broadcasted_iota(jnp.int32, (T, dk), 0) & (C - 1)

    k_edge = k * jnp.exp2(edge - c)
    q_cat, k_cat = [], []
    for J in (range(1, nsb) if rev else range(nsb - 1)):
        ref_j = _bcast_row(c, C, J * HG_SUB + far)
        queries = (rowc < J * HG_SUB) if rev else (rowc >= (J + 1) * HG_SUB)
        q_cat.append((q * jnp.exp2(jnp.where(queries, c - ref_j, EXP2_ZERO))).astype(BF16))
        in_j = (rowc >= J * HG_SUB) & (rowc < (J + 1) * HG_SUB)
        k_cat.append(jnp.where(in_j, k_edge, 0.0).astype(BF16))
    q_cat = jnp.concatenate(q_cat, axis=1)
    k_cat = jnp.concatenate(k_cat, axis=1)

    d_mid = jnp.clip(c - mid, -HG_EXP2_CLAMP, HG_EXP2_CLAMP)
    q_mid = (q * jnp.exp2(d_mid)).astype(BF16)
    k_mid = (k * jnp.exp2(-d_mid)).astype(BF16)
    q_dec = (q * jnp.exp2(c)).astype(BF16)
    k_end = (k * jnp.exp2(c_end - c)).astype(BF16)
    decay = jnp.exp2(c_end)

    o_intra, upd = [], []
    for ci in range(nch):
        rows = slice(ci * C, (ci + 1) * C)
        sc = _dot_nt(q_cat[rows], k_cat[rows])
        sc = sc + jnp.where(inner_mask, _dot_nt(q_mid[rows], k_mid[rows]), 0.0)
        o_intra.append(_dot(sc.astype(BF16), v_bf[rows]))
        upd.append(_dot_tn(v_bf[rows], k_end[rows]))

    outs = [None] * nch
    s_t = s_in
    for ci in (reversed(range(nch)) if rev else range(nch)):
        rows = slice(ci * C, (ci + 1) * C)
        outs[ci] = o_intra[ci] + _dot_nt(q_dec[rows], s_t.astype(BF16))
        s_t = s_t * decay[ci * C:ci * C + 1] + upd[ci]
    return jnp.concatenate(outs, axis=0), s_t


def _hgrn_body(qf_ref, lff_ref, kf_ref, vf_ref, qb_ref, lfb_ref, kb_ref, vb_ref,
               of_ref, ob_ref, sf_ref, sb_ref):
    @pl.when(pl.program_id(2) == 0)
    def _():
        sf_ref[...] = jnp.zeros_like(sf_ref)
        sb_ref[...] = jnp.zeros_like(sb_ref)

    o, s_f = _hgrn_dir(qf_ref[0], kf_ref[0], vf_ref[0], lff_ref[0], sf_ref[...], False)
    of_ref[0] = o
    sf_ref[...] = s_f
    o, s_b = _hgrn_dir(qb_ref[0], kb_ref[0], vb_ref[0], lfb_ref[0], sb_ref[...], True)
    ob_ref[0] = o
    sb_ref[...] = s_b


def _hgrn_call(q, lf_f, k_f, lf_b, k_b, v, T):
    B, L, W = q.shape
    nb = L // T
    assert L % T == 0 and T % HG_CHUNK == 0 and W == HG_WIDTH
    fwd = pl.BlockSpec((1, T, HG_DIM), lambda b, h, j: (b, j, h))
    bwd = pl.BlockSpec((1, T, HG_DIM), lambda b, h, j: (b, nb - 1 - j, h))
    return pl.pallas_call(
        _hgrn_body,
        grid=(B, HG_HEADS, nb),
        in_specs=[fwd, fwd, fwd, fwd, bwd, bwd, bwd, bwd],
        out_specs=[fwd, bwd],
        out_shape=[jax.ShapeDtypeStruct((B, L, W), F32)] * 2,
        scratch_shapes=[pltpu.VMEM((HG_DIM, HG_DIM), F32)] * 2,
        compiler_params=pltpu.CompilerParams(
            dimension_semantics=("parallel", "parallel", "arbitrary"), vmem_limit_bytes=VMEM_LIMIT),
        name="hgrn_scan",
    )(q, lf_f, k_f, v, q, lf_b, k_b, v)


ATT_QBLK = 2 * ATT_RADIUS


def _attn_body(q_ref, kp_ref, kc_ref, kn_ref, vp_ref, vc_ref, vn_ref, o_ref, lse_ref):
    n = pl.program_id(2)
    last = pl.num_programs(2) - 1
    kband = jnp.concatenate([kp_ref[0], kc_ref[0], kn_ref[0]], axis=0)
    vband = jnp.concatenate([vp_ref[0], vc_ref[0], vn_ref[0]], axis=0)
    nq, nk = ATT_QBLK, ATT_QBLK + 2 * ATT_RADIUS
    i = lax.broadcasted_iota(jnp.int32, (nq, nk), 0)
    j = lax.broadcasted_iota(jnp.int32, (nq, nk), 1)
    valid = (j >= i) & (j <= i + 2 * ATT_RADIUS)
    valid = valid & ((j >= ATT_RADIUS) | (n > 0)) & ((j < nq + ATT_RADIUS) | (n < last))
    q = q_ref[0]
    pair_w = 2 * ATT_HDIM
    npairs = ATT_HEADS // 2
    lo_k = lax.broadcasted_iota(jnp.int32, (nk, pair_w), 1) < ATT_HDIM
    lo_q = lax.broadcasted_iota(jnp.int32, (nq, pair_w), 1) < ATT_HDIM
    ones_a = jnp.where(lo_k, 1.0, 0.0).astype(BF16)
    ones_b = jnp.where(lo_k, 0.0, 1.0).astype(BF16)
    scores = []
    for i in range(npairs):
        cols = slice(i * pair_w, (i + 1) * pair_w)
        kp = kband[:, cols]
        scores.append(_dot_nt(q[:, cols], kp * ones_a))
        scores.append(_dot_nt(q[:, cols], kp * ones_b))
    scores = [jnp.where(valid, s, -jnp.inf) for s in scores]
    tops = [jnp.max(s, axis=-1, keepdims=True) for s in scores]
    probs = [jnp.exp2(s - m).astype(BF16) for s, m in zip(scores, tops)]
    for i in range(npairs):
        cols = slice(i * pair_w, (i + 1) * pair_w)
        vp = vband[:, cols]
        v_a = jnp.concatenate([vp * ones_a, ones_a], axis=1)
        v_b = jnp.concatenate([vp * ones_b, ones_b], axis=1)
        res = _dot(probs[2 * i], v_a) + _dot(probs[2 * i + 1], v_b)
        num, den = res[:, 0:pair_w], res[:, pair_w:2 * pair_w]
        o_ref[0, :, cols] = (num / den).astype(o_ref.dtype)
        lse_ref[0, :, cols] = jnp.where(lo_q, tops[2 * i], tops[2 * i + 1]) + jnp.log2(den)


def _attn_call(q, k, v, r):
    B, L, W = q.shape
    Lr = L // r
    assert L % r == 0 and Lr % ATT_QBLK == 0 and Lr // ATT_QBLK >= 2
    view = lambda t: t.reshape(B, Lr, r * W)
    nhalf = Lr // ATT_RADIUS
    center = pl.BlockSpec((1, ATT_QBLK, W), lambda b, c, n: (b, n, c))
    prev = pl.BlockSpec((1, ATT_RADIUS, W), lambda b, c, n: (b, jnp.maximum(2 * n - 1, 0), c))
    nxt = pl.BlockSpec((1, ATT_RADIUS, W), lambda b, c, n: (b, jnp.minimum(2 * n + 2, nhalf - 1), c))
    out, lse = pl.pallas_call(
        _attn_body,
        grid=(B, r, Lr // ATT_QBLK),
        in_specs=[center, prev, center, nxt, prev, center, nxt],
        out_specs=[center, center],
        out_shape=[jax.ShapeDtypeStruct((B, Lr, r * W), BF16), jax.ShapeDtypeStruct((B, Lr, r * W), F32)],
        compiler_params=pltpu.CompilerParams(
            dimension_semantics=("parallel", "parallel", "parallel"), vmem_limit_bytes=VMEM_LIMIT),
        name=f"dilated_attn_r{r}",
    )(view(q), view(k), view(k), view(k), view(v), view(v), view(v))
    return out.reshape(B, L, W), lse.reshape(B, L, W)


FFN_CHUNK = 256


def _merge_body(x_ref, mod_ref, of_ref, ob_ref, sg_ref, a1_ref, l1_ref, a2_ref, l2_ref, a3_ref, l3_ref,
                ga_ref, gb_ref, gn_ref, n2_ref, fin_ref, wa_ref, wb_ref, wo_ref, wfi_ref, wfo_ref,
                y_ref, act_ref):
    mod = mod_ref[0]
    gate1, shift2, scale2, gate2 = mod[2:3], mod[3:4], mod[4:5], mod[5:6]

    o = of_ref[0] + ob_ref[0]
    o_a = jnp.concatenate([_rms(o[:, h * HG_DIM:(h + 1) * HG_DIM]) for h in range(HG_HEADS)], axis=1)
    o_a = o_a * gn_ref[...] * sg_ref[0]

    l1, l2, l3 = l1_ref[0], l2_ref[0], l3_ref[0]
    top = jnp.maximum(jnp.maximum(l1, l2), l3)
    w1, w2, w3 = jnp.exp2(l1 - top), jnp.exp2(l2 - top), jnp.exp2(l3 - top)
    o_b = (a1_ref[0] * w1 + a2_ref[0] * w2 + a3_ref[0] * w3) / (w1 + w2 + w3)

    merged = (ga_ref[0] * _dot(o_a.astype(BF16), wa_ref[...])
              + gb_ref[0] * _dot(o_b.astype(BF16), wb_ref[...]))
    x1 = x_ref[0] + gate1 * _dot(merged.astype(BF16), wo_ref[...])

    h2 = (_rms(x1) * n2_ref[...] * (1.0 + scale2) + shift2).astype(BF16)
    hidden = wfo_ref.shape[0]
    for c0 in range(0, hidden, FFN_CHUNK):
        gt = _dot(h2, wfi_ref[:, c0:c0 + FFN_CHUNK])
        up = _dot(h2, wfi_ref[:, hidden + c0:hidden + c0 + FFN_CHUNK])
        act_ref[:, c0:c0 + FFN_CHUNK] = (gt * _sigmoid(gt) * up).astype(BF16)
    x2 = x1 + gate2 * _dot(act_ref[...], wfo_ref[...])
    y_ref[0] = _rms(x2) * fin_ref[...]


def _merge_call(x, mod, o_f, o_b, sg, att, ga, gb, gn, n2, fin, wa, wb, wo, wfi, wfo, tm):
    B, L, D = x.shape
    hidden = wfo.shape[0]
    assert L % tm == 0 and hidden % FFN_CHUNK == 0 and wfi.shape[1] == 2 * hidden
    row = lambda w: pl.BlockSpec((1, tm, w), lambda b, i: (b, i, 0))
    (a1, l1), (a2, l2), (a3, l3) = att
    return pl.pallas_call(
        _merge_body,
        grid=(B, L // tm),
        in_specs=[row(D), pl.BlockSpec((1,) + mod.shape[1:], lambda b, i: (b, 0, 0))]
                 + [row(512)] * 9 + [row(D), row(D)]
                 + [_resident(t.shape) for t in (gn, n2, fin, wa, wb, wo, wfi, wfo)],
        out_specs=row(D),
        out_shape=jax.ShapeDtypeStruct((B, L, D), F32),
        scratch_shapes=[pltpu.VMEM((tm, hidden), BF16)],
        compiler_params=pltpu.CompilerParams(
            dimension_semantics=("parallel", "parallel"), vmem_limit_bytes=VMEM_LIMIT),
        name="merge_ffn",
    )(x, mod, o_f, o_b, sg, a1, l1, a2, l2, a3, l3, ga, gb, gn, n2, fin, wa, wb, wo, wfi, wfo)


def _rope_tables(L):
    half = ATT_HDIM // 2
    pos = jnp.arange(L, dtype=F32)
    inv = ROPE_THETA ** (-jnp.arange(half, dtype=F32) / half)
    ang = pos[:, None] * inv[None, :]
    cos, sin = jnp.cos(ang), jnp.sin(ang)
    return jnp.tile(jnp.concatenate([cos, cos], axis=1), (1, 2)), jnp.tile(jnp.concatenate([-sin, sin], axis=1), (1, 2))


def _trunk(x, mod, lb, p, tm_in, tm_out, t_scan):
    B, L, D = x.shape
    cos, sin_signed = _rope_tables(L)
    (q_h, lf_f, k_f, lf_b, k_b, v_h, sg, q_a, k_a, v_a, ga, gb) = _inproj_call(
        x, mod, p["g1"], p["w_in"], p["b_in"], lb, cos, sin_signed, tm_in)
    o_f, o_b = _hgrn_call(q_h, lf_f, k_f, lf_b, k_b, v_h, t_scan)
    att = [_attn_call(q_a, k_a, v_a, r) for (_, r) in DILATION_PATTERNS]
    return _merge_call(x, mod, o_f, o_b, sg, att, ga, gb, p["gn"], p["n2"], p["fin"],
                       p["wa"], p["wb"], p["wo"], p["wfi"], p["wfo"], tm_out)


def kernel(x_prompt, x_sample, c_prompt, c_sample, w_ada, b_ada, norm1_g, w_in, b_in, lb_logits, hg_norm_g, w_branch_a, w_branch_b, w_out, norm2_g, w_ffn_in, w_ffn_out, final_norm_g):
    assert w_ada.shape[0] == 1 and lb_logits.shape[0] == 2, "single-layer trunk"
    D = x_prompt.shape[-1]
    bp, bs = c_prompt.shape[0], c_sample.shape[0]
    c_all = jnp.concatenate([c_prompt, c_sample], axis=0)
    c_pad = jnp.pad(c_all, ((0, -(bp + bs) % 8), (0, 0)))
    mod = _mod_call(c_pad, w_ada[0], b_ada[0]).reshape(c_pad.shape[0], 6, D)
    lb = _lb_call(lb_logits)
    row = lambda v: v.reshape(1, -1).astype(F32)
    p = dict(g1=row(norm1_g[0]), w_in=w_in[0].astype(BF16), b_in=row(b_in[0]), gn=row(hg_norm_g[0]),
             n2=row(norm2_g[0]), fin=row(final_norm_g), wa=w_branch_a[0].astype(BF16),
             wb=w_branch_b[0].astype(BF16), wo=w_out[0].astype(BF16), wfi=w_ffn_in[0].astype(BF16),
             wfo=w_ffn_out[0].astype(BF16))
    y_prompt = _trunk(x_prompt, mod[:bp], lb, p, 256, 256, 512)
    y_sample = _trunk(x_sample, mod[bp:bp + bs], lb, p, 256, 256, 512)
    return (y_prompt, y_sample)
```

```python
import jax
import jax.numpy as jnp
from jax import lax
from jax.experimental import pallas as pl
from jax.experimental.pallas import tpu as pltpu

F32 = jnp.float32
BF16 = jnp.bfloat16

NORM_EPS = 1e-6
HG_HEADS = 4
HG_DIM = 128
HG_WIDTH = HG_HEADS * HG_DIM
ATT_HEADS = 8
ATT_HDIM = 64
ATT_WIDTH = ATT_HEADS * ATT_HDIM
DILATION_PATTERNS = ((128, 1), (512, 4), (2048, 16))
ATT_RADIUS = 64
ROPE_THETA = 10000.0
LOG2E = 1.4426950408889634
PROJ_BLOCK = 512
N_PROJ_BLOCKS = 12

HG_CHUNK = 64
HG_SUB = 16
HG_SUB_SHIFT = 4
HG_EXP2_CLAMP = 110.0
EXP2_ZERO = -1e30
VMEM_LIMIT = 56 * 1024 * 1024


def _dot(a, b):
    return jnp.dot(a, b, preferred_element_type=F32)


def _dot_nt(a, b):
    return lax.dot_general(a, b, (((1,), (1,)), ((), ())), preferred_element_type=F32)


def _dot_tn(a, b):
    return lax.dot_general(a, b, (((0,), (0,)), ((), ())), preferred_element_type=F32)


def _sigmoid(x):
    return 1.0 / (1.0 + jnp.exp(-x))


def _rms(x):
    return x * lax.rsqrt(jnp.mean(x * x, axis=-1, keepdims=True) + NORM_EPS)


def _resident(shape):
    nd = len(shape)
    return pl.BlockSpec(shape, lambda *_: (0,) * nd, pipeline_mode=pl.Buffered(1))


def _mod_body(c_ref, w_ref, b_ref, o_ref):
    c = c_ref[...]
    a = c * _sigmoid(c)
    o_ref[...] = jnp.dot(a, w_ref[...], preferred_element_type=F32,
                         precision=lax.Precision.HIGHEST) + b_ref[...]


def _mod_call(c_pad, w_ada, b_ada):
    rows, d = c_pad.shape
    n = w_ada.shape[1]
    bn = 512
    return pl.pallas_call(
        _mod_body,
        grid=(n // bn,),
        in_specs=[pl.BlockSpec((rows, d), lambda j: (0, 0)),
                  pl.BlockSpec((d, bn), lambda j: (0, j)),
                  pl.BlockSpec((1, bn), lambda j: (0, j))],
        out_specs=pl.BlockSpec((rows, bn), lambda j: (0, j)),
        out_shape=jax.ShapeDtypeStruct((rows, n), F32),
        name="adaln_mod",
    )(c_pad, w_ada, b_ada.reshape(1, n))


def _lb_body(l_ref, o_ref):
    l = l_ref[...]
    e = jnp.exp(l - jnp.max(l, axis=0, keepdims=True))
    o_ref[...] = e[0:1] / jnp.sum(e, axis=0, keepdims=True)


def _lb_call(lb_logits):
    n = lb_logits.shape[0]
    flat = lb_logits.reshape(n, -1).astype(F32)
    return pl.pallas_call(
        _lb_body,
        out_shape=jax.ShapeDtypeStruct((1, flat.shape[1]), F32),
        name="hgrn_lower_bounds",
    )(flat)


def _rope(p, cos, sin_signed):
    n = p.shape[-1]
    lane = lax.broadcasted_iota(jnp.int32, p.shape, 1)
    first_half = (lane & 63) < 32
    partner = jnp.where(first_half, pltpu.roll(p, n - 32, 1), pltpu.roll(p, 32, 1))
    reps = n // cos.shape[-1]
    return p * jnp.tile(cos, (1, reps)) + partner * jnp.tile(sin_signed, (1, reps))


def _store_dilated(t_bf, perm_ref, out_ref, r):
    if r == 1:
        out_ref[0] = t_bf
        return
    tm, w = t_bf.shape
    grouped = _dot(perm_ref[...], t_bf).astype(BF16)
    rows = tm // r
    for c in range(r):
        out_ref[0, :, c * w:(c + 1) * w] = grouped[c * rows:(c + 1) * rows]


def _inproj_body(x_ref, mod_ref, g1_ref, w_ref, b_ref, lb_ref, cos_ref, sin_ref, p4_ref, p16_ref,
                 qh_ref, lff_ref, kf_ref, lfb_ref, kb_ref, vh_ref, sg_ref, ga_ref, gb_ref,
                 q1_ref, k1_ref, v1_ref, q4_ref, k4_ref, v4_ref, q16_ref, k16_ref, v16_ref):
    x = x_ref[0]
    mod = mod_ref[0]
    shift1, scale1 = mod[0:1], mod[1:2]
    h = _rms(x) * g1_ref[...]
    h = h * (1.0 + scale1) + shift1
    hb = h.astype(BF16)

    def proj(j):
        cols = slice(j * PROJ_BLOCK, (j + 1) * PROJ_BLOCK)
        return _dot(hb, w_ref[:, cols]) + b_ref[:, cols]

    p = proj(0)
    qh_ref[0] = p * _sigmoid(p) * (HG_DIM ** -0.5)

    def gates(p, lb, lf_ref, k_ref):
        f = lb + (1.0 - lb) * _sigmoid(p)
        lf_ref[0] = jnp.log2(f)
        k_ref[0] = 1.0 - f

    gates(proj(1), lb_ref[:, 0:HG_WIDTH], lff_ref, kf_ref)
    gates(proj(2), lb_ref[:, HG_WIDTH:2 * HG_WIDTH], lfb_ref, kb_ref)
    vh_ref[0] = proj(3).astype(BF16)
    p = proj(4)
    sg_ref[0] = p * _sigmoid(p)

    cos, sin_signed = cos_ref[...], sin_ref[...]
    q_a = (_rope(proj(5), cos, sin_signed) * (ATT_HDIM ** -0.5 * LOG2E)).astype(BF16)
    k_a = _rope(proj(6), cos, sin_signed).astype(BF16)
    v_a = proj(7).astype(BF16)
    for t, refs in ((q_a, (q1_ref, q4_ref, q16_ref)), (k_a, (k1_ref, k4_ref, k16_ref)),
                    (v_a, (v1_ref, v4_ref, v16_ref))):
        for (_, r), perm_ref, out_ref in zip(DILATION_PATTERNS, (None, p4_ref, p16_ref), refs):
            _store_dilated(t, perm_ref, out_ref, r)
    ga_ref[0, :, 0:PROJ_BLOCK] = _sigmoid(proj(8))
    ga_ref[0, :, PROJ_BLOCK:2 * PROJ_BLOCK] = _sigmoid(proj(9))
    gb_ref[0, :, 0:PROJ_BLOCK] = _sigmoid(proj(10))
    gb_ref[0, :, PROJ_BLOCK:2 * PROJ_BLOCK] = _sigmoid(proj(11))


def _dilation_perm(tm, r):
    i = jnp.arange(tm)
    src = r * (i % (tm // r)) + i // (tm // r)
    return (src[:, None] == jnp.arange(tm)[None, :]).astype(BF16)


def _inproj_call(x, mod, g1, w_in, b_in, lb, cos, sin_signed, tm):
    B, L, D = x.shape
    n_in = w_in.shape[1]
    assert n_in == N_PROJ_BLOCKS * PROJ_BLOCK and L % tm == 0
    row = lambda w, r=1: pl.BlockSpec((1, tm // r, r * w), lambda b, i: (b, i, 0))
    tab = pl.BlockSpec((tm, cos.shape[1]), lambda b, i: (i, 0))
    shp = lambda w, dt, r=1: jax.ShapeDtypeStruct((B, L // r, r * w), dt)
    perms = [_dilation_perm(tm, r) for (_, r) in DILATION_PATTERNS[1:]]
    att_specs = [row(ATT_WIDTH, r) for (_, r) in DILATION_PATTERNS for _ in range(3)]
    att_shapes = [shp(ATT_WIDTH, BF16, r) for (_, r) in DILATION_PATTERNS for _ in range(3)]
    return pl.pallas_call(
        _inproj_body,
        grid=(B, L // tm),
        in_specs=[row(D),
                  pl.BlockSpec((1,) + mod.shape[1:], lambda b, i: (b, 0, 0)),
                  _resident(g1.shape), _resident(w_in.shape), _resident(b_in.shape), _resident(lb.shape),
                  tab, tab] + [_resident(p.shape) for p in perms],
        out_specs=[row(512)] * 7 + [row(1024)] * 2 + att_specs,
        out_shape=[shp(512, F32), shp(512, F32), shp(512, F32), shp(512, F32), shp(512, F32),
                   shp(512, BF16), shp(512, F32), shp(1024, F32), shp(1024, F32)] + att_shapes,
        compiler_params=pltpu.CompilerParams(
            dimension_semantics=("parallel", "parallel"), vmem_limit_bytes=VMEM_LIMIT),
        name="inproj",
    )(x, mod, g1, w_in, b_in, lb, cos, sin_signed, *perms)


def _bcast_row(c, group, r):
    T, w = c.shape
    c3 = c.reshape(T // group, group, w)
    return jnp.broadcast_to(c3[:, r:r + 1, :], c3.shape).reshape(T, w)


def _split3(g):
    g1 = g.astype(BF16)
    r1 = g - g1.astype(F32)
    g2 = r1.astype(BF16)
    return g1, g2, (r1 - g2.astype(F32)).astype(BF16)


def _hgrn_dir(q, k, v_bf, g, s_in, rev):
    T, dk = q.shape
    C, nch, nsb = HG_CHUNK, T // HG_CHUNK, HG_CHUNK // HG_SUB
    t = lax.broadcasted_iota(jnp.int32, (C, C), 0)
    s = lax.broadcasted_iota(jnp.int32, (C, C), 1)
    order = (s >= t) if rev else (s <= t)
    inner_mask = order & ((t >> HG_SUB_SHIFT) == (s >> HG_SUB_SHIFT))
    tri = order.astype(BF16)

    g_terms = jnp.concatenate(_split3(g), axis=1)
    c_parts = []
    for ci in range(nch):
        r = _dot(tri, g_terms[ci * C:(ci + 1) * C])
        c_parts.append(r[:, 0:dk] + r[:, dk:2 * dk] + r[:, 2 * dk:3 * dk])
    c = jnp.concatenate(c_parts, axis=0)

    far = 0 if rev else HG_SUB - 1
    edge = _bcast_row(c, HG_SUB, far)
    mid = _bcast_row(c, HG_SUB, HG_SUB // 2)
    c_end = _bcast_row(c, C, 0 if rev else C - 1)
    rowc = lax.broadcasted_iota(jnp.int32, (T, dk), 0) & (C - 1)

    k_edge = k * jnp.exp2(edge - c)
    q_cat, k_cat = [], []
    for J in (range(1, nsb) if rev else range(nsb - 1)):
        ref_j = _bcast_row(c, C, J * HG_SUB + far)
        queries = (rowc < J * HG_SUB) if rev else (rowc >= (J + 1) * HG_SUB)
        q_cat.append((q * jnp.exp2(jnp.where(queries, c - ref_j, EXP2_ZERO))).astype(BF16))
        in_j = (rowc >= J * HG_SUB) & (rowc < (J + 1) * HG_SUB)
        k_cat.append(jnp.where(in_j, k_edge, 0.0).astype(BF16))
    q_cat = jnp.concatenate(q_cat, axis=1)
    k_cat = jnp.concatenate(k_cat, axis=1)

    d_mid = jnp.clip(c - mid, -HG_EXP2_CLAMP, HG_EXP2_CLAMP)
    q_mid = (q * jnp.exp2(d_mid)).astype(BF16)
    k_mid = (k * jnp.exp2(-d_mid)).astype(BF16)
    q_dec = (q * jnp.exp2(c)).astype(BF16)
    k_end = (k * jnp.exp2(c_end - c)).astype(BF16)
    decay = jnp.exp2(c_end)

    o_intra, upd = [], []
    for ci in range(nch):
        rows = slice(ci * C, (ci + 1) * C)
        sc = _dot_nt(q_cat[rows], k_cat[rows])
        sc = sc + jnp.where(inner_mask, _dot_nt(q_mid[rows], k_mid[rows]), 0.0)
        o_intra.append(_dot(sc.astype(BF16), v_bf[rows]))
        upd.append(_dot_tn(v_bf[rows], k_end[rows]))

    outs = [None] * nch
    s_t = s_in
    for ci in (reversed(range(nch)) if rev else range(nch)):
        rows = slice(ci * C, (ci + 1) * C)
        outs[ci] = o_intra[ci] + _dot_nt(q_dec[rows], s_t.astype(BF16))
        s_t = s_t * decay[ci * C:ci * C + 1] + upd[ci]
    return jnp.concatenate(outs, axis=0), s_t


def _hgrn_body(qf_ref, lff_ref, kf_ref, vf_ref, qb_ref, lfb_ref, kb_ref, vb_ref,
               of_ref, ob_ref, sf_ref, sb_ref):
    @pl.when(pl.program_id(2) == 0)
    def _():
        sf_ref[...] = jnp.zeros_like(sf_ref)
        sb_ref[...] = jnp.zeros_like(sb_ref)

    o, s_f = _hgrn_dir(qf_ref[0], kf_ref[0], vf_ref[0], lff_ref[0], sf_ref[...], False)
    of_ref[0] = o
    sf_ref[...] = s_f
    o, s_b = _hgrn_dir(qb_ref[0], kb_ref[0], vb_ref[0], lfb_ref[0], sb_ref[...], True)
    ob_ref[0] = o
    sb_ref[...] = s_b


def _hgrn_call(q, lf_f, k_f, lf_b, k_b, v, T):
    B, L, W = q.shape
    nb = L // T
    assert L % T == 0 and T % HG_CHUNK == 0 and W == HG_WIDTH
    fwd = pl.BlockSpec((1, T, HG_DIM), lambda b, h, j: (b, j, h))
    bwd = pl.BlockSpec((1, T, HG_DIM), lambda b, h, j: (b, nb - 1 - j, h))
    return pl.pallas_call(
        _hgrn_body,
        grid=(B, HG_HEADS, nb),
        in_specs=[fwd, fwd, fwd, fwd, bwd, bwd, bwd, bwd],
        out_specs=[fwd, bwd],
        out_shape=[jax.ShapeDtypeStruct((B, L, W), F32)] * 2,
        scratch_shapes=[pltpu.VMEM((HG_DIM, HG_DIM), F32)] * 2,
        compiler_params=pltpu.CompilerParams(
            dimension_semantics=("parallel", "parallel", "arbitrary"), vmem_limit_bytes=VMEM_LIMIT),
        name="hgrn_scan",
    )(q, lf_f, k_f, v, q, lf_b, k_b, v)


ATT_QBLK = 2 * ATT_RADIUS


def _attn_body(q_ref, kp_ref, kc_ref, kn_ref, vp_ref, vc_ref, vn_ref, o_ref, lse_ref):
    n = pl.program_id(2)
    last = pl.num_programs(2) - 1
    kband = jnp.concatenate([kp_ref[0], kc_ref[0], kn_ref[0]], axis=0)
    vband = jnp.concatenate([vp_ref[0], vc_ref[0], vn_ref[0]], axis=0)
    nq, nk = ATT_QBLK, ATT_QBLK + 2 * ATT_RADIUS
    i = lax.broadcasted_iota(jnp.int32, (nq, nk), 0)
    j = lax.broadcasted_iota(jnp.int32, (nq, nk), 1)
    valid = (j >= i) & (j <= i + 2 * ATT_RADIUS)
    valid = valid & ((j >= ATT_RADIUS) | (n > 0)) & ((j < nq + ATT_RADIUS) | (n < last))
    q = q_ref[0]
    pair_w = 2 * ATT_HDIM
    npairs = ATT_HEADS // 2
    lo_k = lax.broadcasted_iota(jnp.int32, (nk, pair_w), 1) < ATT_HDIM
    lo_q = lax.broadcasted_iota(jnp.int32, (nq, pair_w), 1) < ATT_HDIM
    ones_a = jnp.where(lo_k, 1.0, 0.0).astype(BF16)
    ones_b = jnp.where(lo_k, 0.0, 1.0).astype(BF16)
    scores = []
    for i in range(npairs):
        cols = slice(i * pair_w, (i + 1) * pair_w)
        kp = kband[:, cols]
        scores.append(_dot_nt(q[:, cols], kp * ones_a))
        scores.append(_dot_nt(q[:, cols], kp * ones_b))
    scores = [jnp.where(valid, s, -jnp.inf) for s in scores]
    tops = [jnp.max(s, axis=-1, keepdims=True) for s in scores]
    probs = [jnp.exp2(s - m).astype(BF16) for s, m in zip(scores, tops)]
    for i in range(npairs):
        cols = slice(i * pair_w, (i + 1) * pair_w)
        vp = vband[:, cols]
        v_a = jnp.concatenate([vp * ones_a, ones_a], axis=1)
        v_b = jnp.concatenate([vp * ones_b, ones_b], axis=1)
        res = _dot(probs[2 * i], v_a) + _dot(probs[2 * i + 1], v_b)
        num, den = res[:, 0:pair_w], res[:, pair_w:2 * pair_w]
        o_ref[0, :, cols] = (num / den).astype(o_ref.dtype)
        lse_ref[0, :, cols] = jnp.where(lo_q, tops[2 * i], tops[2 * i + 1]) + jnp.log2(den)


def _attn_call(q, k, v, r):
    B, Lr, rW = q.shape
    W = rW // r
    assert W == ATT_WIDTH and Lr % ATT_QBLK == 0 and Lr // ATT_QBLK >= 2
    nhalf = Lr // ATT_RADIUS
    center = pl.BlockSpec((1, ATT_QBLK, W), lambda b, c, n: (b, n, c))
    prev = pl.BlockSpec((1, ATT_RADIUS, W), lambda b, c, n: (b, jnp.maximum(2 * n - 1, 0), c))
    nxt = pl.BlockSpec((1, ATT_RADIUS, W), lambda b, c, n: (b, jnp.minimum(2 * n + 2, nhalf - 1), c))
    return pl.pallas_call(
        _attn_body,
        grid=(B, r, Lr // ATT_QBLK),
        in_specs=[center, prev, center, nxt, prev, center, nxt],
        out_specs=[center, center],
        out_shape=[jax.ShapeDtypeStruct((B, Lr, rW), BF16), jax.ShapeDtypeStruct((B, Lr, rW), F32)],
        compiler_params=pltpu.CompilerParams(
            dimension_semantics=("parallel", "parallel", "parallel"), vmem_limit_bytes=VMEM_LIMIT),
        name=f"dilated_attn_r{r}",
    )(q, k, k, k, v, v, v)


FFN_CHUNK = 256


def _load_natural(a_ref, l_ref, perm_t_ref, r):
    if r == 1:
        return a_ref[0].astype(F32), l_ref[0]
    w = a_ref.shape[2] // r
    a = jnp.concatenate([a_ref[0, :, c * w:(c + 1) * w] for c in range(r)], axis=0)
    l = jnp.concatenate([l_ref[0, :, c * w:(c + 1) * w] for c in range(r)], axis=0)
    l_hi = l.astype(BF16)
    l_lo = (l - l_hi.astype(F32)).astype(BF16)
    nat = _dot(perm_t_ref[...], jnp.concatenate([a, l_hi, l_lo], axis=1))
    return nat[:, 0:w], nat[:, w:2 * w] + nat[:, 2 * w:3 * w]


def _merge_body(x_ref, mod_ref, of_ref, ob_ref, sg_ref, a1_ref, l1_ref, a2_ref, l2_ref, a3_ref, l3_ref,
                ga_ref, gb_ref, gn_ref, n2_ref, fin_ref, wa_ref, wb_ref, wo_ref, wfi_ref, wfo_ref,
                p4t_ref, p16t_ref, y_ref, act_ref):
    mod = mod_ref[0]
    gate1, shift2, scale2, gate2 = mod[2:3], mod[3:4], mod[4:5], mod[5:6]

    o = of_ref[0] + ob_ref[0]
    o_a = jnp.concatenate([_rms(o[:, h * HG_DIM:(h + 1) * HG_DIM]) for h in range(HG_HEADS)], axis=1)
    o_a = o_a * gn_ref[...] * sg_ref[0]

    (a1, l1), (a2, l2), (a3, l3) = [
        _load_natural(a_ref, l_ref, perm_ref, r)
        for (_, r), a_ref, l_ref, perm_ref in zip(DILATION_PATTERNS, (a1_ref, a2_ref, a3_ref),
                                                  (l1_ref, l2_ref, l3_ref), (None, p4t_ref, p16t_ref))]
    top = jnp.maximum(jnp.maximum(l1, l2), l3)
    w1, w2, w3 = jnp.exp2(l1 - top), jnp.exp2(l2 - top), jnp.exp2(l3 - top)
    o_b = (a1 * w1 + a2 * w2 + a3 * w3) / (w1 + w2 + w3)

    merged = (ga_ref[0] * _dot(o_a.astype(BF16), wa_ref[...])
              + gb_ref[0] * _dot(o_b.astype(BF16), wb_ref[...]))
    x1 = x_ref[0] + gate1 * _dot(merged.astype(BF16), wo_ref[...])

    h2 = (_rms(x1) * n2_ref[...] * (1.0 + scale2) + shift2).astype(BF16)
    hidden = wfo_ref.shape[0]
    for c0 in range(0, hidden, FFN_CHUNK):
        gt = _dot(h2, wfi_ref[:, c0:c0 + FFN_CHUNK])
        up = _dot(h2, wfi_ref[:, hidden + c0:hidden + c0 + FFN_CHUNK])
        act_ref[:, c0:c0 + FFN_CHUNK] = (gt * _sigmoid(gt) * up).astype(BF16)
    x2 = x1 + gate2 * _dot(act_ref[...], wfo_ref[...])
    y_ref[0] = _rms(x2) * fin_ref[...]


def _merge_call(x, mod, o_f, o_b, sg, att, ga, gb, gn, n2, fin, wa, wb, wo, wfi, wfo, tm):
    B, L, D = x.shape
    hidden = wfo.shape[0]
    assert L % tm == 0 and hidden % FFN_CHUNK == 0 and wfi.shape[1] == 2 * hidden
    row = lambda w, r=1: pl.BlockSpec((1, tm // r, r * w), lambda b, i: (b, i, 0))
    att_flat = [t for pair in att for t in pair]
    att_specs = [row(ATT_WIDTH, r) for (_, r) in DILATION_PATTERNS for _ in range(2)]
    perms_t = [_dilation_perm(tm, r).T for (_, r) in DILATION_PATTERNS[1:]]
    return pl.pallas_call(
        _merge_body,
        grid=(B, L // tm),
        in_specs=[row(D), pl.BlockSpec((1,) + mod.shape[1:], lambda b, i: (b, 0, 0))]
                 + [row(512)] * 3 + att_specs + [row(D), row(D)]
                 + [_resident(t.shape) for t in (gn, n2, fin, wa, wb, wo, wfi, wfo, *perms_t)],
        out_specs=row(D),
        out_shape=jax.ShapeDtypeStruct((B, L, D), F32),
        scratch_shapes=[pltpu.VMEM((tm, hidden), BF16)],
        compiler_params=pltpu.CompilerParams(
            dimension_semantics=("parallel", "parallel"), vmem_limit_bytes=VMEM_LIMIT),
        name="merge_ffn",
    )(x, mod, o_f, o_b, sg, *att_flat, ga, gb, gn, n2, fin, wa, wb, wo, wfi, wfo, *perms_t)


def _rope_tables(L):
    half = ATT_HDIM // 2
    pos = jnp.arange(L, dtype=F32)
    inv = ROPE_THETA ** (-jnp.arange(half, dtype=F32) / half)
    ang = pos[:, None] * inv[None, :]
    cos, sin = jnp.cos(ang), jnp.sin(ang)
    return jnp.tile(jnp.concatenate([cos, cos], axis=1), (1, 2)), jnp.tile(jnp.concatenate([-sin, sin], axis=1), (1, 2))


def _trunk(x, mod, lb, p, tm_in, tm_out, t_scan):
    B, L, D = x.shape
    cos, sin_signed = _rope_tables(L)
    (q_h, lf_f, k_f, lf_b, k_b, v_h, sg, ga, gb, *qkv) = _inproj_call(
        x, mod, p["g1"], p["w_in"], p["b_in"], lb, cos, sin_signed, tm_in)
    o_f, o_b = _hgrn_call(q_h, lf_f, k_f, lf_b, k_b, v_h, t_scan)
    att = [_attn_call(*qkv[3 * i:3 * i + 3], r) for i, (_, r) in enumerate(DILATION_PATTERNS)]
    return _merge_call(x, mod, o_f, o_b, sg, att, ga, gb, p["gn"], p["n2"], p["fin"],
                       p["wa"], p["wb"], p["wo"], p["wfi"], p["wfo"], tm_out)


def kernel(x_prompt, x_sample, c_prompt, c_sample, w_ada, b_ada, norm1_g, w_in, b_in, lb_logits, hg_norm_g, w_branch_a, w_branch_b, w_out, norm2_g, w_ffn_in, w_ffn_out, final_norm_g):
    assert w_ada.shape[0] == 1 and lb_logits.shape[0] == 2, "single-layer trunk"
    D = x_prompt.shape[-1]
    bp, bs = c_prompt.shape[0], c_sample.shape[0]
    c_all = jnp.concatenate([c_prompt, c_sample], axis=0)
    c_pad = jnp.pad(c_all, ((0, -(bp + bs) % 8), (0, 0)))
    mod = _mod_call(c_pad, w_ada[0], b_ada[0]).reshape(c_pad.shape[0], 6, D)
    lb = _lb_call(lb_logits)
    row = lambda v: v.reshape(1, -1).astype(F32)
    p = dict(g1=row(norm1_g[0]), w_in=w_in[0].astype(BF16), b_in=row(b_in[0]), gn=row(hg_norm_g[0]),
             n2=row(norm2_g[0]), fin=row(final_norm_g), wa=w_branch_a[0].astype(BF16),
             wb=w_branch_b[0].astype(BF16), wo=w_out[0].astype(BF16), wfi=w_ffn_in[0].astype(BF16),
             wfo=w_ffn_out[0].astype(BF16))
    y_prompt = _trunk(x_prompt, mod[:bp], lb, p, 256, 256, 512)
    y_sample = _trunk(x_sample, mod[bp:bp + bs], lb, p, 256, 256, 512)
    return (y_prompt, y_sample)
```

```python
import jax
import jax.numpy as jnp
from jax import lax
from jax.experimental import pallas as pl
from jax.experimental.pallas import tpu as pltpu

F32 = jnp.float32
BF16 = jnp.bfloat16

NORM_EPS = 1e-6
HG_HEADS = 4
HG_DIM = 128
HG_WIDTH = HG_HEADS * HG_DIM
ATT_HEADS = 8
ATT_HDIM = 64
ATT_WIDTH = ATT_HEADS * ATT_HDIM
DILATION_PATTERNS = ((128, 1), (512, 4), (2048, 16))
ATT_RADIUS = 64
ROPE_THETA = 10000.0
LOG2E = 1.4426950408889634
PROJ_BLOCK = 512
N_PROJ_BLOCKS = 12

HG_CHUNK = 64
HG_SUB = 16
HG_SUB_SHIFT = 4
HG_EXP2_CLAMP = 110.0
EXP2_ZERO = -1e30
VMEM_LIMIT = 56 * 1024 * 1024


def _dot(a, b):
    return jnp.dot(a, b, preferred_element_type=F32)


def _dot_nt(a, b):
    return lax.dot_general(a, b, (((1,), (1,)), ((), ())), preferred_element_type=F32)


def _dot_tn(a, b):
    return lax.dot_general(a, b, (((0,), (0,)), ((), ())), preferred_element_type=F32)


def _sigmoid(x):
    return 1.0 / (1.0 + jnp.exp(-x))


def _rms(x):
    return x * lax.rsqrt(jnp.mean(x * x, axis=-1, keepdims=True) + NORM_EPS)


def _resident(shape):
    nd = len(shape)
    return pl.BlockSpec(shape, lambda *_: (0,) * nd, pipeline_mode=pl.Buffered(1))


def _mod_body(c_ref, w_ref, b_ref, o_ref):
    c = c_ref[...]
    a = c * _sigmoid(c)
    o_ref[...] = jnp.dot(a, w_ref[...], preferred_element_type=F32,
                         precision=lax.Precision.HIGHEST) + b_ref[...]


def _mod_call(c_pad, w_ada, b_ada):
    rows, d = c_pad.shape
    n = w_ada.shape[1]
    bn = 512
    return pl.pallas_call(
        _mod_body,
        grid=(n // bn,),
        in_specs=[pl.BlockSpec((rows, d), lambda j: (0, 0)),
                  pl.BlockSpec((d, bn), lambda j: (0, j)),
                  pl.BlockSpec((1, bn), lambda j: (0, j))],
        out_specs=pl.BlockSpec((rows, bn), lambda j: (0, j)),
        out_shape=jax.ShapeDtypeStruct((rows, n), F32),
        name="adaln_mod",
    )(c_pad, w_ada, b_ada.reshape(1, n))


def _lb_body(l_ref, o_ref):
    l = l_ref[...]
    e = jnp.exp(l - jnp.max(l, axis=0, keepdims=True))
    o_ref[...] = e[0:1] / jnp.sum(e, axis=0, keepdims=True)


def _lb_call(lb_logits):
    n = lb_logits.shape[0]
    flat = lb_logits.reshape(n, -1).astype(F32)
    return pl.pallas_call(
        _lb_body,
        out_shape=jax.ShapeDtypeStruct((1, flat.shape[1]), F32),
        name="hgrn_lower_bounds",
    )(flat)


def _rope(p, cos, sin_signed):
    n = p.shape[-1]
    lane = lax.broadcasted_iota(jnp.int32, p.shape, 1)
    first_half = (lane & 63) < 32
    partner = jnp.where(first_half, pltpu.roll(p, n - 32, 1), pltpu.roll(p, 32, 1))
    reps = n // cos.shape[-1]
    return p * jnp.tile(cos, (1, reps)) + partner * jnp.tile(sin_signed, (1, reps))


def _store_dilated(t_bf, perm_ref, out_ref, r):
    if r == 1:
        out_ref[0] = t_bf
        return
    tm, w = t_bf.shape
    grouped = _dot(perm_ref[...], t_bf).astype(BF16)
    rows = tm // r
    for c in range(r):
        out_ref[0, :, c * w:(c + 1) * w] = grouped[c * rows:(c + 1) * rows]


def _inproj_body(x_ref, mod_ref, g1_ref, w_ref, b_ref, lb_ref, cos_ref, sin_ref, p4_ref, p16_ref,
                 qh_ref, lff_ref, kf_ref, lfb_ref, kb_ref, vh_ref, sg_ref, ga_ref, gb_ref,
                 q1_ref, k1_ref, v1_ref, q4_ref, k4_ref, v4_ref, q16_ref, k16_ref, v16_ref):
    x = x_ref[0]
    mod = mod_ref[0]
    shift1, scale1 = mod[0:1], mod[1:2]
    h = _rms(x) * g1_ref[...]
    h = h * (1.0 + scale1) + shift1
    hb = h.astype(BF16)

    def proj(j):
        cols = slice(j * PROJ_BLOCK, (j + 1) * PROJ_BLOCK)
        return _dot(hb, w_ref[:, cols]) + b_ref[:, cols]

    p = proj(0)
    qh_ref[0] = p * _sigmoid(p) * (HG_DIM ** -0.5)

    def gates(p, lb, lf_ref, k_ref):
        f = lb + (1.0 - lb) * _sigmoid(p)
        lf_ref[0] = jnp.log2(f)
        k_ref[0] = 1.0 - f

    gates(proj(1), lb_ref[:, 0:HG_WIDTH], lff_ref, kf_ref)
    gates(proj(2), lb_ref[:, HG_WIDTH:2 * HG_WIDTH], lfb_ref, kb_ref)
    vh_ref[0] = proj(3).astype(BF16)
    p = proj(4)
    sg_ref[0] = (p * _sigmoid(p)).astype(sg_ref.dtype)

    cos, sin_signed = cos_ref[...], sin_ref[...]
    q_a = (_rope(proj(5), cos, sin_signed) * (ATT_HDIM ** -0.5 * LOG2E)).astype(BF16)
    k_a = _rope(proj(6), cos, sin_signed).astype(BF16)
    v_a = proj(7).astype(BF16)
    for t, refs in ((q_a, (q1_ref, q4_ref, q16_ref)), (k_a, (k1_ref, k4_ref, k16_ref)),
                    (v_a, (v1_ref, v4_ref, v16_ref))):
        for (_, r), perm_ref, out_ref in zip(DILATION_PATTERNS, (None, p4_ref, p16_ref), refs):
            _store_dilated(t, perm_ref, out_ref, r)
    ga_ref[0, :, 0:PROJ_BLOCK] = _sigmoid(proj(8)).astype(ga_ref.dtype)
    ga_ref[0, :, PROJ_BLOCK:2 * PROJ_BLOCK] = _sigmoid(proj(9)).astype(ga_ref.dtype)
    gb_ref[0, :, 0:PROJ_BLOCK] = _sigmoid(proj(10)).astype(gb_ref.dtype)
    gb_ref[0, :, PROJ_BLOCK:2 * PROJ_BLOCK] = _sigmoid(proj(11)).astype(gb_ref.dtype)


def _dilation_perm(tm, r):
    i = jnp.arange(tm)
    src = r * (i % (tm // r)) + i // (tm // r)
    return (src[:, None] == jnp.arange(tm)[None, :]).astype(BF16)


def _inproj_call(x, mod, g1, w_in, b_in, lb, cos, sin_signed, tm):
    B, L, D = x.shape
    n_in = w_in.shape[1]
    assert n_in == N_PROJ_BLOCKS * PROJ_BLOCK and L % tm == 0
    row = lambda w, r=1: pl.BlockSpec((1, tm // r, r * w), lambda b, i: (b, i, 0))
    tab = pl.BlockSpec((tm, cos.shape[1]), lambda b, i: (i, 0))
    shp = lambda w, dt, r=1: jax.ShapeDtypeStruct((B, L // r, r * w), dt)
    perms = [_dilation_perm(tm, r) for (_, r) in DILATION_PATTERNS[1:]]
    att_specs = [row(ATT_WIDTH, r) for (_, r) in DILATION_PATTERNS for _ in range(3)]
    att_shapes = [shp(ATT_WIDTH, BF16, r) for (_, r) in DILATION_PATTERNS for _ in range(3)]
    return pl.pallas_call(
        _inproj_body,
        grid=(B, L // tm),
        in_specs=[row(D),
                  pl.BlockSpec((1,) + mod.shape[1:], lambda b, i: (b, 0, 0)),
                  _resident(g1.shape), _resident(w_in.shape), _resident(b_in.shape), _resident(lb.shape),
                  tab, tab] + [_resident(p.shape) for p in perms],
        out_specs=[row(512)] * 7 + [row(1024)] * 2 + att_specs,
        out_shape=[shp(512, F32), shp(512, F32), shp(512, F32), shp(512, F32), shp(512, F32),
                   shp(512, BF16), shp(512, BF16), shp(1024, BF16), shp(1024, BF16)] + att_shapes,
        compiler_params=pltpu.CompilerParams(
            dimension_semantics=("parallel", "parallel"), vmem_limit_bytes=VMEM_LIMIT),
        name="inproj",
    )(x, mod, g1, w_in, b_in, lb, cos, sin_signed, *perms)


def _bcast_row(c, group, r):
    T, w = c.shape
    c3 = c.reshape(T // group, group, w)
    return jnp.broadcast_to(c3[:, r:r + 1, :], c3.shape).reshape(T, w)


def _split3(g):
    g1 = g.astype(BF16)
    r1 = g - g1.astype(F32)
    g2 = r1.astype(BF16)
    return g1, g2, (r1 - g2.astype(F32)).astype(BF16)


def _hgrn_dir(q, k, v_bf, g, s_in, rev):
    T, dk = q.shape
    C, nch, nsb = HG_CHUNK, T // HG_CHUNK, HG_CHUNK // HG_SUB
    t = lax.broadcasted_iota(jnp.int32, (C, C), 0)
    s = lax.broadcasted_iota(jnp.int32, (C, C), 1)
    order = (s >= t) if rev else (s <= t)
    inner_mask = order & ((t >> HG_SUB_SHIFT) == (s >> HG_SUB_SHIFT))
    tri = order.astype(BF16)

    g_terms = jnp.concatenate(_split3(g), axis=1)
    c_parts = []
    for ci in range(nch):
        r = _dot(tri, g_terms[ci * C:(ci + 1) * C])
        c_parts.append(r[:, 0:dk] + r[:, dk:2 * dk] + r[:, 2 * dk:3 * dk])
    c = jnp.concatenate(c_parts, axis=0)

    far = 0 if rev else HG_SUB - 1
    edge = _bcast_row(c, HG_SUB, far)
    mid = _bcast_row(c, HG_SUB, HG_SUB // 2)
    c_end = _bcast_row(c, C, 0 if rev else C - 1)
    rowc = lax.broadcasted_iota(jnp.int32, (T, dk), 0) & (C - 1)

    k_edge = k * jnp.exp2(edge - c)
    q_cat, k_cat = [], []
    for J in (range(1, nsb) if rev else range(nsb - 1)):
        ref_j = _bcast_row(c, C, J * HG_SUB + far)
        queries = (rowc < J * HG_SUB) if rev else (rowc >= (J + 1) * HG_SUB)
        q_cat.append((q * jnp.exp2(jnp.where(queries, c - ref_j, EXP2_ZERO))).astype(BF16))
        in_j = (rowc >= J * HG_SUB) & (rowc < (J + 1) * HG_SUB)
        k_cat.append(jnp.where(in_j, k_edge, 0.0).astype(BF16))
    q_cat = jnp.concatenate(q_cat, axis=1)
    k_cat = jnp.concatenate(k_cat, axis=1)

    d_mid = jnp.clip(c - mid, -HG_EXP2_CLAMP, HG_EXP2_CLAMP)
    q_mid = (q * jnp.exp2(d_mid)).astype(BF16)
    k_mid = (k * jnp.exp2(-d_mid)).astype(BF16)
    q_dec = (q * jnp.exp2(c)).astype(BF16)
    k_end = (k * jnp.exp2(c_end - c)).astype(BF16)
    decay = jnp.exp2(c_end)

    o_intra, upd = [], []
    for ci in range(nch):
        rows = slice(ci * C, (ci + 1) * C)
        sc = _dot_nt(q_cat[rows], k_cat[rows])
        sc = sc + jnp.where(inner_mask, _dot_nt(q_mid[rows], k_mid[rows]), 0.0)
        o_intra.append(_dot(sc.astype(BF16), v_bf[rows]))
        upd.append(_dot_tn(v_bf[rows], k_end[rows]))

    outs = [None] * nch
    s_t = s_in
    for ci in (reversed(range(nch)) if rev else range(nch)):
        rows = slice(ci * C, (ci + 1) * C)
        outs[ci] = o_intra[ci] + _dot_nt(q_dec[rows], s_t.astype(BF16))
        s_t = s_t * decay[ci * C:ci * C + 1] + upd[ci]
    return jnp.concatenate(outs, axis=0), s_t


def _hgrn_body(qf_ref, lff_ref, kf_ref, vf_ref, qb_ref, lfb_ref, kb_ref, vb_ref,
               of_ref, ob_ref, sf_ref, sb_ref):
    @pl.when(pl.program_id(2) == 0)
    def _():
        sf_ref[...] = jnp.zeros_like(sf_ref)
        sb_ref[...] = jnp.zeros_like(sb_ref)

    o, s_f = _hgrn_dir(qf_ref[0], kf_ref[0], vf_ref[0], lff_ref[0], sf_ref[...], False)
    of_ref[0] = o.astype(of_ref.dtype)
    sf_ref[...] = s_f
    o, s_b = _hgrn_dir(qb_ref[0], kb_ref[0], vb_ref[0], lfb_ref[0], sb_ref[...], True)
    ob_ref[0] = o.astype(ob_ref.dtype)
    sb_ref[...] = s_b


def _hgrn_call(q, lf_f, k_f, lf_b, k_b, v, T):
    B, L, W = q.shape
    nb = L // T
    assert L % T == 0 and T % HG_CHUNK == 0 and W == HG_WIDTH
    fwd = pl.BlockSpec((1, T, HG_DIM), lambda b, h, j: (b, j, h))
    bwd = pl.BlockSpec((1, T, HG_DIM), lambda b, h, j: (b, nb - 1 - j, h))
    return pl.pallas_call(
        _hgrn_body,
        grid=(B, HG_HEADS, nb),
        in_specs=[fwd, fwd, fwd, fwd, bwd, bwd, bwd, bwd],
        out_specs=[fwd, bwd],
        out_shape=[jax.ShapeDtypeStruct((B, L, W), BF16)] * 2,
        scratch_shapes=[pltpu.VMEM((HG_DIM, HG_DIM), F32)] * 2,
        compiler_params=pltpu.CompilerParams(
            dimension_semantics=("parallel", "parallel", "arbitrary"), vmem_limit_bytes=VMEM_LIMIT),
        name="hgrn_scan",
    )(q, lf_f, k_f, v, q, lf_b, k_b, v)


ATT_QBLK = 2 * ATT_RADIUS
ATT_STEP_BLOCKS = 4


def _attn_body(q_ref, kp_ref, kc_ref, kn_ref, vp_ref, vc_ref, vn_ref, o_ref, lse_ref):
    n = pl.program_id(2)
    last = pl.num_programs(2) - 1
    kcat = jnp.concatenate([kp_ref[0], kc_ref[0], kn_ref[0]], axis=0)
    vcat = jnp.concatenate([vp_ref[0], vc_ref[0], vn_ref[0]], axis=0)
    nq, nk = ATT_QBLK, ATT_QBLK + 2 * ATT_RADIUS
    nblk = q_ref.shape[1] // nq
    i = lax.broadcasted_iota(jnp.int32, (nq, nk), 0)
    j = lax.broadcasted_iota(jnp.int32, (nq, nk), 1)
    band = (j >= i) & (j <= i + 2 * ATT_RADIUS)
    has_prev = (j >= ATT_RADIUS) | (n > 0)
    has_next = (j < nq + ATT_RADIUS) | (n < last)

    pair_w = 2 * ATT_HDIM
    npairs = ATT_HEADS // 2
    lane = lax.broadcasted_iota(jnp.int32, kcat.shape, 1) & (pair_w - 1)
    mask_a = jnp.where(lane < ATT_HDIM, 1.0, 0.0).astype(BF16)
    mask_b = jnp.where(lane < ATT_HDIM, 0.0, 1.0).astype(BF16)
    k_a, k_b, v_a, v_b = kcat * mask_a, kcat * mask_b, vcat * mask_a, vcat * mask_b
    lo_k = lax.broadcasted_iota(jnp.int32, (nk, pair_w), 1) < ATT_HDIM
    ones_a = jnp.where(lo_k, 1.0, 0.0).astype(BF16)
    ones_b = jnp.where(lo_k, 0.0, 1.0).astype(BF16)
    lo_q = lax.broadcasted_iota(jnp.int32, (nq, pair_w), 1) < ATT_HDIM

    for blk in range(nblk):
        valid = band
        if blk == 0:
            valid = valid & has_prev
        if blk == nblk - 1:
            valid = valid & has_next
        qrows = slice(blk * nq, (blk + 1) * nq)
        krows = slice(blk * nq, blk * nq + nk)
        scores = []
        for p in range(npairs):
            cols = slice(p * pair_w, (p + 1) * pair_w)
            q = q_ref[0, qrows, cols]
            scores.append(_dot_nt(q, k_a[krows, cols]))
            scores.append(_dot_nt(q, k_b[krows, cols]))
        scores = [jnp.where(valid, s, -jnp.inf) for s in scores]
        tops = [jnp.max(s, axis=-1, keepdims=True) for s in scores]
        probs = [jnp.exp2(s - m).astype(BF16) for s, m in zip(scores, tops)]
        for p in range(npairs):
            cols = slice(p * pair_w, (p + 1) * pair_w)
            res = (_dot(probs[2 * p], jnp.concatenate([v_a[krows, cols], ones_a], axis=1))
                   + _dot(probs[2 * p + 1], jnp.concatenate([v_b[krows, cols], ones_b], axis=1)))
            num, den = res[:, 0:pair_w], res[:, pair_w:2 * pair_w]
            o_ref[0, qrows, cols] = (num / den).astype(o_ref.dtype)
            lse_ref[0, qrows, cols] = jnp.where(lo_q, tops[2 * p], tops[2 * p + 1]) + jnp.log2(den)


def _attn_call(q, k, v, r):
    B, Lr, rW = q.shape
    W = rW // r
    step = ATT_QBLK * min(ATT_STEP_BLOCKS, Lr // ATT_QBLK)
    assert W == ATT_WIDTH and Lr % step == 0
    per_step = step // ATT_RADIUS
    nhalo = Lr // ATT_RADIUS
    center = pl.BlockSpec((1, step, W), lambda b, c, n: (b, n, c))
    prev = pl.BlockSpec((1, ATT_RADIUS, W), lambda b, c, n: (b, jnp.maximum(per_step * n - 1, 0), c))
    nxt = pl.BlockSpec((1, ATT_RADIUS, W), lambda b, c, n: (b, jnp.minimum(per_step * (n + 1), nhalo - 1), c))
    return pl.pallas_call(
        _attn_body,
        grid=(B, r, Lr // step),
        in_specs=[center, prev, center, nxt, prev, center, nxt],
        out_specs=[center, center],
        out_shape=[jax.ShapeDtypeStruct((B, Lr, rW), BF16), jax.ShapeDtypeStruct((B, Lr, rW), F32)],
        compiler_params=pltpu.CompilerParams(
            dimension_semantics=("parallel", "parallel", "parallel"), vmem_limit_bytes=VMEM_LIMIT),
        name=f"dilated_attn_r{r}",
    )(q, k, k, k, v, v, v)


FFN_CHUNK = 256


def _load_natural(a_ref, l_ref, perm_t_ref, r):
    if r == 1:
        return a_ref[0].astype(F32), l_ref[0]
    w = a_ref.shape[2] // r
    a = jnp.concatenate([a_ref[0, :, c * w:(c + 1) * w] for c in range(r)], axis=0)
    l = jnp.concatenate([l_ref[0, :, c * w:(c + 1) * w] for c in range(r)], axis=0)
    l_hi = l.astype(BF16)
    l_lo = (l - l_hi.astype(F32)).astype(BF16)
    nat = _dot(perm_t_ref[...], jnp.concatenate([a, l_hi, l_lo], axis=1))
    return nat[:, 0:w], nat[:, w:2 * w] + nat[:, 2 * w:3 * w]


def _merge_body(x_ref, mod_ref, of_ref, ob_ref, sg_ref, a1_ref, l1_ref, a2_ref, l2_ref, a3_ref, l3_ref,
                ga_ref, gb_ref, gn_ref, n2_ref, fin_ref, wa_ref, wb_ref, wo_ref, wfi_ref, wfo_ref,
                p4t_ref, p16t_ref, y_ref, act_ref):
    mod = mod_ref[0]
    gate1, shift2, scale2, gate2 = mod[2:3], mod[3:4], mod[4:5], mod[5:6]

    o = of_ref[0].astype(F32) + ob_ref[0].astype(F32)
    o_a = jnp.concatenate([_rms(o[:, h * HG_DIM:(h + 1) * HG_DIM]) for h in range(HG_HEADS)], axis=1)
    o_a = o_a * gn_ref[...] * sg_ref[0]

    (a1, l1), (a2, l2), (a3, l3) = [
        _load_natural(a_ref, l_ref, perm_ref, r)
        for (_, r), a_ref, l_ref, perm_ref in zip(DILATION_PATTERNS, (a1_ref, a2_ref, a3_ref),
                                                  (l1_ref, l2_ref, l3_ref), (None, p4t_ref, p16t_ref))]
    top = jnp.maximum(jnp.maximum(l1, l2), l3)
    w1, w2, w3 = jnp.exp2(l1 - top), jnp.exp2(l2 - top), jnp.exp2(l3 - top)
    o_b = (a1 * w1 + a2 * w2 + a3 * w3) / (w1 + w2 + w3)

    merged = (ga_ref[0] * _dot(o_a.astype(BF16), wa_ref[...])
              + gb_ref[0] * _dot(o_b.astype(BF16), wb_ref[...]))
    x1 = x_ref[0] + gate1 * _dot(merged.astype(BF16), wo_ref[...])

    h2 = (_rms(x1) * n2_ref[...] * (1.0 + scale2) + shift2).astype(BF16)
    hidden = wfo_ref.shape[0]
    for c0 in range(0, hidden, FFN_CHUNK):
        gt = _dot(h2, wfi_ref[:, c0:c0 + FFN_CHUNK])
        up = _dot(h2, wfi_ref[:, hidden + c0:hidden + c0 + FFN_CHUNK])
        act_ref[:, c0:c0 + FFN_CHUNK] = (gt * _sigmoid(gt) * up).astype(BF16)
    x2 = x1 + gate2 * _dot(act_ref[...], wfo_ref[...])
    y_ref[0] = _rms(x2) * fin_ref[...]


def _merge_call(x, mod, o_f, o_b, sg, att, ga, gb, gn, n2, fin, wa, wb, wo, wfi, wfo, tm):
    B, L, D = x.shape
    hidden = wfo.shape[0]
    assert L % tm == 0 and hidden % FFN_CHUNK == 0 and wfi.shape[1] == 2 * hidden
    row = lambda w, r=1: pl.BlockSpec((1, tm // r, r * w), lambda b, i: (b, i, 0))
    att_flat = [t for pair in att for t in pair]
    att_specs = [row(ATT_WIDTH, r) for (_, r) in DILATION_PATTERNS for _ in range(2)]
    perms_t = [_dilation_perm(tm, r).T for (_, r) in DILATION_PATTERNS[1:]]
    return pl.pallas_call(
        _merge_body,
        grid=(B, L // tm),
        in_specs=[row(D), pl.BlockSpec((1,) + mod.shape[1:], lambda b, i: (b, 0, 0))]
                 + [row(512)] * 3 + att_specs + [row(D), row(D)]
                 + [_resident(t.shape) for t in (gn, n2, fin, wa, wb, wo, wfi, wfo, *perms_t)],
        out_specs=row(D),
        out_shape=jax.ShapeDtypeStruct((B, L, D), F32),
        scratch_shapes=[pltpu.VMEM((tm, hidden), BF16)],
        compiler_params=pltpu.CompilerParams(
            dimension_semantics=("parallel", "parallel"), vmem_limit_bytes=VMEM_LIMIT),
        name="merge_ffn",
    )(x, mod, o_f, o_b, sg, *att_flat, ga, gb, gn, n2, fin, wa, wb, wo, wfi, wfo, *perms_t)


def _rope_tables(L):
    half = ATT_HDIM // 2
    pos = jnp.arange(L, dtype=F32)
    inv = ROPE_THETA ** (-jnp.arange(half, dtype=F32) / half)
    ang = pos[:, None] * inv[None, :]
    cos, sin = jnp.cos(ang), jnp.sin(ang)
    return jnp.tile(jnp.concatenate([cos, cos], axis=1), (1, 2)), jnp.tile(jnp.concatenate([-sin, sin], axis=1), (1, 2))


def _trunk(x, mod, lb, p, tm_in, tm_out, t_scan):
    B, L, D = x.shape
    cos, sin_signed = _rope_tables(L)
    (q_h, lf_f, k_f, lf_b, k_b, v_h, sg, ga, gb, *qkv) = _inproj_call(
        x, mod, p["g1"], p["w_in"], p["b_in"], lb, cos, sin_signed, tm_in)
    o_f, o_b = _hgrn_call(q_h, lf_f, k_f, lf_b, k_b, v_h, t_scan)
    att = [_attn_call(*qkv[3 * i:3 * i + 3], r) for i, (_, r) in enumerate(DILATION_PATTERNS)]
    return _merge_call(x, mod, o_f, o_b, sg, att, ga, gb, p["gn"], p["n2"], p["fin"],
                       p["wa"], p["wb"], p["wo"], p["wfi"], p["wfo"], tm_out)


def kernel(x_prompt, x_sample, c_prompt, c_sample, w_ada, b_ada, norm1_g, w_in, b_in, lb_logits, hg_norm_g, w_branch_a, w_branch_b, w_out, norm2_g, w_ffn_in, w_ffn_out, final_norm_g):
    assert w_ada.shape[0] == 1 and lb_logits.shape[0] == 2, "single-layer trunk"
    D = x_prompt.shape[-1]
    bp, bs = c_prompt.shape[0], c_sample.shape[0]
    c_all = jnp.concatenate([c_prompt, c_sample], axis=0)
    c_pad = jnp.pad(c_all, ((0, -(bp + bs) % 8), (0, 0)))
    mod = _mod_call(c_pad, w_ada[0], b_ada[0]).reshape(c_pad.shape[0], 6, D)
    lb = _lb_call(lb_logits)
    row = lambda v: v.reshape(1, -1).astype(F32)
    p = dict(g1=row(norm1_g[0]), w_in=w_in[0].astype(BF16), b_in=row(b_in[0]), gn=row(hg_norm_g[0]),
             n2=row(norm2_g[0]), fin=row(final_norm_g), wa=w_branch_a[0].astype(BF16),
             wb=w_branch_b[0].astype(BF16), wo=w_out[0].astype(BF16), wfi=w_ffn_in[0].astype(BF16),
             wfo=w_ffn_out[0].astype(BF16))
    y_prompt = _trunk(x_prompt, mod[:bp], lb, p, 256, 256, 512)
    y_sample = _trunk(x_sample, mod[bp:bp + bs], lb, p, 256, 256, 512)
    return (y_prompt, y_sample)
```

```python
import jax
import jax.numpy as jnp
from jax import lax
from jax.experimental import pallas as pl
from jax.experimental.pallas import tpu as pltpu

F32 = jnp.float32
BF16 = jnp.bfloat16

NORM_EPS = 1e-6
HG_HEADS = 4
HG_DIM = 128
HG_WIDTH = HG_HEADS * HG_DIM
ATT_HEADS = 8
ATT_HDIM = 64
ATT_WIDTH = ATT_HEADS * ATT_HDIM
DILATION_PATTERNS = ((128, 1), (512, 4), (2048, 16))
ATT_RADIUS = 64
ROPE_THETA = 10000.0
LOG2E = 1.4426950408889634
PROJ_BLOCK = 512
N_PROJ_BLOCKS = 12
PERM_ROWS = 256

HG_CHUNK = 64
HG_SUB = 16
HG_SUB_SHIFT = 4
HG_EXP2_CLAMP = 110.0
EXP2_ZERO = -1e30
VMEM_LIMIT = 56 * 1024 * 1024


def _dot(a, b):
    return jnp.dot(a, b, preferred_element_type=F32)


def _dot_nt(a, b):
    return lax.dot_general(a, b, (((1,), (1,)), ((), ())), preferred_element_type=F32)


def _dot_tn(a, b):
    return lax.dot_general(a, b, (((0,), (0,)), ((), ())), preferred_element_type=F32)


def _sigmoid(x):
    return 1.0 / (1.0 + jnp.exp(-x))


def _rms(x):
    return x * lax.rsqrt(jnp.mean(x * x, axis=-1, keepdims=True) + NORM_EPS)


def _resident(shape):
    nd = len(shape)
    return pl.BlockSpec(shape, lambda *_: (0,) * nd, pipeline_mode=pl.Buffered(1))


def _mod_body(c_ref, w_ref, b_ref, o_ref):
    c = c_ref[...]
    a = c * _sigmoid(c)
    o_ref[...] = jnp.dot(a, w_ref[...], preferred_element_type=F32,
                         precision=lax.Precision.HIGHEST) + b_ref[...]


def _mod_call(c_pad, w_ada, b_ada):
    rows, d = c_pad.shape
    n = w_ada.shape[1]
    bn = 512
    return pl.pallas_call(
        _mod_body,
        grid=(n // bn,),
        in_specs=[pl.BlockSpec((rows, d), lambda j: (0, 0)),
                  pl.BlockSpec((d, bn), lambda j: (0, j)),
                  pl.BlockSpec((1, bn), lambda j: (0, j))],
        out_specs=pl.BlockSpec((rows, bn), lambda j: (0, j)),
        out_shape=jax.ShapeDtypeStruct((rows, n), F32),
        name="adaln_mod",
    )(c_pad, w_ada, b_ada.reshape(1, n))


def _lb_body(l_ref, o_ref):
    l = l_ref[...]
    e = jnp.exp(l - jnp.max(l, axis=0, keepdims=True))
    o_ref[...] = e[0:1] / jnp.sum(e, axis=0, keepdims=True)


def _lb_call(lb_logits):
    n = lb_logits.shape[0]
    flat = lb_logits.reshape(n, -1).astype(F32)
    return pl.pallas_call(
        _lb_body,
        out_shape=jax.ShapeDtypeStruct((1, flat.shape[1]), F32),
        name="hgrn_lower_bounds",
    )(flat)


def _rope(p, cos, sin_signed):
    n = p.shape[-1]
    lane = lax.broadcasted_iota(jnp.int32, p.shape, 1)
    first_half = (lane & 63) < 32
    partner = jnp.where(first_half, pltpu.roll(p, n - 32, 1), pltpu.roll(p, 32, 1))
    reps = n // cos.shape[-1]
    return p * jnp.tile(cos, (1, reps)) + partner * jnp.tile(sin_signed, (1, reps))


def _store_dilated(t_bf, perm_ref, out_ref, r):
    if r == 1:
        out_ref[0] = t_bf
        return
    tm, w = t_bf.shape
    rows = PERM_ROWS // r
    for sub in range(tm // PERM_ROWS):
        grouped = _dot(perm_ref[...], t_bf[sub * PERM_ROWS:(sub + 1) * PERM_ROWS]).astype(BF16)
        for c in range(r):
            out_ref[0, sub * rows:(sub + 1) * rows, c * w:(c + 1) * w] = grouped[c * rows:(c + 1) * rows]


def _inproj_body(x_ref, mod_ref, g1_ref, w_ref, b_ref, lb_ref, cos_ref, sin_ref, p4_ref, p16_ref,
                 qh_ref, lff_ref, kf_ref, lfb_ref, kb_ref, vh_ref, sg_ref, ga_ref, gb_ref,
                 q1_ref, k1_ref, v1_ref, q4_ref, k4_ref, v4_ref, q16_ref, k16_ref, v16_ref):
    x = x_ref[0]
    mod = mod_ref[0]
    shift1, scale1 = mod[0:1], mod[1:2]
    h = _rms(x) * g1_ref[...]
    h = h * (1.0 + scale1) + shift1
    hb = h.astype(BF16)

    def proj(j):
        cols = slice(j * PROJ_BLOCK, (j + 1) * PROJ_BLOCK)
        return _dot(hb, w_ref[:, cols]) + b_ref[:, cols]

    p = proj(0)
    qh_ref[0] = p * _sigmoid(p) * (HG_DIM ** -0.5)

    def gates(p, lb, lf_ref, k_ref):
        f = lb + (1.0 - lb) * _sigmoid(p)
        lf_ref[0] = jnp.log2(f)
        k_ref[0] = 1.0 - f

    gates(proj(1), lb_ref[:, 0:HG_WIDTH], lff_ref, kf_ref)
    gates(proj(2), lb_ref[:, HG_WIDTH:2 * HG_WIDTH], lfb_ref, kb_ref)
    vh_ref[0] = proj(3).astype(BF16)
    p = proj(4)
    sg_ref[0] = (p * _sigmoid(p)).astype(sg_ref.dtype)

    cos, sin_signed = cos_ref[...], sin_ref[...]
    q_a = (_rope(proj(5), cos, sin_signed) * (ATT_HDIM ** -0.5 * LOG2E)).astype(BF16)
    k_a = _rope(proj(6), cos, sin_signed).astype(BF16)
    v_a = proj(7).astype(BF16)
    for t, refs in ((q_a, (q1_ref, q4_ref, q16_ref)), (k_a, (k1_ref, k4_ref, k16_ref)),
                    (v_a, (v1_ref, v4_ref, v16_ref))):
        for (_, r), perm_ref, out_ref in zip(DILATION_PATTERNS, (None, p4_ref, p16_ref), refs):
            _store_dilated(t, perm_ref, out_ref, r)
    ga_ref[0, :, 0:PROJ_BLOCK] = _sigmoid(proj(8)).astype(ga_ref.dtype)
    ga_ref[0, :, PROJ_BLOCK:2 * PROJ_BLOCK] = _sigmoid(proj(9)).astype(ga_ref.dtype)
    gb_ref[0, :, 0:PROJ_BLOCK] = _sigmoid(proj(10)).astype(gb_ref.dtype)
    gb_ref[0, :, PROJ_BLOCK:2 * PROJ_BLOCK] = _sigmoid(proj(11)).astype(gb_ref.dtype)


def _dilation_perm(tm, r):
    i = jnp.arange(tm)
    src = r * (i % (tm // r)) + i // (tm // r)
    return (src[:, None] == jnp.arange(tm)[None, :]).astype(BF16)


def _inproj_call(x, mod, g1, w_in, b_in, lb, cos, sin_signed, tm):
    B, L, D = x.shape
    n_in = w_in.shape[1]
    assert n_in == N_PROJ_BLOCKS * PROJ_BLOCK and L % tm == 0 and tm % PERM_ROWS == 0
    row = lambda w, r=1: pl.BlockSpec((1, tm // r, r * w), lambda b, i: (b, i, 0))
    tab = pl.BlockSpec((tm, cos.shape[1]), lambda b, i: (i, 0))
    shp = lambda w, dt, r=1: jax.ShapeDtypeStruct((B, L // r, r * w), dt)
    perms = [_dilation_perm(PERM_ROWS, r) for (_, r) in DILATION_PATTERNS[1:]]
    att_specs = [row(ATT_WIDTH, r) for (_, r) in DILATION_PATTERNS for _ in range(3)]
    att_shapes = [shp(ATT_WIDTH, BF16, r) for (_, r) in DILATION_PATTERNS for _ in range(3)]
    return pl.pallas_call(
        _inproj_body,
        grid=(B, L // tm),
        in_specs=[row(D),
                  pl.BlockSpec((1,) + mod.shape[1:], lambda b, i: (b, 0, 0)),
                  _resident(g1.shape), _resident(w_in.shape), _resident(b_in.shape), _resident(lb.shape),
                  tab, tab] + [_resident(p.shape) for p in perms],
        out_specs=[row(512)] * 7 + [row(1024)] * 2 + att_specs,
        out_shape=[shp(512, F32), shp(512, F32), shp(512, F32), shp(512, F32), shp(512, F32),
                   shp(512, BF16), shp(512, BF16), shp(1024, BF16), shp(1024, BF16)] + att_shapes,
        compiler_params=pltpu.CompilerParams(
            dimension_semantics=("parallel", "parallel"), vmem_limit_bytes=VMEM_LIMIT),
        name="inproj",
    )(x, mod, g1, w_in, b_in, lb, cos, sin_signed, *perms)


def _bcast_row(c, group, r):
    T, w = c.shape
    c3 = c.reshape(T // group, group, w)
    return jnp.broadcast_to(c3[:, r:r + 1, :], c3.shape).reshape(T, w)


def _split3(g):
    g1 = g.astype(BF16)
    r1 = g - g1.astype(F32)
    g2 = r1.astype(BF16)
    return g1, g2, (r1 - g2.astype(F32)).astype(BF16)


def _hgrn_dir(q, k, v_bf, g, s_in, rev):
    T, dk = q.shape
    C, nch, nsb = HG_CHUNK, T // HG_CHUNK, HG_CHUNK // HG_SUB
    t = lax.broadcasted_iota(jnp.int32, (C, C), 0)
    s = lax.broadcasted_iota(jnp.int32, (C, C), 1)
    order = (s >= t) if rev else (s <= t)
    inner_mask = order & ((t >> HG_SUB_SHIFT) == (s >> HG_SUB_SHIFT))
    tri = order.astype(BF16)

    g_terms = jnp.concatenate(_split3(g), axis=1)
    c_parts = []
    for ci in range(nch):
        r = _dot(tri, g_terms[ci * C:(ci + 1) * C])
        c_parts.append(r[:, 0:dk] + r[:, dk:2 * dk] + r[:, 2 * dk:3 * dk])
    c = jnp.concatenate(c_parts, axis=0)

    far = 0 if rev else HG_SUB - 1
    edge = _bcast_row(c, HG_SUB, far)
    mid = _bcast_row(c, HG_SUB, HG_SUB // 2)
    c_end = _bcast_row(c, C, 0 if rev else C - 1)
    rowc = lax.broadcasted_iota(jnp.int32, (T, dk), 0) & (C - 1)

    k_edge = k * jnp.exp2(edge - c)
    q_cat, k_cat = [], []
    for J in (range(1, nsb) if rev else range(nsb - 1)):
        ref_j = _bcast_row(c, C, J * HG_SUB + far)
        queries = (rowc < J * HG_SUB) if rev else (rowc >= (J + 1) * HG_SUB)
        q_cat.append((q * jnp.exp2(jnp.where(queries, c - ref_j, EXP2_ZERO))).astype(BF16))
        in_j = (rowc >= J * HG_SUB) & (rowc < (J + 1) * HG_SUB)
        k_cat.append(jnp.where(in_j, k_edge, 0.0).astype(BF16))
    q_cat = jnp.concatenate(q_cat, axis=1)
    k_cat = jnp.concatenate(k_cat, axis=1)

    d_mid = jnp.clip(c - mid, -HG_EXP2_CLAMP, HG_EXP2_CLAMP)
    q_mid = (q * jnp.exp2(d_mid)).astype(BF16)
    k_mid = (k * jnp.exp2(-d_mid)).astype(BF16)
    q_dec = (q * jnp.exp2(c)).astype(BF16)
    k_end = (k * jnp.exp2(c_end - c)).astype(BF16)
    decay = jnp.exp2(c_end)

    o_intra, upd = [], []
    for ci in range(nch):
        rows = slice(ci * C, (ci + 1) * C)
        sc = _dot_nt(q_cat[rows], k_cat[rows])
        sc = sc + jnp.where(inner_mask, _dot_nt(q_mid[rows], k_mid[rows]), 0.0)
        o_intra.append(_dot(sc.astype(BF16), v_bf[rows]))
        upd.append(_dot_tn(v_bf[rows], k_end[rows]))

    outs = [None] * nch
    s_t = s_in
    for ci in (reversed(range(nch)) if rev else range(nch)):
        rows = slice(ci * C, (ci + 1) * C)
        outs[ci] = o_intra[ci] + _dot_nt(q_dec[rows], s_t.astype(BF16))
        s_t = s_t * decay[ci * C:ci * C + 1] + upd[ci]
    return jnp.concatenate(outs, axis=0), s_t


def _hgrn_body(qf_ref, lff_ref, kf_ref, vf_ref, qb_ref, lfb_ref, kb_ref, vb_ref,
               of_ref, ob_ref, sf_ref, sb_ref):
    @pl.when(pl.program_id(2) == 0)
    def _():
        sf_ref[...] = jnp.zeros_like(sf_ref)
        sb_ref[...] = jnp.zeros_like(sb_ref)

    o, s_f = _hgrn_dir(qf_ref[0], kf_ref[0], vf_ref[0], lff_ref[0], sf_ref[...], False)
    of_ref[0] = o.astype(of_ref.dtype)
    sf_ref[...] = s_f
    o, s_b = _hgrn_dir(qb_ref[0], kb_ref[0], vb_ref[0], lfb_ref[0], sb_ref[...], True)
    ob_ref[0] = o.astype(ob_ref.dtype)
    sb_ref[...] = s_b


def _hgrn_call(q, lf_f, k_f, lf_b, k_b, v, T):
    B, L, W = q.shape
    nb = L // T
    assert L % T == 0 and T % HG_CHUNK == 0 and W == HG_WIDTH
    fwd = pl.BlockSpec((1, T, HG_DIM), lambda b, h, j: (b, j, h))
    bwd = pl.BlockSpec((1, T, HG_DIM), lambda b, h, j: (b, nb - 1 - j, h))
    return pl.pallas_call(
        _hgrn_body,
        grid=(B, HG_HEADS, nb),
        in_specs=[fwd, fwd, fwd, fwd, bwd, bwd, bwd, bwd],
        out_specs=[fwd, bwd],
        out_shape=[jax.ShapeDtypeStruct((B, L, W), BF16)] * 2,
        scratch_shapes=[pltpu.VMEM((HG_DIM, HG_DIM), F32)] * 2,
        compiler_params=pltpu.CompilerParams(
            dimension_semantics=("parallel", "parallel", "arbitrary"), vmem_limit_bytes=VMEM_LIMIT),
        name="hgrn_scan",
    )(q, lf_f, k_f, v, q, lf_b, k_b, v)


ATT_QBLK = 2 * ATT_RADIUS
ATT_STEP_BLOCKS = 4


def _attn_body(q_ref, kp_ref, kc_ref, kn_ref, vp_ref, vc_ref, vn_ref, o_ref, lse_ref):
    n = pl.program_id(2)
    last = pl.num_programs(2) - 1
    kcat = jnp.concatenate([kp_ref[0], kc_ref[0], kn_ref[0]], axis=0)
    vcat = jnp.concatenate([vp_ref[0], vc_ref[0], vn_ref[0]], axis=0)
    nq, nk = ATT_QBLK, ATT_QBLK + 2 * ATT_RADIUS
    nblk = q_ref.shape[1] // nq
    i = lax.broadcasted_iota(jnp.int32, (nq, nk), 0)
    j = lax.broadcasted_iota(jnp.int32, (nq, nk), 1)
    band = (j >= i) & (j <= i + 2 * ATT_RADIUS)
    has_prev = (j >= ATT_RADIUS) | (n > 0)
    has_next = (j < nq + ATT_RADIUS) | (n < last)

    pair_w = 2 * ATT_HDIM
    npairs = ATT_HEADS // 2
    lane = lax.broadcasted_iota(jnp.int32, kcat.shape, 1) & (pair_w - 1)
    mask_a = jnp.where(lane < ATT_HDIM, 1.0, 0.0).astype(BF16)
    mask_b = jnp.where(lane < ATT_HDIM, 0.0, 1.0).astype(BF16)
    k_a, k_b, v_a, v_b = kcat * mask_a, kcat * mask_b, vcat * mask_a, vcat * mask_b
    lo_k = lax.broadcasted_iota(jnp.int32, (nk, pair_w), 1) < ATT_HDIM
    ones_a = jnp.where(lo_k, 1.0, 0.0).astype(BF16)
    ones_b = jnp.where(lo_k, 0.0, 1.0).astype(BF16)
    lo_q = lax.broadcasted_iota(jnp.int32, (nq, pair_w), 1) < ATT_HDIM

    for blk in range(nblk):
        valid = band
        if blk == 0:
            valid = valid & has_prev
        if blk == nblk - 1:
            valid = valid & has_next
        qrows = slice(blk * nq, (blk + 1) * nq)
        krows = slice(blk * nq, blk * nq + nk)
        scores = []
        for p in range(npairs):
            cols = slice(p * pair_w, (p + 1) * pair_w)
            q = q_ref[0, qrows, cols]
            scores.append(_dot_nt(q, k_a[krows, cols]))
            scores.append(_dot_nt(q, k_b[krows, cols]))
        scores = [jnp.where(valid, s, -jnp.inf) for s in scores]
        tops = [jnp.max(s, axis=-1, keepdims=True) for s in scores]
        probs = [jnp.exp2(s - m).astype(BF16) for s, m in zip(scores, tops)]
        for p in range(npairs):
            cols = slice(p * pair_w, (p + 1) * pair_w)
            res = (_dot(probs[2 * p], jnp.concatenate([v_a[krows, cols], ones_a], axis=1))
                   + _dot(probs[2 * p + 1], jnp.concatenate([v_b[krows, cols], ones_b], axis=1)))
            num, den = res[:, 0:pair_w], res[:, pair_w:2 * pair_w]
            o_ref[0, qrows, cols] = (num / den).astype(o_ref.dtype)
            lse_ref[0, qrows, cols] = jnp.where(lo_q, tops[2 * p], tops[2 * p + 1]) + jnp.log2(den)


def _attn_call(q, k, v, r):
    B, Lr, rW = q.shape
    W = rW // r
    step = ATT_QBLK * min(ATT_STEP_BLOCKS, Lr // ATT_QBLK)
    assert W == ATT_WIDTH and Lr % step == 0
    per_step = step // ATT_RADIUS
    nhalo = Lr // ATT_RADIUS
    center = pl.BlockSpec((1, step, W), lambda b, c, n: (b, n, c))
    prev = pl.BlockSpec((1, ATT_RADIUS, W), lambda b, c, n: (b, jnp.maximum(per_step * n - 1, 0), c))
    nxt = pl.BlockSpec((1, ATT_RADIUS, W), lambda b, c, n: (b, jnp.minimum(per_step * (n + 1), nhalo - 1), c))
    return pl.pallas_call(
        _attn_body,
        grid=(B, r, Lr // step),
        in_specs=[center, prev, center, nxt, prev, center, nxt],
        out_specs=[center, center],
        out_shape=[jax.ShapeDtypeStruct((B, Lr, rW), BF16), jax.ShapeDtypeStruct((B, Lr, rW), F32)],
        compiler_params=pltpu.CompilerParams(
            dimension_semantics=("parallel", "parallel", "parallel"), vmem_limit_bytes=VMEM_LIMIT),
        name=f"dilated_attn_r{r}",
    )(q, k, k, k, v, v, v)


FFN_CHUNK = 256


def _load_natural(a_ref, l_ref, perm_t_ref, r):
    if r == 1:
        return a_ref[0].astype(F32), l_ref[0]
    w = a_ref.shape[2] // r
    rows = PERM_ROWS // r
    a_nat, l_nat = [], []
    for sub in range(a_ref.shape[1] // rows):
        blk = slice(sub * rows, (sub + 1) * rows)
        a = jnp.concatenate([a_ref[0, blk, c * w:(c + 1) * w] for c in range(r)], axis=0)
        l = jnp.concatenate([l_ref[0, blk, c * w:(c + 1) * w] for c in range(r)], axis=0)
        l_hi = l.astype(BF16)
        l_lo = (l - l_hi.astype(F32)).astype(BF16)
        nat = _dot(perm_t_ref[...], jnp.concatenate([a, l_hi, l_lo], axis=1))
        a_nat.append(nat[:, 0:w])
        l_nat.append(nat[:, w:2 * w] + nat[:, 2 * w:3 * w])
    return jnp.concatenate(a_nat, axis=0), jnp.concatenate(l_nat, axis=0)


def _merge_body(x_ref, mod_ref, of_ref, ob_ref, sg_ref, a1_ref, l1_ref, a2_ref, l2_ref, a3_ref, l3_ref,
                ga_ref, gb_ref, gn_ref, n2_ref, fin_ref, wa_ref, wb_ref, wo_ref, wfi_ref, wfo_ref,
                p4t_ref, p16t_ref, y_ref, act_ref):
    mod = mod_ref[0]
    gate1, shift2, scale2, gate2 = mod[2:3], mod[3:4], mod[4:5], mod[5:6]

    o = of_ref[0].astype(F32) + ob_ref[0].astype(F32)
    o_a = jnp.concatenate([_rms(o[:, h * HG_DIM:(h + 1) * HG_DIM]) for h in range(HG_HEADS)], axis=1)
    o_a = o_a * gn_ref[...] * sg_ref[0]

    (a1, l1), (a2, l2), (a3, l3) = [
        _load_natural(a_ref, l_ref, perm_ref, r)
        for (_, r), a_ref, l_ref, perm_ref in zip(DILATION_PATTERNS, (a1_ref, a2_ref, a3_ref),
                                                  (l1_ref, l2_ref, l3_ref), (None, p4t_ref, p16t_ref))]
    top = jnp.maximum(jnp.maximum(l1, l2), l3)
    w1, w2, w3 = jnp.exp2(l1 - top), jnp.exp2(l2 - top), jnp.exp2(l3 - top)
    o_b = (a1 * w1 + a2 * w2 + a3 * w3) / (w1 + w2 + w3)

    merged = (ga_ref[0] * _dot(o_a.astype(BF16), wa_ref[...])
              + gb_ref[0] * _dot(o_b.astype(BF16), wb_ref[...]))
    x1 = x_ref[0] + gate1 * _dot(merged.astype(BF16), wo_ref[...])

    h2 = (_rms(x1) * n2_ref[...] * (1.0 + scale2) + shift2).astype(BF16)
    hidden = wfo_ref.shape[0]
    for c0 in range(0, hidden, FFN_CHUNK):
        gt = _dot(h2, wfi_ref[:, c0:c0 + FFN_CHUNK])
        up = _dot(h2, wfi_ref[:, hidden + c0:hidden + c0 + FFN_CHUNK])
        act_ref[:, c0:c0 + FFN_CHUNK] = (gt * _sigmoid(gt) * up).astype(BF16)
    x2 = x1 + gate2 * _dot(act_ref[...], wfo_ref[...])
    y_ref[0] = _rms(x2) * fin_ref[...]


def _merge_call(x, mod, o_f, o_b, sg, att, ga, gb, gn, n2, fin, wa, wb, wo, wfi, wfo, tm):
    B, L, D = x.shape
    hidden = wfo.shape[0]
    assert L % tm == 0 and tm % PERM_ROWS == 0 and hidden % FFN_CHUNK == 0 and wfi.shape[1] == 2 * hidden
    row = lambda w, r=1: pl.BlockSpec((1, tm // r, r * w), lambda b, i: (b, i, 0))
    att_flat = [t for pair in att for t in pair]
    att_specs = [row(ATT_WIDTH, r) for (_, r) in DILATION_PATTERNS for _ in range(2)]
    perms_t = [_dilation_perm(PERM_ROWS, r).T for (_, r) in DILATION_PATTERNS[1:]]
    return pl.pallas_call(
        _merge_body,
        grid=(B, L // tm),
        in_specs=[row(D), pl.BlockSpec((1,) + mod.shape[1:], lambda b, i: (b, 0, 0))]
                 + [row(512)] * 3 + att_specs + [row(D), row(D)]
                 + [_resident(t.shape) for t in (gn, n2, fin, wa, wb, wo, wfi, wfo, *perms_t)],
        out_specs=row(D),
        out_shape=jax.ShapeDtypeStruct((B, L, D), F32),
        scratch_shapes=[pltpu.VMEM((tm, hidden), BF16)],
        compiler_params=pltpu.CompilerParams(
            dimension_semantics=("parallel", "parallel"), vmem_limit_bytes=VMEM_LIMIT),
        name="merge_ffn",
    )(x, mod, o_f, o_b, sg, *att_flat, ga, gb, gn, n2, fin, wa, wb, wo, wfi, wfo, *perms_t)


def _rope_tables(L):
    half = ATT_HDIM // 2
    pos = jnp.arange(L, dtype=F32)
    inv = ROPE_THETA ** (-jnp.arange(half, dtype=F32) / half)
    ang = pos[:, None] * inv[None, :]
    cos, sin = jnp.cos(ang), jnp.sin(ang)
    return jnp.tile(jnp.concatenate([cos, cos], axis=1), (1, 2)), jnp.tile(jnp.concatenate([-sin, sin], axis=1), (1, 2))


def _trunk(x, mod, lb, p, tm_in, tm_out, t_scan):
    B, L, D = x.shape
    cos, sin_signed = _rope_tables(L)
    (q_h, lf_f, k_f, lf_b, k_b, v_h, sg, ga, gb, *qkv) = _inproj_call(
        x, mod, p["g1"], p["w_in"], p["b_in"], lb, cos, sin_signed, tm_in)
    o_f, o_b = _hgrn_call(q_h, lf_f, k_f, lf_b, k_b, v_h, t_scan)
    att = [_attn_call(*qkv[3 * i:3 * i + 3], r) for i, (_, r) in enumerate(DILATION_PATTERNS)]
    return _merge_call(x, mod, o_f, o_b, sg, att, ga, gb, p["gn"], p["n2"], p["fin"],
                       p["wa"], p["wb"], p["wo"], p["wfi"], p["wfo"], tm_out)


def kernel(x_prompt, x_sample, c_prompt, c_sample, w_ada, b_ada, norm1_g, w_in, b_in, lb_logits, hg_norm_g, w_branch_a, w_branch_b, w_out, norm2_g, w_ffn_in, w_ffn_out, final_norm_g):
    assert w_ada.shape[0] == 1 and lb_logits.shape[0] == 2, "single-layer trunk"
    D = x_prompt.shape[-1]
    bp, bs = c_prompt.shape[0], c_sample.shape[0]
    c_all = jnp.concatenate([c_prompt, c_sample], axis=0)
    c_pad = jnp.pad(c_all, ((0, -(bp + bs) % 8), (0, 0)))
    mod = _mod_call(c_pad, w_ada[0], b_ada[0]).reshape(c_pad.shape[0], 6, D)
    lb = _lb_call(lb_logits)
    row = lambda v: v.reshape(1, -1).astype(F32)
    p = dict(g1=row(norm1_g[0]), w_in=w_in[0].astype(BF16), b_in=row(b_in[0]), gn=row(hg_norm_g[0]),
             n2=row(norm2_g[0]), fin=row(final_norm_g), wa=w_branch_a[0].astype(BF16),
             wb=w_branch_b[0].astype(BF16), wo=w_out[0].astype(BF16), wfi=w_ffn_in[0].astype(BF16),
             wfo=w_ffn_out[0].astype(BF16))
    y_prompt = _trunk(x_prompt, mod[:bp], lb, p, 512, 512, 512)
    y_sample = _trunk(x_sample, mod[bp:bp + bs], lb, p, 512, 512, 512)
    return (y_prompt, y_sample)
```

```python
import numpy as np
import jax
import jax.numpy as jnp
from jax import lax
from jax.experimental import pallas as pl
from jax.experimental.pallas import tpu as pltpu

F32 = jnp.float32
BF16 = jnp.bfloat16

NORM_EPS = 1e-6
HG_HEADS = 4
HG_DIM = 128
HG_WIDTH = HG_HEADS * HG_DIM
ATT_HEADS = 8
ATT_HDIM = 64
ATT_WIDTH = ATT_HEADS * ATT_HDIM
DILATION_PATTERNS = ((128, 1), (512, 4), (2048, 16))
ATT_RADIUS = 64
ROPE_THETA = 10000.0
LOG2E = 1.4426950408889634
PROJ_BLOCK = 512
N_PROJ_BLOCKS = 12
PERM_ROWS = 256

HG_CHUNK = 64
HG_SUB = 16
HG_SUB_SHIFT = 4
HG_EXP2_CLAMP = 110.0
EXP2_ZERO = -1e30
VMEM_LIMIT = 56 * 1024 * 1024


def _dot(a, b):
    return jnp.dot(a, b, preferred_element_type=F32)


def _dot_nt(a, b):
    return lax.dot_general(a, b, (((1,), (1,)), ((), ())), preferred_element_type=F32)


def _dot_tn(a, b):
    return lax.dot_general(a, b, (((0,), (0,)), ((), ())), preferred_element_type=F32)


def _sigmoid(x):
    return 1.0 / (1.0 + jnp.exp(-x))


def _rms(x):
    return x * lax.rsqrt(jnp.mean(x * x, axis=-1, keepdims=True) + NORM_EPS)


def _resident(shape):
    nd = len(shape)
    return pl.BlockSpec(shape, lambda *_: (0,) * nd, pipeline_mode=pl.Buffered(1))


def _mod_body(c_ref, w_ref, b_ref, o_ref):
    c = c_ref[...]
    a = c * _sigmoid(c)
    o_ref[...] = jnp.dot(a, w_ref[...], preferred_element_type=F32,
                         precision=lax.Precision.HIGHEST) + b_ref[...]


def _mod_call(c_pad, w_ada, b_ada):
    rows, d = c_pad.shape
    n = w_ada.shape[1]
    bn = 2048
    return pl.pallas_call(
        _mod_body,
        grid=(n // bn,),
        in_specs=[pl.BlockSpec((rows, d), lambda j: (0, 0)),
                  pl.BlockSpec((d, bn), lambda j: (0, j)),
                  pl.BlockSpec((1, bn), lambda j: (0, j))],
        out_specs=pl.BlockSpec((rows, bn), lambda j: (0, j)),
        out_shape=jax.ShapeDtypeStruct((rows, n), F32),
        name="adaln_mod",
    )(c_pad, w_ada, b_ada.reshape(1, n))


def _lb_body(l_ref, o_ref):
    l = l_ref[...]
    e = jnp.exp(l - jnp.max(l, axis=0, keepdims=True))
    o_ref[...] = e[0:1] / jnp.sum(e, axis=0, keepdims=True)


def _lb_call(lb_logits):
    n = lb_logits.shape[0]
    flat = lb_logits.reshape(n, -1).astype(F32)
    return pl.pallas_call(
        _lb_body,
        out_shape=jax.ShapeDtypeStruct((1, flat.shape[1]), F32),
        name="hgrn_lower_bounds",
    )(flat)


def _rope(p, cos, sin_signed):
    n = p.shape[-1]
    lane = lax.broadcasted_iota(jnp.int32, p.shape, 1)
    first_half = (lane & 63) < 32
    partner = jnp.where(first_half, pltpu.roll(p, n - 32, 1), pltpu.roll(p, 32, 1))
    reps = n // cos.shape[-1]
    return p * jnp.tile(cos, (1, reps)) + partner * jnp.tile(sin_signed, (1, reps))


def _store_dilated(t_bf, perm_ref, out_ref, r):
    if r == 1:
        out_ref[0] = t_bf
        return
    tm, w = t_bf.shape
    rows = PERM_ROWS // r
    for sub in range(tm // PERM_ROWS):
        grouped = _dot(perm_ref[...], t_bf[sub * PERM_ROWS:(sub + 1) * PERM_ROWS]).astype(BF16)
        for c in range(r):
            out_ref[0, sub * rows:(sub + 1) * rows, c * w:(c + 1) * w] = grouped[c * rows:(c + 1) * rows]


def _inproj_body(x_ref, mod_ref, g1_ref, w_ref, b_ref, lb_ref, cos_ref, sin_ref, p4_ref, p16_ref,
                 qh_ref, lff_ref, kf_ref, lfb_ref, kb_ref, vh_ref, sg_ref, ga_ref, gb_ref,
                 q1_ref, k1_ref, v1_ref, q4_ref, k4_ref, v4_ref, q16_ref, k16_ref, v16_ref):
    x = x_ref[0]
    mod = mod_ref[0]
    shift1, scale1 = mod[0:1], mod[1:2]
    h = _rms(x) * g1_ref[...]
    h = h * (1.0 + scale1) + shift1
    hb = h.astype(BF16)

    def proj(j):
        cols = slice(j * PROJ_BLOCK, (j + 1) * PROJ_BLOCK)
        return _dot(hb, w_ref[:, cols]) + b_ref[:, cols]

    p = proj(0)
    qh_ref[0] = p * _sigmoid(p) * (HG_DIM ** -0.5)

    def gates(p, lb, lf_ref, k_ref):
        f = lb + (1.0 - lb) * _sigmoid(p)
        lf_ref[0] = jnp.log2(f)
        k_ref[0] = 1.0 - f

    gates(proj(1), lb_ref[:, 0:HG_WIDTH], lff_ref, kf_ref)
    gates(proj(2), lb_ref[:, HG_WIDTH:2 * HG_WIDTH], lfb_ref, kb_ref)
    vh_ref[0] = proj(3).astype(BF16)
    p = proj(4)
    sg_ref[0] = (p * _sigmoid(p)).astype(sg_ref.dtype)

    cos, sin_signed = cos_ref[...], sin_ref[...]
    q_a = (_rope(proj(5), cos, sin_signed) * (ATT_HDIM ** -0.5 * LOG2E)).astype(BF16)
    k_a = _rope(proj(6), cos, sin_signed).astype(BF16)
    v_a = proj(7).astype(BF16)
    for t, refs in ((q_a, (q1_ref, q4_ref, q16_ref)), (k_a, (k1_ref, k4_ref, k16_ref)),
                    (v_a, (v1_ref, v4_ref, v16_ref))):
        for (_, r), perm_ref, out_ref in zip(DILATION_PATTERNS, (None, p4_ref, p16_ref), refs):
            _store_dilated(t, perm_ref, out_ref, r)
    ga_ref[0, :, 0:PROJ_BLOCK] = _sigmoid(proj(8)).astype(ga_ref.dtype)
    ga_ref[0, :, PROJ_BLOCK:2 * PROJ_BLOCK] = _sigmoid(proj(9)).astype(ga_ref.dtype)
    gb_ref[0, :, 0:PROJ_BLOCK] = _sigmoid(proj(10)).astype(gb_ref.dtype)
    gb_ref[0, :, PROJ_BLOCK:2 * PROJ_BLOCK] = _sigmoid(proj(11)).astype(gb_ref.dtype)


def _dilation_perm(tm, r):
    i = np.arange(tm)
    src = r * (i % (tm // r)) + i // (tm // r)
    return jnp.asarray(src[:, None] == np.arange(tm)[None, :], dtype=BF16)


def _inproj_call(x, mod, g1, w_in, b_in, lb, cos, sin_signed, tm):
    B, L, D = x.shape
    n_in = w_in.shape[1]
    assert n_in == N_PROJ_BLOCKS * PROJ_BLOCK and L % tm == 0 and tm % PERM_ROWS == 0
    row = lambda w, r=1: pl.BlockSpec((1, tm // r, r * w), lambda b, i: (b, i, 0))
    tab = pl.BlockSpec((tm, cos.shape[1]), lambda b, i: (i, 0))
    shp = lambda w, dt, r=1: jax.ShapeDtypeStruct((B, L // r, r * w), dt)
    perms = [_dilation_perm(PERM_ROWS, r) for (_, r) in DILATION_PATTERNS[1:]]
    att_specs = [row(ATT_WIDTH, r) for (_, r) in DILATION_PATTERNS for _ in range(3)]
    att_shapes = [shp(ATT_WIDTH, BF16, r) for (_, r) in DILATION_PATTERNS for _ in range(3)]
    return pl.pallas_call(
        _inproj_body,
        grid=(B, L // tm),
        in_specs=[row(D),
                  pl.BlockSpec((1,) + mod.shape[1:], lambda b, i: (b, 0, 0)),
                  _resident(g1.shape), _resident(w_in.shape), _resident(b_in.shape), _resident(lb.shape),
                  tab, tab] + [_resident(p.shape) for p in perms],
        out_specs=[row(512)] * 7 + [row(1024)] * 2 + att_specs,
        out_shape=[shp(512, F32), shp(512, F32), shp(512, F32), shp(512, F32), shp(512, F32),
                   shp(512, BF16), shp(512, BF16), shp(1024, BF16), shp(1024, BF16)] + att_shapes,
        compiler_params=pltpu.CompilerParams(
            dimension_semantics=("parallel", "parallel"), vmem_limit_bytes=VMEM_LIMIT),
        name="inproj",
    )(x, mod, g1, w_in, b_in, lb, cos, sin_signed, *perms)


def _bcast_row(c, group, r):
    T, w = c.shape
    c3 = c.reshape(T // group, group, w)
    return jnp.broadcast_to(c3[:, r:r + 1, :], c3.shape).reshape(T, w)


def _split3(g):
    g1 = g.astype(BF16)
    r1 = g - g1.astype(F32)
    g2 = r1.astype(BF16)
    return g1, g2, (r1 - g2.astype(F32)).astype(BF16)


def _hgrn_dir(q, k, v_bf, g, s_in, rev, o_ref):
    T, dk = q.shape
    C, nch, nsb = HG_CHUNK, T // HG_CHUNK, HG_CHUNK // HG_SUB
    t = lax.broadcasted_iota(jnp.int32, (C, C), 0)
    s = lax.broadcasted_iota(jnp.int32, (C, C), 1)
    order = (s >= t) if rev else (s <= t)
    inner_mask = order & ((t >> HG_SUB_SHIFT) == (s >> HG_SUB_SHIFT))
    tri = order.astype(BF16)

    g_terms = jnp.concatenate(_split3(g), axis=1)
    c_parts = []
    for ci in range(nch):
        r = _dot(tri, g_terms[ci * C:(ci + 1) * C])
        c_parts.append(r[:, 0:dk] + r[:, dk:2 * dk] + r[:, 2 * dk:3 * dk])
    c = jnp.concatenate(c_parts, axis=0)
    yield

    far = 0 if rev else HG_SUB - 1
    edge = _bcast_row(c, HG_SUB, far)
    mid = _bcast_row(c, HG_SUB, HG_SUB // 2)
    c_end = _bcast_row(c, C, 0 if rev else C - 1)
    rowc = lax.broadcasted_iota(jnp.int32, (T, dk), 0) & (C - 1)

    k_edge = k * jnp.exp2(edge - c)
    q_cat, k_cat = [], []
    for J in (range(1, nsb) if rev else range(nsb - 1)):
        ref_j = _bcast_row(c, C, J * HG_SUB + far)
        queries = (rowc < J * HG_SUB) if rev else (rowc >= (J + 1) * HG_SUB)
        q_cat.append((q * jnp.exp2(jnp.where(queries, c - ref_j, EXP2_ZERO))).astype(BF16))
        in_j = (rowc >= J * HG_SUB) & (rowc < (J + 1) * HG_SUB)
        k_cat.append(jnp.where(in_j, k_edge, 0.0).astype(BF16))
    q_cat = jnp.concatenate(q_cat, axis=1)
    k_cat = jnp.concatenate(k_cat, axis=1)

    d_mid = jnp.clip(c - mid, -HG_EXP2_CLAMP, HG_EXP2_CLAMP)
    q_mid = (q * jnp.exp2(d_mid)).astype(BF16)
    k_mid = (k * jnp.exp2(-d_mid)).astype(BF16)
    q_dec = (q * jnp.exp2(c)).astype(BF16)
    k_end = (k * jnp.exp2(c_end - c)).astype(BF16)
    decay = jnp.exp2(c_end)
    yield

    o_intra, upd = [], []
    for ci in range(nch):
        rows = slice(ci * C, (ci + 1) * C)
        sc = _dot_nt(q_cat[rows], k_cat[rows])
        sc = sc + jnp.where(inner_mask, _dot_nt(q_mid[rows], k_mid[rows]), 0.0)
        o_intra.append(_dot(sc.astype(BF16), v_bf[rows]))
        upd.append(_dot_tn(v_bf[rows], k_end[rows]))
        yield

    s_t = s_in
    for ci in (reversed(range(nch)) if rev else range(nch)):
        rows = slice(ci * C, (ci + 1) * C)
        o_ref[0, rows] = (o_intra[ci] + _dot_nt(q_dec[rows], s_t.astype(BF16))).astype(o_ref.dtype)
        s_t = s_t * decay[ci * C:ci * C + 1] + upd[ci]
        yield
    return s_t


def _hgrn_body(qf_ref, lff_ref, kf_ref, vf_ref, qb_ref, lfb_ref, kb_ref, vb_ref,
               of_ref, ob_ref, sf_ref, sb_ref):
    @pl.when(pl.program_id(2) == 0)
    def _():
        sf_ref[...] = jnp.zeros_like(sf_ref)
        sb_ref[...] = jnp.zeros_like(sb_ref)

    runs = [(_hgrn_dir(qf_ref[0], kf_ref[0], vf_ref[0], lff_ref[0], sf_ref[...], False, of_ref), sf_ref),
            (_hgrn_dir(qb_ref[0], kb_ref[0], vb_ref[0], lfb_ref[0], sb_ref[...], True, ob_ref), sb_ref)]
    while runs:
        for run in list(runs):
            gen, state_ref = run
            try:
                next(gen)
            except StopIteration as done:
                state_ref[...] = done.value
                runs.remove(run)


def _hgrn_call(q, lf_f, k_f, lf_b, k_b, v, T):
    B, L, W = q.shape
    nb = L // T
    assert L % T == 0 and T % HG_CHUNK == 0 and W == HG_WIDTH
    fwd = pl.BlockSpec((1, T, HG_DIM), lambda b, h, j: (b, j, h))
    bwd = pl.BlockSpec((1, T, HG_DIM), lambda b, h, j: (b, nb - 1 - j, h))
    return pl.pallas_call(
        _hgrn_body,
        grid=(B, HG_HEADS, nb),
        in_specs=[fwd, fwd, fwd, fwd, bwd, bwd, bwd, bwd],
        out_specs=[fwd, bwd],
        out_shape=[jax.ShapeDtypeStruct((B, L, W), BF16)] * 2,
        scratch_shapes=[pltpu.VMEM((HG_DIM, HG_DIM), F32)] * 2,
        compiler_params=pltpu.CompilerParams(
            dimension_semantics=("parallel", "parallel", "arbitrary"), vmem_limit_bytes=VMEM_LIMIT),
        name="hgrn_scan",
    )(q, lf_f, k_f, v, q, lf_b, k_b, v)


ATT_QBLK = 2 * ATT_RADIUS
ATT_STEP_BLOCKS = 4


def _attn_body(q_ref, kp_ref, kc_ref, kn_ref, vp_ref, vc_ref, vn_ref, o_ref, lse_ref):
    n = pl.program_id(2)
    last = pl.num_programs(2) - 1
    kcat = jnp.concatenate([kp_ref[0], kc_ref[0], kn_ref[0]], axis=0)
    vcat = jnp.concatenate([vp_ref[0], vc_ref[0], vn_ref[0]], axis=0)
    nq, nk = ATT_QBLK, ATT_QBLK + 2 * ATT_RADIUS
    nblk = q_ref.shape[1] // nq
    i = lax.broadcasted_iota(jnp.int32, (nq, nk), 0)
    j = lax.broadcasted_iota(jnp.int32, (nq, nk), 1)
    band = (j >= i) & (j <= i + 2 * ATT_RADIUS)
    has_prev = (j >= ATT_RADIUS) | (n > 0)
    has_next = (j < nq + ATT_RADIUS) | (n < last)

    pair_w = 2 * ATT_HDIM
    npairs = ATT_HEADS // 2
    lane = lax.broadcasted_iota(jnp.int32, kcat.shape, 1) & (pair_w - 1)
    mask_a = jnp.where(lane < ATT_HDIM, 1.0, 0.0).astype(BF16)
    mask_b = jnp.where(lane < ATT_HDIM, 0.0, 1.0).astype(BF16)
    k_a, k_b, v_a, v_b = kcat * mask_a, kcat * mask_b, vcat * mask_a, vcat * mask_b
    lo_k = lax.broadcasted_iota(jnp.int32, (nk, pair_w), 1) < ATT_HDIM
    ones_a = jnp.where(lo_k, 1.0, 0.0).astype(BF16)
    ones_b = jnp.where(lo_k, 0.0, 1.0).astype(BF16)
    lo_q = lax.broadcasted_iota(jnp.int32, (nq, pair_w), 1) < ATT_HDIM

    for blk in range(nblk):
        valid = band
        if blk == 0:
            valid = valid & has_prev
        if blk == nblk - 1:
            valid = valid & has_next
        qrows = slice(blk * nq, (blk + 1) * nq)
        krows = slice(blk * nq, blk * nq + nk)
        scores = []
        for p in range(npairs):
            cols = slice(p * pair_w, (p + 1) * pair_w)
            q = q_ref[0, qrows, cols]
            scores.append(_dot_nt(q, k_a[krows, cols]))
            scores.append(_dot_nt(q, k_b[krows, cols]))
        scores = [jnp.where(valid, s, -jnp.inf) for s in scores]
        tops = [jnp.max(s, axis=-1, keepdims=True) for s in scores]
        probs = [jnp.exp2(s - m).astype(BF16) for s, m in zip(scores, tops)]
        for p in range(npairs):
            cols = slice(p * pair_w, (p + 1) * pair_w)
            res = (_dot(probs[2 * p], jnp.concatenate([v_a[krows, cols], ones_a], axis=1))
                   + _dot(probs[2 * p + 1], jnp.concatenate([v_b[krows, cols], ones_b], axis=1)))
            num, den = res[:, 0:pair_w], res[:, pair_w:2 * pair_w]
            o_ref[0, qrows, cols] = (num / den).astype(o_ref.dtype)
            lse_ref[0, qrows, cols] = jnp.where(lo_q, tops[2 * p], tops[2 * p + 1]) + jnp.log2(den)


def _attn_call(q, k, v, r):
    B, Lr, rW = q.shape
    W = rW // r
    step = ATT_QBLK * min(ATT_STEP_BLOCKS, Lr // ATT_QBLK)
    assert W == ATT_WIDTH and Lr % step == 0
    per_step = step // ATT_RADIUS
    nhalo = Lr // ATT_RADIUS
    center = pl.BlockSpec((1, step, W), lambda b, c, n: (b, n, c))
    prev = pl.BlockSpec((1, ATT_RADIUS, W), lambda b, c, n: (b, jnp.maximum(per_step * n - 1, 0), c))
    nxt = pl.BlockSpec((1, ATT_RADIUS, W), lambda b, c, n: (b, jnp.minimum(per_step * (n + 1), nhalo - 1), c))
    return pl.pallas_call(
        _attn_body,
        grid=(B, r, Lr // step),
        in_specs=[center, prev, center, nxt, prev, center, nxt],
        out_specs=[center, center],
        out_shape=[jax.ShapeDtypeStruct((B, Lr, rW), BF16), jax.ShapeDtypeStruct((B, Lr, rW), F32)],
        compiler_params=pltpu.CompilerParams(
            dimension_semantics=("parallel", "parallel", "parallel"), vmem_limit_bytes=VMEM_LIMIT),
        name=f"dilated_attn_r{r}",
    )(q, k, k, k, v, v, v)


FFN_CHUNK = 256


def _load_natural(a_ref, l_ref, perm_t_ref, r):
    if r == 1:
        return a_ref[0].astype(F32), l_ref[0]
    w = a_ref.shape[2] // r
    rows = PERM_ROWS // r
    a_nat, l_nat = [], []
    for sub in range(a_ref.shape[1] // rows):
        blk = slice(sub * rows, (sub + 1) * rows)
        a = jnp.concatenate([a_ref[0, blk, c * w:(c + 1) * w] for c in range(r)], axis=0)
        l = jnp.concatenate([l_ref[0, blk, c * w:(c + 1) * w] for c in range(r)], axis=0)
        l_hi = l.astype(BF16)
        l_lo = (l - l_hi.astype(F32)).astype(BF16)
        nat = _dot(perm_t_ref[...], jnp.concatenate([a, l_hi, l_lo], axis=1))
        a_nat.append(nat[:, 0:w])
        l_nat.append(nat[:, w:2 * w] + nat[:, 2 * w:3 * w])
    return jnp.concatenate(a_nat, axis=0), jnp.concatenate(l_nat, axis=0)


def _merge_body(x_ref, mod_ref, of_ref, ob_ref, sg_ref, a1_ref, l1_ref, a2_ref, l2_ref, a3_ref, l3_ref,
                ga_ref, gb_ref, gn_ref, n2_ref, fin_ref, wa_ref, wb_ref, wo_ref, wfi_ref, wfo_ref,
                p4t_ref, p16t_ref, y_ref, act_ref):
    mod = mod_ref[0]
    gate1, shift2, scale2, gate2 = mod[2:3], mod[3:4], mod[4:5], mod[5:6]

    o = of_ref[0].astype(F32) + ob_ref[0].astype(F32)
    o_a = jnp.concatenate([_rms(o[:, h * HG_DIM:(h + 1) * HG_DIM]) for h in range(HG_HEADS)], axis=1)
    o_a = o_a * gn_ref[...] * sg_ref[0]

    (a1, l1), (a2, l2), (a3, l3) = [
        _load_natural(a_ref, l_ref, perm_ref, r)
        for (_, r), a_ref, l_ref, perm_ref in zip(DILATION_PATTERNS, (a1_ref, a2_ref, a3_ref),
                                                  (l1_ref, l2_ref, l3_ref), (None, p4t_ref, p16t_ref))]
    top = jnp.maximum(jnp.maximum(l1, l2), l3)
    w1, w2, w3 = jnp.exp2(l1 - top), jnp.exp2(l2 - top), jnp.exp2(l3 - top)
    o_b = (a1 * w1 + a2 * w2 + a3 * w3) / (w1 + w2 + w3)

    merged = (ga_ref[0] * _dot(o_a.astype(BF16), wa_ref[...])
              + gb_ref[0] * _dot(o_b.astype(BF16), wb_ref[...]))
    x1 = x_ref[0] + gate1 * _dot(merged.astype(BF16), wo_ref[...])

    h2 = (_rms(x1) * n2_ref[...] * (1.0 + scale2) + shift2).astype(BF16)
    hidden = wfo_ref.shape[0]
    for c0 in range(0, hidden, FFN_CHUNK):
        gt = _dot(h2, wfi_ref[:, c0:c0 + FFN_CHUNK])
        up = _dot(h2, wfi_ref[:, hidden + c0:hidden + c0 + FFN_CHUNK])
        act_ref[:, c0:c0 + FFN_CHUNK] = (gt * _sigmoid(gt) * up).astype(BF16)
    x2 = x1 + gate2 * _dot(act_ref[...], wfo_ref[...])
    y_ref[0] = _rms(x2) * fin_ref[...]


def _merge_call(x, mod, o_f, o_b, sg, att, ga, gb, gn, n2, fin, wa, wb, wo, wfi, wfo, tm):
    B, L, D = x.shape
    hidden = wfo.shape[0]
    assert L % tm == 0 and tm % PERM_ROWS == 0 and hidden % FFN_CHUNK == 0 and wfi.shape[1] == 2 * hidden
    row = lambda w, r=1: pl.BlockSpec((1, tm // r, r * w), lambda b, i: (b, i, 0))
    att_flat = [t for pair in att for t in pair]
    att_specs = [row(ATT_WIDTH, r) for (_, r) in DILATION_PATTERNS for _ in range(2)]
    perms_t = [_dilation_perm(PERM_ROWS, r).T for (_, r) in DILATION_PATTERNS[1:]]
    return pl.pallas_call(
        _merge_body,
        grid=(B, L // tm),
        in_specs=[row(D), pl.BlockSpec((1,) + mod.shape[1:], lambda b, i: (b, 0, 0))]
                 + [row(512)] * 3 + att_specs + [row(D), row(D)]
                 + [_resident(t.shape) for t in (gn, n2, fin, wa, wb, wo, wfi, wfo, *perms_t)],
        out_specs=row(D),
        out_shape=jax.ShapeDtypeStruct((B, L, D), F32),
        scratch_shapes=[pltpu.VMEM((tm, hidden), BF16)],
        compiler_params=pltpu.CompilerParams(
            dimension_semantics=("parallel", "parallel"), vmem_limit_bytes=VMEM_LIMIT),
        name="merge_ffn",
    )(x, mod, o_f, o_b, sg, *att_flat, ga, gb, gn, n2, fin, wa, wb, wo, wfi, wfo, *perms_t)


def _rope_tables(L):
    half = ATT_HDIM // 2
    lane = np.arange(2 * ATT_HDIM)
    inv = ROPE_THETA ** (-jnp.asarray(lane % half, dtype=F32) / half)
    sign = jnp.asarray(np.where(lane % ATT_HDIM < half, -1.0, 1.0), dtype=F32)
    ang = jnp.arange(L, dtype=F32)[:, None] * inv[None, :]
    return jnp.cos(ang), jnp.sin(ang) * sign[None, :]


def _trunk(x, mod, lb, p, tm_in, tm_out, t_scan):
    B, L, D = x.shape
    cos, sin_signed = _rope_tables(L)
    (q_h, lf_f, k_f, lf_b, k_b, v_h, sg, ga, gb, *qkv) = _inproj_call(
        x, mod, p["g1"], p["w_in"], p["b_in"], lb, cos, sin_signed, tm_in)
    o_f, o_b = _hgrn_call(q_h, lf_f, k_f, lf_b, k_b, v_h, t_scan)
    att = [_attn_call(*qkv[3 * i:3 * i + 3], r) for i, (_, r) in enumerate(DILATION_PATTERNS)]
    return _merge_call(x, mod, o_f, o_b, sg, att, ga, gb, p["gn"], p["n2"], p["fin"],
                       p["wa"], p["wb"], p["wo"], p["wfi"], p["wfo"], tm_out)


def kernel(x_prompt, x_sample, c_prompt, c_sample, w_ada, b_ada, norm1_g, w_in, b_in, lb_logits, hg_norm_g, w_branch_a, w_branch_b, w_out, norm2_g, w_ffn_in, w_ffn_out, final_norm_g):
    assert w_ada.shape[0] == 1 and lb_logits.shape[0] == 2, "single-layer trunk"
    D = x_prompt.shape[-1]
    bp, bs = c_prompt.shape[0], c_sample.shape[0]
    c_all = jnp.concatenate([c_prompt, c_sample], axis=0)
    c_pad = jnp.pad(c_all, ((0, -(bp + bs) % 8), (0, 0)))
    mod = _mod_call(c_pad, w_ada[0], b_ada[0]).reshape(c_pad.shape[0], 6, D)
    lb = _lb_call(lb_logits)
    row = lambda v: v.reshape(1, -1).astype(F32)
    p = dict(g1=row(norm1_g[0]), w_in=w_in[0].astype(BF16), b_in=row(b_in[0]), gn=row(hg_norm_g[0]),
             n2=row(norm2_g[0]), fin=row(final_norm_g), wa=w_branch_a[0].astype(BF16),
             wb=w_branch_b[0].astype(BF16), wo=w_out[0].astype(BF16), wfi=w_ffn_in[0].astype(BF16),
             wfo=w_ffn_out[0].astype(BF16))
    y_prompt = _trunk(x_prompt, mod[:bp], lb, p, 512, 512, 1024)
    y_sample = _trunk(x_sample, mod[bp:bp + bs], lb, p, 512, 512, 1024)
    return (y_prompt, y_sample)
```

```python
import numpy as np
import jax
import jax.numpy as jnp
from jax import lax
from jax.experimental import pallas as pl
from jax.experimental.pallas import tpu as pltpu

F32 = jnp.float32
BF16 = jnp.bfloat16

NORM_EPS = 1e-6
HG_HEADS = 4
HG_DIM = 128
HG_WIDTH = HG_HEADS * HG_DIM
ATT_HEADS = 8
ATT_HDIM = 64
ATT_WIDTH = ATT_HEADS * ATT_HDIM
DILATION_PATTERNS = ((128, 1), (512, 4), (2048, 16))
ATT_RADIUS = 64
ROPE_THETA = 10000.0
LOG2E = 1.4426950408889634
PROJ_BLOCK = 512
N_PROJ_BLOCKS = 12
PERM_ROWS = 256

HG_CHUNK = 64
HG_SUB = 16
HG_SUB_SHIFT = 4
HG_STEP_HEADS = 2
HG_EXP2_CLAMP = 110.0
EXP2_ZERO = -1e30
VMEM_LIMIT = 56 * 1024 * 1024


def _dot(a, b):
    return jnp.dot(a, b, preferred_element_type=F32)


def _dot_nt(a, b):
    return lax.dot_general(a, b, (((1,), (1,)), ((), ())), preferred_element_type=F32)


def _dot_tn(a, b):
    return lax.dot_general(a, b, (((0,), (0,)), ((), ())), preferred_element_type=F32)


def _sigmoid(x):
    return 1.0 / (1.0 + jnp.exp(-x))


def _rms(x):
    return x * lax.rsqrt(jnp.mean(x * x, axis=-1, keepdims=True) + NORM_EPS)


def _resident(shape):
    nd = len(shape)
    return pl.BlockSpec(shape, lambda *_: (0,) * nd, pipeline_mode=pl.Buffered(1))


def _mod_body(c_ref, w_ref, b_ref, o_ref):
    c = c_ref[...]
    a = c * _sigmoid(c)
    o_ref[...] = jnp.dot(a, w_ref[...], preferred_element_type=F32,
                         precision=lax.Precision.HIGHEST) + b_ref[...]


def _mod_call(c_pad, w_ada, b_ada):
    rows, d = c_pad.shape
    n = w_ada.shape[1]
    bn = 2048
    return pl.pallas_call(
        _mod_body,
        grid=(n // bn,),
        in_specs=[pl.BlockSpec((rows, d), lambda j: (0, 0)),
                  pl.BlockSpec((d, bn), lambda j: (0, j)),
                  pl.BlockSpec((1, bn), lambda j: (0, j))],
        out_specs=pl.BlockSpec((rows, bn), lambda j: (0, j)),
        out_shape=jax.ShapeDtypeStruct((rows, n), F32),
        name="adaln_mod",
    )(c_pad, w_ada, b_ada.reshape(1, n))


def _lb_body(l_ref, o_ref):
    l = l_ref[...]
    e = jnp.exp(l - jnp.max(l, axis=0, keepdims=True))
    o_ref[...] = e[0:1] / jnp.sum(e, axis=0, keepdims=True)


def _lb_call(lb_logits):
    n = lb_logits.shape[0]
    flat = lb_logits.reshape(n, -1).astype(F32)
    return pl.pallas_call(
        _lb_body,
        out_shape=jax.ShapeDtypeStruct((1, flat.shape[1]), F32),
        name="hgrn_lower_bounds",
    )(flat)


def _rope(p, cos, sin_signed):
    n = p.shape[-1]
    lane = lax.broadcasted_iota(jnp.int32, p.shape, 1)
    first_half = (lane & 63) < 32
    partner = jnp.where(first_half, pltpu.roll(p, n - 32, 1), pltpu.roll(p, 32, 1))
    reps = n // cos.shape[-1]
    return p * jnp.tile(cos, (1, reps)) + partner * jnp.tile(sin_signed, (1, reps))


def _store_dilated(t_bf, perm_ref, out_ref, r):
    if r == 1:
        out_ref[0] = t_bf
        return
    tm, w = t_bf.shape
    rows = PERM_ROWS // r
    for sub in range(tm // PERM_ROWS):
        grouped = _dot(perm_ref[...], t_bf[sub * PERM_ROWS:(sub + 1) * PERM_ROWS]).astype(BF16)
        for c in range(r):
            out_ref[0, sub * rows:(sub + 1) * rows, c * w:(c + 1) * w] = grouped[c * rows:(c + 1) * rows]


def _inproj_body(x_ref, mod_ref, g1_ref, w_ref, b_ref, lb_ref, cos_ref, sin_ref, p4_ref, p16_ref,
                 qh_ref, lff_ref, kf_ref, lfb_ref, kb_ref, vh_ref, sg_ref, ga_ref, gb_ref,
                 q1_ref, k1_ref, v1_ref, q4_ref, k4_ref, v4_ref, q16_ref, k16_ref, v16_ref):
    x = x_ref[0]
    mod = mod_ref[0]
    shift1, scale1 = mod[0:1], mod[1:2]
    h = _rms(x) * g1_ref[...]
    h = h * (1.0 + scale1) + shift1
    hb = h.astype(BF16)

    def proj(j):
        cols = slice(j * PROJ_BLOCK, (j + 1) * PROJ_BLOCK)
        return _dot(hb, w_ref[:, cols]) + b_ref[:, cols]

    p = proj(0)
    qh_ref[0] = p * _sigmoid(p) * (HG_DIM ** -0.5)

    def gates(p, lb, lf_ref, k_ref):
        f = lb + (1.0 - lb) * _sigmoid(p)
        lf_ref[0] = jnp.log2(f)
        k_ref[0] = 1.0 - f

    gates(proj(1), lb_ref[:, 0:HG_WIDTH], lff_ref, kf_ref)
    gates(proj(2), lb_ref[:, HG_WIDTH:2 * HG_WIDTH], lfb_ref, kb_ref)
    vh_ref[0] = proj(3).astype(BF16)
    p = proj(4)
    sg_ref[0] = (p * _sigmoid(p)).astype(sg_ref.dtype)

    cos, sin_signed = cos_ref[...], sin_ref[...]
    q_a = (_rope(proj(5), cos, sin_signed) * (ATT_HDIM ** -0.5 * LOG2E)).astype(BF16)
    k_a = _rope(proj(6), cos, sin_signed).astype(BF16)
    v_a = proj(7).astype(BF16)
    for t, refs in ((q_a, (q1_ref, q4_ref, q16_ref)), (k_a, (k1_ref, k4_ref, k16_ref)),
                    (v_a, (v1_ref, v4_ref, v16_ref))):
        for (_, r), perm_ref, out_ref in zip(DILATION_PATTERNS, (None, p4_ref, p16_ref), refs):
            _store_dilated(t, perm_ref, out_ref, r)
    ga_ref[0, :, 0:PROJ_BLOCK] = _sigmoid(proj(8)).astype(ga_ref.dtype)
    ga_ref[0, :, PROJ_BLOCK:2 * PROJ_BLOCK] = _sigmoid(proj(9)).astype(ga_ref.dtype)
    gb_ref[0, :, 0:PROJ_BLOCK] = _sigmoid(proj(10)).astype(gb_ref.dtype)
    gb_ref[0, :, PROJ_BLOCK:2 * PROJ_BLOCK] = _sigmoid(proj(11)).astype(gb_ref.dtype)


def _dilation_perm(tm, r):
    i = np.arange(tm)
    src = r * (i % (tm // r)) + i // (tm // r)
    return jnp.asarray(src[:, None] == np.arange(tm)[None, :], dtype=BF16)


def _inproj_call(x, mod, g1, w_in, b_in, lb, cos, sin_signed, tm):
    B, L, D = x.shape
    n_in = w_in.shape[1]
    assert n_in == N_PROJ_BLOCKS * PROJ_BLOCK and L % tm == 0 and tm % PERM_ROWS == 0
    row = lambda w, r=1: pl.BlockSpec((1, tm // r, r * w), lambda b, i: (b, i, 0))
    tab = pl.BlockSpec((tm, cos.shape[1]), lambda b, i: (i, 0))
    shp = lambda w, dt, r=1: jax.ShapeDtypeStruct((B, L // r, r * w), dt)
    perms = [_dilation_perm(PERM_ROWS, r) for (_, r) in DILATION_PATTERNS[1:]]
    att_specs = [row(ATT_WIDTH, r) for (_, r) in DILATION_PATTERNS for _ in range(3)]
    att_shapes = [shp(ATT_WIDTH, BF16, r) for (_, r) in DILATION_PATTERNS for _ in range(3)]
    return pl.pallas_call(
        _inproj_body,
        grid=(B, L // tm),
        in_specs=[row(D),
                  pl.BlockSpec((1,) + mod.shape[1:], lambda b, i: (b, 0, 0)),
                  _resident(g1.shape), _resident(w_in.shape), _resident(b_in.shape), _resident(lb.shape),
                  tab, tab] + [_resident(p.shape) for p in perms],
        out_specs=[row(512)] * 7 + [row(1024)] * 2 + att_specs,
        out_shape=[shp(512, F32), shp(512, F32), shp(512, F32), shp(512, F32), shp(512, F32),
                   shp(512, BF16), shp(512, BF16), shp(1024, BF16), shp(1024, BF16)] + att_shapes,
        compiler_params=pltpu.CompilerParams(
            dimension_semantics=("parallel", "parallel"), vmem_limit_bytes=VMEM_LIMIT),
        name="inproj",
    )(x, mod, g1, w_in, b_in, lb, cos, sin_signed, *perms)


def _bcast_row(c, group, r):
    T, w = c.shape
    c3 = c.reshape(T // group, group, w)
    return jnp.broadcast_to(c3[:, r:r + 1, :], c3.shape).reshape(T, w)


def _split3(g):
    g1 = g.astype(BF16)
    r1 = g - g1.astype(F32)
    g2 = r1.astype(BF16)
    return g1, g2, (r1 - g2.astype(F32)).astype(BF16)


def _hgrn_dir(q, k, v_bf, g, s_in, rev, o_ref, cols):
    T, dk = q.shape
    C, nch, nsb = HG_CHUNK, T // HG_CHUNK, HG_CHUNK // HG_SUB
    t = lax.broadcasted_iota(jnp.int32, (C, C), 0)
    s = lax.broadcasted_iota(jnp.int32, (C, C), 1)
    order = (s >= t) if rev else (s <= t)
    inner_mask = order & ((t >> HG_SUB_SHIFT) == (s >> HG_SUB_SHIFT))
    tri = order.astype(BF16)

    g_terms = jnp.concatenate(_split3(g), axis=1)
    c_parts = []
    for ci in range(nch):
        r = _dot(tri, g_terms[ci * C:(ci + 1) * C])
        c_parts.append(r[:, 0:dk] + r[:, dk:2 * dk] + r[:, 2 * dk:3 * dk])
    c = jnp.concatenate(c_parts, axis=0)
    yield

    far = 0 if rev else HG_SUB - 1
    edge = _bcast_row(c, HG_SUB, far)
    mid = _bcast_row(c, HG_SUB, HG_SUB // 2)
    c_end = _bcast_row(c, C, 0 if rev else C - 1)
    rowc = lax.broadcasted_iota(jnp.int32, (T, dk), 0) & (C - 1)

    k_edge = k * jnp.exp2(edge - c)
    q_cat, k_cat = [], []
    for J in (range(1, nsb) if rev else range(nsb - 1)):
        ref_j = _bcast_row(c, C, J * HG_SUB + far)
        queries = (rowc < J * HG_SUB) if rev else (rowc >= (J + 1) * HG_SUB)
        q_cat.append((q * jnp.exp2(jnp.where(queries, c - ref_j, EXP2_ZERO))).astype(BF16))
        in_j = (rowc >= J * HG_SUB) & (rowc < (J + 1) * HG_SUB)
        k_cat.append(jnp.where(in_j, k_edge, 0.0).astype(BF16))
    q_cat = jnp.concatenate(q_cat, axis=1)
    k_cat = jnp.concatenate(k_cat, axis=1)

    d_mid = jnp.clip(c - mid, -HG_EXP2_CLAMP, HG_EXP2_CLAMP)
    q_mid = (q * jnp.exp2(d_mid)).astype(BF16)
    k_mid = (k * jnp.exp2(-d_mid)).astype(BF16)
    q_dec = (q * jnp.exp2(c)).astype(BF16)
    k_end = (k * jnp.exp2(c_end - c)).astype(BF16)
    decay = jnp.exp2(c_end)
    yield

    o_intra, upd = [], []
    for ci in range(nch):
        rows = slice(ci * C, (ci + 1) * C)
        sc = _dot_nt(q_cat[rows], k_cat[rows])
        sc = sc + jnp.where(inner_mask, _dot_nt(q_mid[rows], k_mid[rows]), 0.0)
        o_intra.append(_dot(sc.astype(BF16), v_bf[rows]))
        upd.append(_dot_tn(v_bf[rows], k_end[rows]))
        yield

    s_t = s_in
    for ci in (reversed(range(nch)) if rev else range(nch)):
        rows = slice(ci * C, (ci + 1) * C)
        o_ref[0, rows, cols] = (o_intra[ci] + _dot_nt(q_dec[rows], s_t.astype(BF16))).astype(o_ref.dtype)
        s_t = s_t * decay[ci * C:ci * C + 1] + upd[ci]
        yield
    return s_t


def _hgrn_body(qf_ref, lff_ref, kf_ref, vf_ref, qb_ref, lfb_ref, kb_ref, vb_ref,
               of_ref, ob_ref, sf_ref, sb_ref):
    @pl.when(pl.program_id(2) == 0)
    def _():
        sf_ref[...] = jnp.zeros_like(sf_ref)
        sb_ref[...] = jnp.zeros_like(sb_ref)

    runs = []
    for h in range(HG_STEP_HEADS):
        cols = slice(h * HG_DIM, (h + 1) * HG_DIM)
        runs.append((_hgrn_dir(qf_ref[0, :, cols], kf_ref[0, :, cols], vf_ref[0, :, cols], lff_ref[0, :, cols],
                               sf_ref[h], False, of_ref, cols), sf_ref, h))
        runs.append((_hgrn_dir(qb_ref[0, :, cols], kb_ref[0, :, cols], vb_ref[0, :, cols], lfb_ref[0, :, cols],
                               sb_ref[h], True, ob_ref, cols), sb_ref, h))
    while runs:
        for run in list(runs):
            gen, state_ref, h = run
            try:
                next(gen)
            except StopIteration as done:
                state_ref[h] = done.value
                runs.remove(run)


def _hgrn_call(q, lf_f, k_f, lf_b, k_b, v, T):
    B, L, W = q.shape
    nb = L // T
    assert L % T == 0 and T % HG_CHUNK == 0 and W == HG_WIDTH
    width = HG_STEP_HEADS * HG_DIM
    fwd = pl.BlockSpec((1, T, width), lambda b, h, j: (b, j, h))
    bwd = pl.BlockSpec((1, T, width), lambda b, h, j: (b, nb - 1 - j, h))
    return pl.pallas_call(
        _hgrn_body,
        grid=(B, HG_HEADS // HG_STEP_HEADS, nb),
        in_specs=[fwd, fwd, fwd, fwd, bwd, bwd, bwd, bwd],
        out_specs=[fwd, bwd],
        out_shape=[jax.ShapeDtypeStruct((B, L, W), BF16)] * 2,
        scratch_shapes=[pltpu.VMEM((HG_STEP_HEADS, HG_DIM, HG_DIM), F32)] * 2,
        compiler_params=pltpu.CompilerParams(
            dimension_semantics=("parallel", "parallel", "arbitrary"), vmem_limit_bytes=VMEM_LIMIT),
        name="hgrn_scan",
    )(q, lf_f, k_f, v, q, lf_b, k_b, v)


ATT_QBLK = 2 * ATT_RADIUS
ATT_STEP_BLOCKS = 4


def _attn_body(q_ref, kp_ref, kc_ref, kn_ref, vp_ref, vc_ref, vn_ref, o_ref, lse_ref):
    n = pl.program_id(2)
    last = pl.num_programs(2) - 1
    kcat = jnp.concatenate([kp_ref[0], kc_ref[0], kn_ref[0]], axis=0)
    vcat = jnp.concatenate([vp_ref[0], vc_ref[0], vn_ref[0]], axis=0)
    nq, nk = ATT_QBLK, ATT_QBLK + 2 * ATT_RADIUS
    nblk = q_ref.shape[1] // nq
    i = lax.broadcasted_iota(jnp.int32, (nq, nk), 0)
    j = lax.broadcasted_iota(jnp.int32, (nq, nk), 1)
    band = (j >= i) & (j <= i + 2 * ATT_RADIUS)
    has_prev = (j >= ATT_RADIUS) | (n > 0)
    has_next = (j < nq + ATT_RADIUS) | (n < last)

    pair_w = 2 * ATT_HDIM
    npairs = ATT_HEADS // 2
    lane = lax.broadcasted_iota(jnp.int32, kcat.shape, 1) & (pair_w - 1)
    mask_a = jnp.where(lane < ATT_HDIM, 1.0, 0.0).astype(BF16)
    mask_b = jnp.where(lane < ATT_HDIM, 0.0, 1.0).astype(BF16)
    k_a, k_b, v_a, v_b = kcat * mask_a, kcat * mask_b, vcat * mask_a, vcat * mask_b
    lo_k = lax.broadcasted_iota(jnp.int32, (nk, pair_w), 1) < ATT_HDIM
    ones_a = jnp.where(lo_k, 1.0, 0.0).astype(BF16)
    ones_b = jnp.where(lo_k, 0.0, 1.0).astype(BF16)
    lo_q = lax.broadcasted_iota(jnp.int32, (nq, pair_w), 1) < ATT_HDIM

    for blk in range(nblk):
        valid = band
        if blk == 0:
            valid = valid & has_prev
        if blk == nblk - 1:
            valid = valid & has_next
        qrows = slice(blk * nq, (blk + 1) * nq)
        krows = slice(blk * nq, blk * nq + nk)
        scores = []
        for p in range(npairs):
            cols = slice(p * pair_w, (p + 1) * pair_w)
            q = q_ref[0, qrows, cols]
            scores.append(_dot_nt(q, k_a[krows, cols]))
            scores.append(_dot_nt(q, k_b[krows, cols]))
        scores = [jnp.where(valid, s, -jnp.inf) for s in scores]
        tops = [jnp.max(s, axis=-1, keepdims=True) for s in scores]
        probs = [jnp.exp2(s - m).astype(BF16) for s, m in zip(scores, tops)]
        for p in range(npairs):
            cols = slice(p * pair_w, (p + 1) * pair_w)
            res = (_dot(probs[2 * p], jnp.concatenate([v_a[krows, cols], ones_a], axis=1))
                   + _dot(probs[2 * p + 1], jnp.concatenate([v_b[krows, cols], ones_b], axis=1)))
            num, den = res[:, 0:pair_w], res[:, pair_w:2 * pair_w]
            o_ref[0, qrows, cols] = (num / den).astype(o_ref.dtype)
            lse_ref[0, qrows, cols] = jnp.where(lo_q, tops[2 * p], tops[2 * p + 1]) + jnp.log2(den)


def _attn_call(q, k, v, r):
    B, Lr, rW = q.shape
    W = rW // r
    step = ATT_QBLK * min(ATT_STEP_BLOCKS, Lr // ATT_QBLK)
    assert W == ATT_WIDTH and Lr % step == 0
    per_step = step // ATT_RADIUS
    nhalo = Lr // ATT_RADIUS
    center = pl.BlockSpec((1, step, W), lambda b, c, n: (b, n, c))
    prev = pl.BlockSpec((1, ATT_RADIUS, W), lambda b, c, n: (b, jnp.maximum(per_step * n - 1, 0), c))
    nxt = pl.BlockSpec((1, ATT_RADIUS, W), lambda b, c, n: (b, jnp.minimum(per_step * (n + 1), nhalo - 1), c))
    return pl.pallas_call(
        _attn_body,
        grid=(B, r, Lr // step),
        in_specs=[center, prev, center, nxt, prev, center, nxt],
        out_specs=[center, center],
        out_shape=[jax.ShapeDtypeStruct((B, Lr, rW), BF16), jax.ShapeDtypeStruct((B, Lr, rW), F32)],
        compiler_params=pltpu.CompilerParams(
            dimension_semantics=("parallel", "parallel", "parallel"), vmem_limit_bytes=VMEM_LIMIT),
        name=f"dilated_attn_r{r}",
    )(q, k, k, k, v, v, v)


FFN_CHUNK = 256


def _load_natural(a_ref, l_ref, perm_t_ref, r):
    if r == 1:
        return a_ref[0].astype(F32), l_ref[0]
    w = a_ref.shape[2] // r
    rows = PERM_ROWS // r
    a_nat, l_nat = [], []
    for sub in range(a_ref.shape[1] // rows):
        blk = slice(sub * rows, (sub + 1) * rows)
        a = jnp.concatenate([a_ref[0, blk, c * w:(c + 1) * w] for c in range(r)], axis=0)
        l = jnp.concatenate([l_ref[0, blk, c * w:(c + 1) * w] for c in range(r)], axis=0)
        l_hi = l.astype(BF16)
        l_lo = (l - l_hi.astype(F32)).astype(BF16)
        nat = _dot(perm_t_ref[...], jnp.concatenate([a, l_hi, l_lo], axis=1))
        a_nat.append(nat[:, 0:w])
        l_nat.append(nat[:, w:2 * w] + nat[:, 2 * w:3 * w])
    return jnp.concatenate(a_nat, axis=0), jnp.concatenate(l_nat, axis=0)


def _merge_body(x_ref, mod_ref, of_ref, ob_ref, sg_ref, a1_ref, l1_ref, a2_ref, l2_ref, a3_ref, l3_ref,
                ga_ref, gb_ref, gn_ref, n2_ref, fin_ref, wa_ref, wb_ref, wo_ref, wfi_ref, wfo_ref,
                p4t_ref, p16t_ref, y_ref, act_ref):
    mod = mod_ref[0]
    gate1, shift2, scale2, gate2 = mod[2:3], mod[3:4], mod[4:5], mod[5:6]

    o = of_ref[0].astype(F32) + ob_ref[0].astype(F32)
    o_a = jnp.concatenate([_rms(o[:, h * HG_DIM:(h + 1) * HG_DIM]) for h in range(HG_HEADS)], axis=1)
    o_a = o_a * gn_ref[...] * sg_ref[0]

    (a1, l1), (a2, l2), (a3, l3) = [
        _load_natural(a_ref, l_ref, perm_ref, r)
        for (_, r), a_ref, l_ref, perm_ref in zip(DILATION_PATTERNS, (a1_ref, a2_ref, a3_ref),
                                                  (l1_ref, l2_ref, l3_ref), (None, p4t_ref, p16t_ref))]
    top = jnp.maximum(jnp.maximum(l1, l2), l3)
    w1, w2, w3 = jnp.exp2(l1 - top), jnp.exp2(l2 - top), jnp.exp2(l3 - top)
    o_b = (a1 * w1 + a2 * w2 + a3 * w3) / (w1 + w2 + w3)

    merged = (ga_ref[0] * _dot(o_a.astype(BF16), wa_ref[...])
              + gb_ref[0] * _dot(o_b.astype(BF16), wb_ref[...]))
    x1 = x_ref[0] + gate1 * _dot(merged.astype(BF16), wo_ref[...])

    h2 = (_rms(x1) * n2_ref[...] * (1.0 + scale2) + shift2).astype(BF16)
    hidden = wfo_ref.shape[0]
    for c0 in range(0, hidden, FFN_CHUNK):
        gt = _dot(h2, wfi_ref[:, c0:c0 + FFN_CHUNK])
        up = _dot(h2, wfi_ref[:, hidden + c0:hidden + c0 + FFN_CHUNK])
        act_ref[:, c0:c0 + FFN_CHUNK] = (gt * _sigmoid(gt) * up).astype(BF16)
    x2 = x1 + gate2 * _dot(act_ref[...], wfo_ref[...])
    y_ref[0] = _rms(x2) * fin_ref[...]


def _merge_call(x, mod, o_f, o_b, sg, att, ga, gb, gn, n2, fin, wa, wb, wo, wfi, wfo, tm):
    B, L, D = x.shape
    hidden = wfo.shape[0]
    assert L % tm == 0 and tm % PERM_ROWS == 0 and hidden % FFN_CHUNK == 0 and wfi.shape[1] == 2 * hidden
    row = lambda w, r=1: pl.BlockSpec((1, tm // r, r * w), lambda b, i: (b, i, 0))
    att_flat = [t for pair in att for t in pair]
    att_specs = [row(ATT_WIDTH, r) for (_, r) in DILATION_PATTERNS for _ in range(2)]
    perms_t = [_dilation_perm(PERM_ROWS, r).T for (_, r) in DILATION_PATTERNS[1:]]
    return pl.pallas_call(
        _merge_body,
        grid=(B, L // tm),
        in_specs=[row(D), pl.BlockSpec((1,) + mod.shape[1:], lambda b, i: (b, 0, 0))]
                 + [row(512)] * 3 + att_specs + [row(D), row(D)]
                 + [_resident(t.shape) for t in (gn, n2, fin, wa, wb, wo, wfi, wfo, *perms_t)],
        out_specs=row(D),
        out_shape=jax.ShapeDtypeStruct((B, L, D), F32),
        scratch_shapes=[pltpu.VMEM((tm, hidden), BF16)],
        compiler_params=pltpu.CompilerParams(
            dimension_semantics=("parallel", "parallel"), vmem_limit_bytes=VMEM_LIMIT),
        name="merge_ffn",
    )(x, mod, o_f, o_b, sg, *att_flat, ga, gb, gn, n2, fin, wa, wb, wo, wfi, wfo, *perms_t)


def _rope_tables(L):
    half = ATT_HDIM // 2
    lane = np.arange(2 * ATT_HDIM)
    inv = ROPE_THETA ** (-(lane % half).astype(np.float64) / half)
    sign = np.where(lane % ATT_HDIM < half, -1.0, 1.0)
    ang = np.arange(L, dtype=np.float64)[:, None] * inv[None, :]
    return jnp.asarray(np.cos(ang), dtype=F32), jnp.asarray(np.sin(ang) * sign[None, :], dtype=F32)


def _trunk(x, mod, lb, p, tm_in, tm_out, t_scan):
    B, L, D = x.shape
    cos, sin_signed = _rope_tables(L)
    (q_h, lf_f, k_f, lf_b, k_b, v_h, sg, ga, gb, *qkv) = _inproj_call(
        x, mod, p["g1"], p["w_in"], p["b_in"], lb, cos, sin_signed, tm_in)
    o_f, o_b = _hgrn_call(q_h, lf_f, k_f, lf_b, k_b, v_h, t_scan)
    att = [_attn_call(*qkv[3 * i:3 * i + 3], r) for i, (_, r) in enumerate(DILATION_PATTERNS)]
    return _merge_call(x, mod, o_f, o_b, sg, att, ga, gb, p["gn"], p["n2"], p["fin"],
                       p["wa"], p["wb"], p["wo"], p["wfi"], p["wfo"], tm_out)


def kernel(x_prompt, x_sample, c_prompt, c_sample, w_ada, b_ada, norm1_g, w_in, b_in, lb_logits, hg_norm_g, w_branch_a, w_branch_b, w_out, norm2_g, w_ffn_in, w_ffn_out, final_norm_g):
    assert w_ada.shape[0] == 1 and lb_logits.shape[0] == 2, "single-layer trunk"
    D = x_prompt.shape[-1]
    bp, bs = c_prompt.shape[0], c_sample.shape[0]
    c_all = jnp.concatenate([c_prompt, c_sample], axis=0)
    c_pad = jnp.pad(c_all, ((0, -(bp + bs) % 8), (0, 0)))
    mod = _mod_call(c_pad, w_ada[0], b_ada[0]).reshape(c_pad.shape[0], 6, D)
    lb = _lb_call(lb_logits)
    row = lambda v: v.reshape(1, -1).astype(F32)
    p = dict(g1=row(norm1_g[0]), w_in=w_in[0].astype(BF16), b_in=row(b_in[0]), gn=row(hg_norm_g[0]),
             n2=row(norm2_g[0]), fin=row(final_norm_g), wa=w_branch_a[0].astype(BF16),
             wb=w_branch_b[0].astype(BF16), wo=w_out[0].astype(BF16), wfi=w_ffn_in[0].astype(BF16),
             wfo=w_ffn_out[0].astype(BF16))
    y_prompt = _trunk(x_prompt, mod[:bp], lb, p, 512, 512, 1024)
    y_sample = _trunk(x_sample, mod[bp:bp + bs], lb, p, 512, 512, 1024)
    return (y_prompt, y_sample)
```

```python
import numpy as np
import jax
import jax.numpy as jnp
from jax import lax
from jax.experimental import pallas as pl
from jax.experimental.pallas import tpu as pltpu

F32 = jnp.float32
BF16 = jnp.bfloat16

NORM_EPS = 1e-6
HG_HEADS = 4
HG_DIM = 128
HG_WIDTH = HG_HEADS * HG_DIM
ATT_HEADS = 8
ATT_HDIM = 64
ATT_WIDTH = ATT_HEADS * ATT_HDIM
DILATION_PATTERNS = ((128, 1), (512, 4), (2048, 16))
ATT_RADIUS = 64
ROPE_THETA = 10000.0
LOG2E = 1.4426950408889634
PROJ_BLOCK = 512
N_PROJ_BLOCKS = 12
PERM_ROWS = 256

HG_CHUNK = 64
HG_SUB = 16
HG_SUB_SHIFT = 4
HG_STEP_HEADS = 2
HG_EXP2_CLAMP = 110.0
EXP2_ZERO = -1e30
VMEM_LIMIT = 56 * 1024 * 1024


def _dot(a, b):
    return jnp.dot(a, b, preferred_element_type=F32)


def _dot_nt(a, b):
    return lax.dot_general(a, b, (((1,), (1,)), ((), ())), preferred_element_type=F32)


def _dot_tn(a, b):
    return lax.dot_general(a, b, (((0,), (0,)), ((), ())), preferred_element_type=F32)


def _sigmoid(x):
    return 1.0 / (1.0 + jnp.exp(-x))


def _rms(x):
    return x * lax.rsqrt(jnp.mean(x * x, axis=-1, keepdims=True) + NORM_EPS)


def _resident(shape):
    nd = len(shape)
    return pl.BlockSpec(shape, lambda *_: (0,) * nd, pipeline_mode=pl.Buffered(1))


def _mod_body(c_ref, w_ref, b_ref, o_ref):
    c = c_ref[...]
    a = c * _sigmoid(c)
    o_ref[...] = jnp.dot(a, w_ref[...], preferred_element_type=F32,
                         precision=lax.Precision.HIGHEST) + b_ref[...]


def _mod_call(c_pad, w_ada, b_ada):
    rows, d = c_pad.shape
    n = w_ada.shape[1]
    bn = 2048
    return pl.pallas_call(
        _mod_body,
        grid=(n // bn,),
        in_specs=[pl.BlockSpec((rows, d), lambda j: (0, 0)),
                  pl.BlockSpec((d, bn), lambda j: (0, j)),
                  pl.BlockSpec((1, bn), lambda j: (0, j))],
        out_specs=pl.BlockSpec((rows, bn), lambda j: (0, j)),
        out_shape=jax.ShapeDtypeStruct((rows, n), F32),
        name="adaln_mod",
    )(c_pad, w_ada, b_ada.reshape(1, n))


def _lb_body(l_ref, o_ref):
    l = l_ref[...]
    e = jnp.exp(l - jnp.max(l, axis=0, keepdims=True))
    o_ref[...] = e[0:1] / jnp.sum(e, axis=0, keepdims=True)


def _lb_call(lb_logits):
    n = lb_logits.shape[0]
    flat = lb_logits.reshape(n, -1).astype(F32)
    return pl.pallas_call(
        _lb_body,
        out_shape=jax.ShapeDtypeStruct((1, flat.shape[1]), F32),
        name="hgrn_lower_bounds",
    )(flat)


def _rope(p, cos, sin_signed):
    n = p.shape[-1]
    lane = lax.broadcasted_iota(jnp.int32, p.shape, 1)
    first_half = (lane & 63) < 32
    partner = jnp.where(first_half, pltpu.roll(p, n - 32, 1), pltpu.roll(p, 32, 1))
    reps = n // cos.shape[-1]
    return p * jnp.tile(cos, (1, reps)) + partner * jnp.tile(sin_signed, (1, reps))


def _store_dilated(t_bf, perm_ref, out_ref, r):
    if r == 1:
        out_ref[0] = t_bf
        return
    tm, w = t_bf.shape
    rows = PERM_ROWS // r
    for sub in range(tm // PERM_ROWS):
        grouped = _dot(perm_ref[...], t_bf[sub * PERM_ROWS:(sub + 1) * PERM_ROWS]).astype(BF16)
        for c in range(r):
            out_ref[0, sub * rows:(sub + 1) * rows, c * w:(c + 1) * w] = grouped[c * rows:(c + 1) * rows]


def _inproj_body(x_ref, mod_ref, g1_ref, w_ref, b_ref, lb_ref, cos_ref, sin_ref, p4_ref, p16_ref,
                 qh_ref, lff_ref, kf_ref, lfb_ref, kb_ref, vh_ref, sg_ref, ga_ref, gb_ref,
                 q1_ref, k1_ref, v1_ref, q4_ref, k4_ref, v4_ref, q16_ref, k16_ref, v16_ref):
    x = x_ref[0]
    mod = mod_ref[0]
    shift1, scale1 = mod[0:1], mod[1:2]
    h = _rms(x) * g1_ref[...]
    h = h * (1.0 + scale1) + shift1
    hb = h.astype(BF16)

    def proj(j):
        cols = slice(j * PROJ_BLOCK, (j + 1) * PROJ_BLOCK)
        return _dot(hb, w_ref[:, cols]) + b_ref[:, cols]

    p = proj(0)
    qh_ref[0] = p * _sigmoid(p) * (HG_DIM ** -0.5)

    def gates(p, lb, lf_ref, k_ref):
        f = lb + (1.0 - lb) * _sigmoid(p)
        lf_ref[0] = jnp.log2(f)
        k_ref[0] = 1.0 - f

    gates(proj(1), lb_ref[:, 0:HG_WIDTH], lff_ref, kf_ref)
    gates(proj(2), lb_ref[:, HG_WIDTH:2 * HG_WIDTH], lfb_ref, kb_ref)
    vh_ref[0] = proj(3).astype(BF16)
    p = proj(4)
    sg_ref[0] = (p * _sigmoid(p)).astype(sg_ref.dtype)

    cos, sin_signed = cos_ref[...], sin_ref[...]
    q_a = (_rope(proj(5), cos, sin_signed) * (ATT_HDIM ** -0.5 * LOG2E)).astype(BF16)
    k_a = _rope(proj(6), cos, sin_signed).astype(BF16)
    v_a = proj(7).astype(BF16)
    for t, refs in ((q_a, (q1_ref, q4_ref, q16_ref)), (k_a, (k1_ref, k4_ref, k16_ref)),
                    (v_a, (v1_ref, v4_ref, v16_ref))):
        for (_, r), perm_ref, out_ref in zip(DILATION_PATTERNS, (None, p4_ref, p16_ref), refs):
            _store_dilated(t, perm_ref, out_ref, r)
    ga_ref[0, :, 0:PROJ_BLOCK] = _sigmoid(proj(8)).astype(ga_ref.dtype)
    ga_ref[0, :, PROJ_BLOCK:2 * PROJ_BLOCK] = _sigmoid(proj(9)).astype(ga_ref.dtype)
    gb_ref[0, :, 0:PROJ_BLOCK] = _sigmoid(proj(10)).astype(gb_ref.dtype)
    gb_ref[0, :, PROJ_BLOCK:2 * PROJ_BLOCK] = _sigmoid(proj(11)).astype(gb_ref.dtype)


def _dilation_perm(tm, r):
    i = np.arange(tm)
    src = r * (i % (tm // r)) + i // (tm // r)
    return jnp.asarray(src[:, None] == np.arange(tm)[None, :], dtype=BF16)


def _inproj_call(x, mod, g1, w_in, b_in, lb, cos, sin_signed, tm):
    B, L, D = x.shape
    n_in = w_in.shape[1]
    assert n_in == N_PROJ_BLOCKS * PROJ_BLOCK and L % tm == 0 and tm % PERM_ROWS == 0
    row = lambda w, r=1: pl.BlockSpec((1, tm // r, r * w), lambda b, i: (b, i, 0))
    tab = pl.BlockSpec((tm, cos.shape[1]), lambda b, i: (i, 0))
    shp = lambda w, dt, r=1: jax.ShapeDtypeStruct((B, L // r, r * w), dt)
    perms = [_dilation_perm(PERM_ROWS, r) for (_, r) in DILATION_PATTERNS[1:]]
    att_specs = [row(ATT_WIDTH, r) for (_, r) in DILATION_PATTERNS for _ in range(3)]
    att_shapes = [shp(ATT_WIDTH, BF16, r) for (_, r) in DILATION_PATTERNS for _ in range(3)]
    return pl.pallas_call(
        _inproj_body,
        grid=(B, L // tm),
        in_specs=[row(D),
                  pl.BlockSpec((1,) + mod.shape[1:], lambda b, i: (b, 0, 0)),
                  _resident(g1.shape), _resident(w_in.shape), _resident(b_in.shape), _resident(lb.shape),
                  tab, tab] + [_resident(p.shape) for p in perms],
        out_specs=[row(512)] * 7 + [row(1024)] * 2 + att_specs,
        out_shape=[shp(512, F32), shp(512, F32), shp(512, F32), shp(512, F32), shp(512, F32),
                   shp(512, BF16), shp(512, BF16), shp(1024, BF16), shp(1024, BF16)] + att_shapes,
        compiler_params=pltpu.CompilerParams(
            dimension_semantics=("parallel", "parallel"), vmem_limit_bytes=VMEM_LIMIT),
        name="inproj",
    )(x, mod, g1, w_in, b_in, lb, cos, sin_signed, *perms)


def _bcast_row(c, group, r):
    T, w = c.shape
    c3 = c.reshape(T // group, group, w)
    return jnp.broadcast_to(c3[:, r:r + 1, :], c3.shape).reshape(T, w)


def _split3(g):
    g1 = g.astype(BF16)
    r1 = g - g1.astype(F32)
    g2 = r1.astype(BF16)
    return g1, g2, (r1 - g2.astype(F32)).astype(BF16)


def _hgrn_dir(q, k, v_bf, g, s_in, rev, o_ref, cols):
    T, dk = q.shape
    C, nch, nsb = HG_CHUNK, T // HG_CHUNK, HG_CHUNK // HG_SUB
    t = lax.broadcasted_iota(jnp.int32, (C, C), 0)
    s = lax.broadcasted_iota(jnp.int32, (C, C), 1)
    order = (s >= t) if rev else (s <= t)
    inner_mask = order & ((t >> HG_SUB_SHIFT) == (s >> HG_SUB_SHIFT))
    tri = order.astype(BF16)

    g_terms = jnp.concatenate(_split3(g), axis=1)
    c_parts = []
    for ci in range(nch):
        r = _dot(tri, g_terms[ci * C:(ci + 1) * C])
        c_parts.append(r[:, 0:dk] + r[:, dk:2 * dk] + r[:, 2 * dk:3 * dk])
    c = jnp.concatenate(c_parts, axis=0)
    yield

    far = 0 if rev else HG_SUB - 1
    edge = _bcast_row(c, HG_SUB, far)
    mid = _bcast_row(c, HG_SUB, HG_SUB // 2)
    c_end = _bcast_row(c, C, 0 if rev else C - 1)
    rowc = lax.broadcasted_iota(jnp.int32, (T, dk), 0) & (C - 1)

    k_edge = k * jnp.exp2(edge - c)
    q_cat, k_cat = [], []
    for J in (range(1, nsb) if rev else range(nsb - 1)):
        ref_j = _bcast_row(c, C, J * HG_SUB + far)
        queries = (rowc < J * HG_SUB) if rev else (rowc >= (J + 1) * HG_SUB)
        q_cat.append((q * jnp.exp2(jnp.where(queries, c - ref_j, EXP2_ZERO))).astype(BF16))
        in_j = (rowc >= J * HG_SUB) & (rowc < (J + 1) * HG_SUB)
        k_cat.append(jnp.where(in_j, k_edge, 0.0).astype(BF16))
    q_cat = jnp.concatenate(q_cat, axis=1)
    k_cat = jnp.concatenate(k_cat, axis=1)

    d_mid = jnp.clip(c - mid, -HG_EXP2_CLAMP, HG_EXP2_CLAMP)
    q_mid = (q * jnp.exp2(d_mid)).astype(BF16)
    k_mid = (k * jnp.exp2(-d_mid)).astype(BF16)
    q_dec = (q * jnp.exp2(c)).astype(BF16)
    k_end = (k * jnp.exp2(c_end - c)).astype(BF16)
    decay = jnp.exp2(c_end)
    yield

    o_intra, upd = [], []
    for ci in range(nch):
        rows = slice(ci * C, (ci + 1) * C)
        sc = _dot_nt(q_cat[rows], k_cat[rows])
        sc = sc + jnp.where(inner_mask, _dot_nt(q_mid[rows], k_mid[rows]), 0.0)
        o_intra.append(_dot(sc.astype(BF16), v_bf[rows]))
        upd.append(_dot_tn(v_bf[rows], k_end[rows]))
        yield

    s_t = s_in
    for ci in (reversed(range(nch)) if rev else range(nch)):
        rows = slice(ci * C, (ci + 1) * C)
        o_ref[0, rows, cols] = (o_intra[ci] + _dot_nt(q_dec[rows], s_t.astype(BF16))).astype(o_ref.dtype)
        s_t = s_t * decay[ci * C:ci * C + 1] + upd[ci]
        yield
    return s_t


def _hgrn_body(qf_ref, lff_ref, kf_ref, vf_ref, qb_ref, lfb_ref, kb_ref, vb_ref,
               of_ref, ob_ref, sf_ref, sb_ref):
    @pl.when(pl.program_id(2) == 0)
    def _():
        sf_ref[...] = jnp.zeros_like(sf_ref)
        sb_ref[...] = jnp.zeros_like(sb_ref)

    runs = []
    for h in range(HG_STEP_HEADS):
        cols = slice(h * HG_DIM, (h + 1) * HG_DIM)
        runs.append((_hgrn_dir(qf_ref[0, :, cols], kf_ref[0, :, cols], vf_ref[0, :, cols], lff_ref[0, :, cols],
                               sf_ref[h], False, of_ref, cols), sf_ref, h))
        runs.append((_hgrn_dir(qb_ref[0, :, cols], kb_ref[0, :, cols], vb_ref[0, :, cols], lfb_ref[0, :, cols],
                               sb_ref[h], True, ob_ref, cols), sb_ref, h))
    while runs:
        for run in list(runs):
            gen, state_ref, h = run
            try:
                next(gen)
            except StopIteration as done:
                state_ref[h] = done.value
                runs.remove(run)


def _hgrn_call(q, lf_f, k_f, lf_b, k_b, v, T):
    B, L, W = q.shape
    nb = L // T
    assert L % T == 0 and T % HG_CHUNK == 0 and W == HG_WIDTH
    width = HG_STEP_HEADS * HG_DIM
    fwd = pl.BlockSpec((1, T, width), lambda b, h, j: (b, j, h))
    bwd = pl.BlockSpec((1, T, width), lambda b, h, j: (b, nb - 1 - j, h))
    return pl.pallas_call(
        _hgrn_body,
        grid=(B, HG_HEADS // HG_STEP_HEADS, nb),
        in_specs=[fwd, fwd, fwd, fwd, bwd, bwd, bwd, bwd],
        out_specs=[fwd, bwd],
        out_shape=[jax.ShapeDtypeStruct((B, L, W), BF16)] * 2,
        scratch_shapes=[pltpu.VMEM((HG_STEP_HEADS, HG_DIM, HG_DIM), F32)] * 2,
        compiler_params=pltpu.CompilerParams(
            dimension_semantics=("parallel", "parallel", "arbitrary"), vmem_limit_bytes=VMEM_LIMIT),
        name="hgrn_scan",
    )(q, lf_f, k_f, v, q, lf_b, k_b, v)


ATT_QBLK = 2 * ATT_RADIUS
ATT_STEP_BLOCKS = 8


def _attn_body(q_ref, kp_ref, kc_ref, kn_ref, vp_ref, vc_ref, vn_ref, o_ref, lse_ref):
    n = pl.program_id(2)
    last = pl.num_programs(2) - 1
    kcat = jnp.concatenate([kp_ref[0], kc_ref[0], kn_ref[0]], axis=0)
    vcat = jnp.concatenate([vp_ref[0], vc_ref[0], vn_ref[0]], axis=0)
    nq, nk = ATT_QBLK, ATT_QBLK + 2 * ATT_RADIUS
    nblk = q_ref.shape[1] // nq
    i = lax.broadcasted_iota(jnp.int32, (nq, nk), 0)
    j = lax.broadcasted_iota(jnp.int32, (nq, nk), 1)
    band = (j >= i) & (j <= i + 2 * ATT_RADIUS)
    has_prev = (j >= ATT_RADIUS) | (n > 0)
    has_next = (j < nq + ATT_RADIUS) | (n < last)

    pair_w = 2 * ATT_HDIM
    npairs = ATT_HEADS // 2
    lane = lax.broadcasted_iota(jnp.int32, kcat.shape, 1) & (pair_w - 1)
    mask_a = jnp.where(lane < ATT_HDIM, 1.0, 0.0).astype(BF16)
    mask_b = jnp.where(lane < ATT_HDIM, 0.0, 1.0).astype(BF16)
    k_a, k_b, v_a, v_b = kcat * mask_a, kcat * mask_b, vcat * mask_a, vcat * mask_b
    lo_k = lax.broadcasted_iota(jnp.int32, (nk, pair_w), 1) < ATT_HDIM
    ones_a = jnp.where(lo_k, 1.0, 0.0).astype(BF16)
    ones_b = jnp.where(lo_k, 0.0, 1.0).astype(BF16)
    lo_q = lax.broadcasted_iota(jnp.int32, (nq, pair_w), 1) < ATT_HDIM

    for blk in range(nblk):
        valid = band
        if blk == 0:
            valid = valid & has_prev
        if blk == nblk - 1:
            valid = valid & has_next
        qrows = slice(blk * nq, (blk + 1) * nq)
        krows = slice(blk * nq, blk * nq + nk)
        scores = []
        for p in range(npairs):
            cols = slice(p * pair_w, (p + 1) * pair_w)
            q = q_ref[0, qrows, cols]
            scores.append(_dot_nt(q, k_a[krows, cols]))
            scores.append(_dot_nt(q, k_b[krows, cols]))
        scores = [jnp.where(valid, s, -jnp.inf) for s in scores]
        tops = [jnp.max(s, axis=-1, keepdims=True) for s in scores]
        probs = [jnp.exp2(s - m).astype(BF16) for s, m in zip(scores, tops)]
        for p in range(npairs):
            cols = slice(p * pair_w, (p + 1) * pair_w)
            res = (_dot(probs[2 * p], jnp.concatenate([v_a[krows, cols], ones_a], axis=1))
                   + _dot(probs[2 * p + 1], jnp.concatenate([v_b[krows, cols], ones_b], axis=1)))
            num, den = res[:, 0:pair_w], res[:, pair_w:2 * pair_w]
            o_ref[0, qrows, cols] = (num / den).astype(o_ref.dtype)
            lse_ref[0, qrows, cols] = jnp.where(lo_q, tops[2 * p], tops[2 * p + 1]) + jnp.log2(den)


def _attn_call(q, k, v, r):
    B, Lr, rW = q.shape
    W = rW // r
    step = ATT_QBLK * min(ATT_STEP_BLOCKS, Lr // ATT_QBLK)
    assert W == ATT_WIDTH and Lr % step == 0
    per_step = step // ATT_RADIUS
    nhalo = Lr // ATT_RADIUS
    center = pl.BlockSpec((1, step, W), lambda b, c, n: (b, n, c))
    prev = pl.BlockSpec((1, ATT_RADIUS, W), lambda b, c, n: (b, jnp.maximum(per_step * n - 1, 0), c))
    nxt = pl.BlockSpec((1, ATT_RADIUS, W), lambda b, c, n: (b, jnp.minimum(per_step * (n + 1), nhalo - 1), c))
    return pl.pallas_call(
        _attn_body,
        grid=(B, r, Lr // step),
        in_specs=[center, prev, center, nxt, prev, center, nxt],
        out_specs=[center, center],
        out_shape=[jax.ShapeDtypeStruct((B, Lr, rW), BF16), jax.ShapeDtypeStruct((B, Lr, rW), F32)],
        compiler_params=pltpu.CompilerParams(
            dimension_semantics=("parallel", "parallel", "parallel"), vmem_limit_bytes=VMEM_LIMIT),
        name=f"dilated_attn_r{r}",
    )(q, k, k, k, v, v, v)


FFN_CHUNK = 256


def _load_natural(a_ref, l_ref, perm_t_ref, r):
    if r == 1:
        return a_ref[0].astype(F32), l_ref[0]
    w = a_ref.shape[2] // r
    rows = PERM_ROWS // r
    a_nat, l_nat = [], []
    for sub in range(a_ref.shape[1] // rows):
        blk = slice(sub * rows, (sub + 1) * rows)
        a = jnp.concatenate([a_ref[0, blk, c * w:(c + 1) * w] for c in range(r)], axis=0)
        l = jnp.concatenate([l_ref[0, blk, c * w:(c + 1) * w] for c in range(r)], axis=0)
        l_hi = l.astype(BF16)
        l_lo = (l - l_hi.astype(F32)).astype(BF16)
        nat = _dot(perm_t_ref[...], jnp.concatenate([a, l_hi, l_lo], axis=1))
        a_nat.append(nat[:, 0:w])
        l_nat.append(nat[:, w:2 * w] + nat[:, 2 * w:3 * w])
    return jnp.concatenate(a_nat, axis=0), jnp.concatenate(l_nat, axis=0)


def _merge_body(x_ref, mod_ref, of_ref, ob_ref, sg_ref, a1_ref, l1_ref, a2_ref, l2_ref, a3_ref, l3_ref,
                ga_ref, gb_ref, gn_ref, n2_ref, fin_ref, wa_ref, wb_ref, wo_ref, wfi_ref, wfo_ref,
                p4t_ref, p16t_ref, y_ref, act_ref):
    mod = mod_ref[0]
    gate1, shift2, scale2, gate2 = mod[2:3], mod[3:4], mod[4:5], mod[5:6]

    o = of_ref[0].astype(F32) + ob_ref[0].astype(F32)
    o_a = jnp.concatenate([_rms(o[:, h * HG_DIM:(h + 1) * HG_DIM]) for h in range(HG_HEADS)], axis=1)
    o_a = o_a * gn_ref[...] * sg_ref[0]

    (a1, l1), (a2, l2), (a3, l3) = [
        _load_natural(a_ref, l_ref, perm_ref, r)
        for (_, r), a_ref, l_ref, perm_ref in zip(DILATION_PATTERNS, (a1_ref, a2_ref, a3_ref),
                                                  (l1_ref, l2_ref, l3_ref), (None, p4t_ref, p16t_ref))]
    top = jnp.maximum(jnp.maximum(l1, l2), l3)
    w1, w2, w3 = jnp.exp2(l1 - top), jnp.exp2(l2 - top), jnp.exp2(l3 - top)
    o_b = (a1 * w1 + a2 * w2 + a3 * w3) / (w1 + w2 + w3)

    merged = (ga_ref[0] * _dot(o_a.astype(BF16), wa_ref[...])
              + gb_ref[0] * _dot(o_b.astype(BF16), wb_ref[...]))
    x1 = x_ref[0] + gate1 * _dot(merged.astype(BF16), wo_ref[...])

    h2 = (_rms(x1) * n2_ref[...] * (1.0 + scale2) + shift2).astype(BF16)
    hidden = wfo_ref.shape[0]
    for c0 in range(0, hidden, FFN_CHUNK):
        gt = _dot(h2, wfi_ref[:, c0:c0 + FFN_CHUNK])
        up = _dot(h2, wfi_ref[:, hidden + c0:hidden + c0 + FFN_CHUNK])
        act_ref[:, c0:c0 + FFN_CHUNK] = (gt * _sigmoid(gt) * up).astype(BF16)
    x2 = x1 + gate2 * _dot(act_ref[...], wfo_ref[...])
    y_ref[0] = _rms(x2) * fin_ref[...]


def _merge_call(x, mod, o_f, o_b, sg, att, ga, gb, gn, n2, fin, wa, wb, wo, wfi, wfo, tm):
    B, L, D = x.shape
    hidden = wfo.shape[0]
    assert L % tm == 0 and tm % PERM_ROWS == 0 and hidden % FFN_CHUNK == 0 and wfi.shape[1] == 2 * hidden
    row = lambda w, r=1: pl.BlockSpec((1, tm // r, r * w), lambda b, i: (b, i, 0))
    att_flat = [t for pair in att for t in pair]
    att_specs = [row(ATT_WIDTH, r) for (_, r) in DILATION_PATTERNS for _ in range(2)]
    perms_t = [_dilation_perm(PERM_ROWS, r).T for (_, r) in DILATION_PATTERNS[1:]]
    return pl.pallas_call(
        _merge_body,
        grid=(B, L // tm),
        in_specs=[row(D), pl.BlockSpec((1,) + mod.shape[1:], lambda b, i: (b, 0, 0))]
                 + [row(512)] * 3 + att_specs + [row(D), row(D)]
                 + [_resident(t.shape) for t in (gn, n2, fin, wa, wb, wo, wfi, wfo, *perms_t)],
        out_specs=row(D),
        out_shape=jax.ShapeDtypeStruct((B, L, D), F32),
        scratch_shapes=[pltpu.VMEM((tm, hidden), BF16)],
        compiler_params=pltpu.CompilerParams(
            dimension_semantics=("parallel", "parallel"), vmem_limit_bytes=VMEM_LIMIT),
        name="merge_ffn",
    )(x, mod, o_f, o_b, sg, *att_flat, ga, gb, gn, n2, fin, wa, wb, wo, wfi, wfo, *perms_t)


def _rope_tables(L):
    half = ATT_HDIM // 2
    lane = np.arange(2 * ATT_HDIM)
    inv = ROPE_THETA ** (-(lane % half).astype(np.float64) / half)
    sign = np.where(lane % ATT_HDIM < half, -1.0, 1.0)
    ang = np.arange(L, dtype=np.float64)[:, None] * inv[None, :]
    return jnp.asarray(np.cos(ang), dtype=F32), jnp.asarray(np.sin(ang) * sign[None, :], dtype=F32)


def _trunk(x, mod, lb, p, tm_in, tm_out, t_scan):
    B, L, D = x.shape
    cos, sin_signed = _rope_tables(L)
    (q_h, lf_f, k_f, lf_b, k_b, v_h, sg, ga, gb, *qkv) = _inproj_call(
        x, mod, p["g1"], p["w_in"], p["b_in"], lb, cos, sin_signed, tm_in)
    o_f, o_b = _hgrn_call(q_h, lf_f, k_f, lf_b, k_b, v_h, t_scan)
    att = [_attn_call(*qkv[3 * i:3 * i + 3], r) for i, (_, r) in enumerate(DILATION_PATTERNS)]
    return _merge_call(x, mod, o_f, o_b, sg, att, ga, gb, p["gn"], p["n2"], p["fin"],
                       p["wa"], p["wb"], p["wo"], p["wfi"], p["wfo"], tm_out)


def kernel(x_prompt, x_sample, c_prompt, c_sample, w_ada, b_ada, norm1_g, w_in, b_in, lb_logits, hg_norm_g, w_branch_a, w_branch_b, w_out, norm2_g, w_ffn_in, w_ffn_out, final_norm_g):
    assert w_ada.shape[0] == 1 and lb_logits.shape[0] == 2, "single-layer trunk"
    D = x_prompt.shape[-1]
    bp, bs = c_prompt.shape[0], c_sample.shape[0]
    c_all = jnp.concatenate([c_prompt, c_sample], axis=0)
    c_pad = jnp.pad(c_all, ((0, -(bp + bs) % 8), (0, 0)))
    mod = _mod_call(c_pad, w_ada[0], b_ada[0]).reshape(c_pad.shape[0], 6, D)
    lb = _lb_call(lb_logits)
    row = lambda v: v.reshape(1, -1).astype(F32)
    p = dict(g1=row(norm1_g[0]), w_in=w_in[0].astype(BF16), b_in=row(b_in[0]), gn=row(hg_norm_g[0]),
             n2=row(norm2_g[0]), fin=row(final_norm_g), wa=w_branch_a[0].astype(BF16),
             wb=w_branch_b[0].astype(BF16), wo=w_out[0].astype(BF16), wfi=w_ffn_in[0].astype(BF16),
             wfo=w_ffn_out[0].astype(BF16))
    y_prompt = _trunk(x_prompt, mod[:bp], lb, p, 512, 512, 2048)
    y_sample = _trunk(x_sample, mod[bp:bp + bs], lb, p, 512, 512, 2048)
    return (y_prompt, y_sample)
```

```python
import numpy as np
import jax
import jax.numpy as jnp
from jax import lax
from jax.experimental import pallas as pl
from jax.experimental.pallas import tpu as pltpu

F32 = jnp.float32
BF16 = jnp.bfloat16

NORM_EPS = 1e-6
HG_HEADS = 4
HG_DIM = 128
HG_WIDTH = HG_HEADS * HG_DIM
ATT_HEADS = 8
ATT_HDIM = 64
ATT_WIDTH = ATT_HEADS * ATT_HDIM
DILATION_PATTERNS = ((128, 1), (512, 4), (2048, 16))
ATT_RADIUS = 64
ROPE_THETA = 10000.0
LOG2E = 1.4426950408889634
PROJ_BLOCK = 512
N_PROJ_BLOCKS = 12
PERM_ROWS = 256

HG_CHUNK = 64
HG_SUB = 16
HG_SUB_SHIFT = 4
HG_STEP_HEADS = 2
HG_EXP2_CLAMP = 110.0
EXP2_ZERO = -1e30
VMEM_LIMIT = 56 * 1024 * 1024


def _dot(a, b):
    return jnp.dot(a, b, preferred_element_type=F32)


def _dot_nt(a, b):
    return lax.dot_general(a, b, (((1,), (1,)), ((), ())), preferred_element_type=F32)


def _dot_tn(a, b):
    return lax.dot_general(a, b, (((0,), (0,)), ((), ())), preferred_element_type=F32)


def _sigmoid(x):
    return 1.0 / (1.0 + jnp.exp(-x))


def _rms(x):
    return x * lax.rsqrt(jnp.mean(x * x, axis=-1, keepdims=True) + NORM_EPS)


def _resident(shape):
    nd = len(shape)
    return pl.BlockSpec(shape, lambda *_: (0,) * nd, pipeline_mode=pl.Buffered(1))


def _mod_body(c_ref, w_ref, b_ref, o_ref):
    c = c_ref[...]
    a = c * _sigmoid(c)
    o_ref[...] = jnp.dot(a, w_ref[...], preferred_element_type=F32,
                         precision=lax.Precision.HIGHEST) + b_ref[...]


def _mod_call(c_pad, w_ada, b_ada):
    rows, d = c_pad.shape
    n = w_ada.shape[1]
    bn = 2048
    return pl.pallas_call(
        _mod_body,
        grid=(n // bn,),
        in_specs=[pl.BlockSpec((rows, d), lambda j: (0, 0)),
                  pl.BlockSpec((d, bn), lambda j: (0, j)),
                  pl.BlockSpec((1, bn), lambda j: (0, j))],
        out_specs=pl.BlockSpec((rows, bn), lambda j: (0, j)),
        out_shape=jax.ShapeDtypeStruct((rows, n), F32),
        name="adaln_mod",
    )(c_pad, w_ada, b_ada.reshape(1, n))


def _lb_body(l_ref, o_ref):
    l = l_ref[...]
    e = jnp.exp(l - jnp.max(l, axis=0, keepdims=True))
    o_ref[...] = e[0:1] / jnp.sum(e, axis=0, keepdims=True)


def _lb_call(lb_logits):
    n = lb_logits.shape[0]
    flat = lb_logits.reshape(n, -1).astype(F32)
    return pl.pallas_call(
        _lb_body,
        out_shape=jax.ShapeDtypeStruct((1, flat.shape[1]), F32),
        name="hgrn_lower_bounds",
    )(flat)


def _rope(p, cos, sin_signed):
    n = p.shape[-1]
    lane = lax.broadcasted_iota(jnp.int32, p.shape, 1)
    first_half = (lane & 63) < 32
    partner = jnp.where(first_half, pltpu.roll(p, n - 32, 1), pltpu.roll(p, 32, 1))
    reps = n // cos.shape[-1]
    return p * jnp.tile(cos, (1, reps)) + partner * jnp.tile(sin_signed, (1, reps))


def _store_dilated(t_bf, perm_ref, out_ref, r):
    if r == 1:
        out_ref[0] = t_bf
        return
    tm, w = t_bf.shape
    rows = PERM_ROWS // r
    for sub in range(tm // PERM_ROWS):
        grouped = _dot(perm_ref[...], t_bf[sub * PERM_ROWS:(sub + 1) * PERM_ROWS]).astype(BF16)
        for c in range(r):
            out_ref[0, sub * rows:(sub + 1) * rows, c * w:(c + 1) * w] = grouped[c * rows:(c + 1) * rows]


def _inproj_body(x_ref, mod_ref, g1_ref, w_ref, b_ref, lb_ref, cos_ref, sin_ref, p4_ref, p16_ref,
                 qh_ref, lff_ref, kf_ref, lfb_ref, kb_ref, vh_ref, sg_ref, ga_ref, gb_ref,
                 q1_ref, k1_ref, v1_ref, q4_ref, k4_ref, v4_ref, q16_ref, k16_ref, v16_ref):
    x = x_ref[0]
    mod = mod_ref[0]
    shift1, scale1 = mod[0:1], mod[1:2]
    h = _rms(x) * g1_ref[...]
    h = h * (1.0 + scale1) + shift1
    hb = h.astype(BF16)

    def proj(j):
        cols = slice(j * PROJ_BLOCK, (j + 1) * PROJ_BLOCK)
        return _dot(hb, w_ref[:, cols]) + b_ref[:, cols]

    p = proj(0)
    qh_ref[0] = p * _sigmoid(p) * (HG_DIM ** -0.5)

    def gates(p, lb, lf_ref, k_ref):
        f = lb + (1.0 - lb) * _sigmoid(p)
        lf_ref[0] = jnp.log2(f)
        k_ref[0] = 1.0 - f

    gates(proj(1), lb_ref[:, 0:HG_WIDTH], lff_ref, kf_ref)
    gates(proj(2), lb_ref[:, HG_WIDTH:2 * HG_WIDTH], lfb_ref, kb_ref)
    vh_ref[0] = proj(3).astype(BF16)
    p = proj(4)
    sg_ref[0] = (p * _sigmoid(p)).astype(sg_ref.dtype)

    cos, sin_signed = cos_ref[...], sin_ref[...]
    q_a = (_rope(proj(5), cos, sin_signed) * (ATT_HDIM ** -0.5 * LOG2E)).astype(BF16)
    k_a = _rope(proj(6), cos, sin_signed).astype(BF16)
    v_a = proj(7).astype(BF16)
    for t, refs in ((q_a, (q1_ref, q4_ref, q16_ref)), (k_a, (k1_ref, k4_ref, k16_ref)),
                    (v_a, (v1_ref, v4_ref, v16_ref))):
        for (_, r), perm_ref, out_ref in zip(DILATION_PATTERNS, (None, p4_ref, p16_ref), refs):
            _store_dilated(t, perm_ref, out_ref, r)
    ga_ref[0, :, 0:PROJ_BLOCK] = _sigmoid(proj(8)).astype(ga_ref.dtype)
    ga_ref[0, :, PROJ_BLOCK:2 * PROJ_BLOCK] = _sigmoid(proj(9)).astype(ga_ref.dtype)
    gb_ref[0, :, 0:PROJ_BLOCK] = _sigmoid(proj(10)).astype(gb_ref.dtype)
    gb_ref[0, :, PROJ_BLOCK:2 * PROJ_BLOCK] = _sigmoid(proj(11)).astype(gb_ref.dtype)


def _dilation_perm(tm, r):
    i = np.arange(tm)
    src = r * (i % (tm // r)) + i // (tm // r)
    return jnp.asarray(src[:, None] == np.arange(tm)[None, :], dtype=BF16)


def _inproj_call(x, mod, g1, w_in, b_in, lb, cos, sin_signed, tm):
    B, L, D = x.shape
    n_in = w_in.shape[1]
    assert n_in == N_PROJ_BLOCKS * PROJ_BLOCK and L % tm == 0 and tm % PERM_ROWS == 0
    row = lambda w, r=1: pl.BlockSpec((1, tm // r, r * w), lambda b, i: (b, i, 0))
    tab = pl.BlockSpec((tm, cos.shape[1]), lambda b, i: (i, 0))
    shp = lambda w, dt, r=1: jax.ShapeDtypeStruct((B, L // r, r * w), dt)
    perms = [_dilation_perm(PERM_ROWS, r) for (_, r) in DILATION_PATTERNS[1:]]
    att_specs = [row(ATT_WIDTH, r) for (_, r) in DILATION_PATTERNS for _ in range(3)]
    att_shapes = [shp(ATT_WIDTH, BF16, r) for (_, r) in DILATION_PATTERNS for _ in range(3)]
    return pl.pallas_call(
        _inproj_body,
        grid=(B, L // tm),
        in_specs=[row(D),
                  pl.BlockSpec((1,) + mod.shape[1:], lambda b, i: (b, 0, 0)),
                  _resident(g1.shape), _resident(w_in.shape), _resident(b_in.shape), _resident(lb.shape),
                  tab, tab] + [_resident(p.shape) for p in perms],
        out_specs=[row(512)] * 7 + [row(1024)] * 2 + att_specs,
        out_shape=[shp(512, F32), shp(512, F32), shp(512, F32), shp(512, F32), shp(512, F32),
                   shp(512, BF16), shp(512, BF16), shp(1024, BF16), shp(1024, BF16)] + att_shapes,
        compiler_params=pltpu.CompilerParams(
            dimension_semantics=("parallel", "parallel"), vmem_limit_bytes=VMEM_LIMIT),
        name="inproj",
    )(x, mod, g1, w_in, b_in, lb, cos, sin_signed, *perms)


def _bcast_row(c, group, r):
    T, w = c.shape
    c3 = c.reshape(T // group, group, w)
    return jnp.broadcast_to(c3[:, r:r + 1, :], c3.shape).reshape(T, w)


def _split2(g):
    g1 = g.astype(BF16)
    return g1, (g - g1.astype(F32)).astype(BF16)


def _hgrn_dir(q, k, v_bf, g, s_in, rev, o_ref, cols):
    T, dk = q.shape
    C, nch, nsb = HG_CHUNK, T // HG_CHUNK, HG_CHUNK // HG_SUB
    t = lax.broadcasted_iota(jnp.int32, (C, C), 0)
    s = lax.broadcasted_iota(jnp.int32, (C, C), 1)
    order = (s >= t) if rev else (s <= t)
    inner_mask = order & ((t >> HG_SUB_SHIFT) == (s >> HG_SUB_SHIFT))
    tri = order.astype(BF16)

    g_terms = jnp.concatenate(_split2(g), axis=1)
    c_parts = []
    for ci in range(nch):
        r = _dot(tri, g_terms[ci * C:(ci + 1) * C])
        c_parts.append(r[:, 0:dk] + r[:, dk:2 * dk])
    c = jnp.concatenate(c_parts, axis=0)
    yield

    far = 0 if rev else HG_SUB - 1
    edge = _bcast_row(c, HG_SUB, far)
    mid = _bcast_row(c, HG_SUB, HG_SUB // 2)
    c_end = _bcast_row(c, C, 0 if rev else C - 1)
    rowc = lax.broadcasted_iota(jnp.int32, (T, dk), 0) & (C - 1)

    k_edge = k * jnp.exp2(edge - c)
    q_cat, k_cat = [], []
    for J in (range(1, nsb) if rev else range(nsb - 1)):
        ref_j = _bcast_row(c, C, J * HG_SUB + far)
        queries = (rowc < J * HG_SUB) if rev else (rowc >= (J + 1) * HG_SUB)
        q_cat.append((q * jnp.exp2(jnp.where(queries, c - ref_j, EXP2_ZERO))).astype(BF16))
        in_j = (rowc >= J * HG_SUB) & (rowc < (J + 1) * HG_SUB)
        k_cat.append(jnp.where(in_j, k_edge, 0.0).astype(BF16))
    q_cat = jnp.concatenate(q_cat, axis=1)
    k_cat = jnp.concatenate(k_cat, axis=1)

    d_mid = jnp.clip(c - mid, -HG_EXP2_CLAMP, HG_EXP2_CLAMP)
    q_mid = (q * jnp.exp2(d_mid)).astype(BF16)
    k_mid = (k * jnp.exp2(-d_mid)).astype(BF16)
    q_dec = (q * jnp.exp2(c)).astype(BF16)
    k_end = (k * jnp.exp2(c_end - c)).astype(BF16)
    decay = jnp.exp2(c_end)
    yield

    o_intra, upd = [], []
    for ci in range(nch):
        rows = slice(ci * C, (ci + 1) * C)
        sc = _dot_nt(q_cat[rows], k_cat[rows])
        sc = sc + jnp.where(inner_mask, _dot_nt(q_mid[rows], k_mid[rows]), 0.0)
        o_intra.append(_dot(sc.astype(BF16), v_bf[rows]))
        upd.append(_dot_tn(v_bf[rows], k_end[rows]))
        yield

    s_t = s_in
    for ci in (reversed(range(nch)) if rev else range(nch)):
        rows = slice(ci * C, (ci + 1) * C)
        o_ref[0, rows, cols] = (o_intra[ci] + _dot_nt(q_dec[rows], s_t.astype(BF16))).astype(o_ref.dtype)
        s_t = s_t * decay[ci * C:ci * C + 1] + upd[ci]
        yield
    return s_t


def _hgrn_body(qf_ref, lff_ref, kf_ref, vf_ref, qb_ref, lfb_ref, kb_ref, vb_ref,
               of_ref, ob_ref, sf_ref, sb_ref):
    @pl.when(pl.program_id(2) == 0)
    def _():
        sf_ref[...] = jnp.zeros_like(sf_ref)
        sb_ref[...] = jnp.zeros_like(sb_ref)

    runs = []
    for h in range(HG_STEP_HEADS):
        cols = slice(h * HG_DIM, (h + 1) * HG_DIM)
        runs.append((_hgrn_dir(qf_ref[0, :, cols], kf_ref[0, :, cols], vf_ref[0, :, cols], lff_ref[0, :, cols],
                               sf_ref[h], False, of_ref, cols), sf_ref, h))
        runs.append((_hgrn_dir(qb_ref[0, :, cols], kb_ref[0, :, cols], vb_ref[0, :, cols], lfb_ref[0, :, cols],
                               sb_ref[h], True, ob_ref, cols), sb_ref, h))
    while runs:
        for run in list(runs):
            gen, state_ref, h = run
            try:
                next(gen)
            except StopIteration as done:
                state_ref[h] = done.value
                runs.remove(run)


def _hgrn_call(q, lf_f, k_f, lf_b, k_b, v, T):
    B, L, W = q.shape
    nb = L // T
    assert L % T == 0 and T % HG_CHUNK == 0 and W == HG_WIDTH
    width = HG_STEP_HEADS * HG_DIM
    fwd = pl.BlockSpec((1, T, width), lambda b, h, j: (b, j, h))
    bwd = pl.BlockSpec((1, T, width), lambda b, h, j: (b, nb - 1 - j, h))
    return pl.pallas_call(
        _hgrn_body,
        grid=(B, HG_HEADS // HG_STEP_HEADS, nb),
        in_specs=[fwd, fwd, fwd, fwd, bwd, bwd, bwd, bwd],
        out_specs=[fwd, bwd],
        out_shape=[jax.ShapeDtypeStruct((B, L, W), BF16)] * 2,
        scratch_shapes=[pltpu.VMEM((HG_STEP_HEADS, HG_DIM, HG_DIM), F32)] * 2,
        compiler_params=pltpu.CompilerParams(
            dimension_semantics=("parallel", "parallel", "arbitrary"), vmem_limit_bytes=VMEM_LIMIT),
        name="hgrn_scan",
    )(q, lf_f, k_f, v, q, lf_b, k_b, v)


ATT_QBLK = 2 * ATT_RADIUS
ATT_STEP_BLOCKS = 8


def _attn_body(q_ref, kp_ref, kc_ref, kn_ref, vp_ref, vc_ref, vn_ref, o_ref, lse_ref):
    n = pl.program_id(2)
    last = pl.num_programs(2) - 1
    kcat = jnp.concatenate([kp_ref[0], kc_ref[0], kn_ref[0]], axis=0)
    vcat = jnp.concatenate([vp_ref[0], vc_ref[0], vn_ref[0]], axis=0)
    nq, nk = ATT_QBLK, ATT_QBLK + 2 * ATT_RADIUS
    nblk = q_ref.shape[1] // nq
    i = lax.broadcasted_iota(jnp.int32, (nq, nk), 0)
    j = lax.broadcasted_iota(jnp.int32, (nq, nk), 1)
    band = (j >= i) & (j <= i + 2 * ATT_RADIUS)
    has_prev = (j >= ATT_RADIUS) | (n > 0)
    has_next = (j < nq + ATT_RADIUS) | (n < last)

    pair_w = 2 * ATT_HDIM
    npairs = ATT_HEADS // 2
    lane = lax.broadcasted_iota(jnp.int32, kcat.shape, 1) & (pair_w - 1)
    mask_a = jnp.where(lane < ATT_HDIM, 1.0, 0.0).astype(BF16)
    mask_b = jnp.where(lane < ATT_HDIM, 0.0, 1.0).astype(BF16)
    k_a, k_b, v_a, v_b = kcat * mask_a, kcat * mask_b, vcat * mask_a, vcat * mask_b
    lo_k = lax.broadcasted_iota(jnp.int32, (nk, pair_w), 1) < ATT_HDIM
    ones_a = jnp.where(lo_k, 1.0, 0.0).astype(BF16)
    ones_b = jnp.where(lo_k, 0.0, 1.0).astype(BF16)
    lo_q = lax.broadcasted_iota(jnp.int32, (nq, pair_w), 1) < ATT_HDIM

    for blk in range(nblk):
        valid = band
        if blk == 0:
            valid = valid & has_prev
        if blk == nblk - 1:
            valid = valid & has_next
        qrows = slice(blk * nq, (blk + 1) * nq)
        krows = slice(blk * nq, blk * nq + nk)
        scores = []
        for p in range(npairs):
            cols = slice(p * pair_w, (p + 1) * pair_w)
            q = q_ref[0, qrows, cols]
            scores.append(_dot_nt(q, k_a[krows, cols]))
            scores.append(_dot_nt(q, k_b[krows, cols]))
        scores = [jnp.where(valid, s, -jnp.inf) for s in scores]
        tops = [jnp.max(s, axis=-1, keepdims=True) for s in scores]
        probs = [jnp.exp2(s - m).astype(BF16) for s, m in zip(scores, tops)]
        for p in range(npairs):
            cols = slice(p * pair_w, (p + 1) * pair_w)
            res = (_dot(probs[2 * p], jnp.concatenate([v_a[krows, cols], ones_a], axis=1))
                   + _dot(probs[2 * p + 1], jnp.concatenate([v_b[krows, cols], ones_b], axis=1)))
            num, den = res[:, 0:pair_w], res[:, pair_w:2 * pair_w]
            o_ref[0, qrows, cols] = (num / den).astype(o_ref.dtype)
            lse_ref[0, qrows, cols] = jnp.where(lo_q, tops[2 * p], tops[2 * p + 1]) + jnp.log2(den)


def _attn_call(q, k, v, r):
    B, Lr, rW = q.shape
    W = rW // r
    step = ATT_QBLK * min(ATT_STEP_BLOCKS, Lr // ATT_QBLK)
    assert W == ATT_WIDTH and Lr % step == 0
    per_step = step // ATT_RADIUS
    nhalo = Lr // ATT_RADIUS
    center = pl.BlockSpec((1, step, W), lambda b, c, n: (b, n, c))
    prev = pl.BlockSpec((1, ATT_RADIUS, W), lambda b, c, n: (b, jnp.maximum(per_step * n - 1, 0), c))
    nxt = pl.BlockSpec((1, ATT_RADIUS, W), lambda b, c, n: (b, jnp.minimum(per_step * (n + 1), nhalo - 1), c))
    return pl.pallas_call(
        _attn_body,
        grid=(B, r, Lr // step),
        in_specs=[center, prev, center, nxt, prev, center, nxt],
        out_specs=[center, center],
        out_shape=[jax.ShapeDtypeStruct((B, Lr, rW), BF16), jax.ShapeDtypeStruct((B, Lr, rW), F32)],
        compiler_params=pltpu.CompilerParams(
            dimension_semantics=("parallel", "parallel", "parallel"), vmem_limit_bytes=VMEM_LIMIT),
        name=f"dilated_attn_r{r}",
    )(q, k, k, k, v, v, v)


FFN_CHUNK = 256


def _load_natural(a_ref, l_ref, perm_t_ref, r):
    if r == 1:
        return a_ref[0].astype(F32), l_ref[0]
    w = a_ref.shape[2] // r
    rows = PERM_ROWS // r
    a_nat, l_nat = [], []
    for sub in range(a_ref.shape[1] // rows):
        blk = slice(sub * rows, (sub + 1) * rows)
        a = jnp.concatenate([a_ref[0, blk, c * w:(c + 1) * w] for c in range(r)], axis=0)
        l = jnp.concatenate([l_ref[0, blk, c * w:(c + 1) * w] for c in range(r)], axis=0)
        l_hi = l.astype(BF16)
        l_lo = (l - l_hi.astype(F32)).astype(BF16)
        nat = _dot(perm_t_ref[...], jnp.concatenate([a, l_hi, l_lo], axis=1))
        a_nat.append(nat[:, 0:w])
        l_nat.append(nat[:, w:2 * w] + nat[:, 2 * w:3 * w])
    return jnp.concatenate(a_nat, axis=0), jnp.concatenate(l_nat, axis=0)


def _merge_body(x_ref, mod_ref, of_ref, ob_ref, sg_ref, a1_ref, l1_ref, a2_ref, l2_ref, a3_ref, l3_ref,
                ga_ref, gb_ref, gn_ref, n2_ref, fin_ref, wa_ref, wb_ref, wo_ref, wfi_ref, wfo_ref,
                p4t_ref, p16t_ref, y_ref, act_ref):
    mod = mod_ref[0]
    gate1, shift2, scale2, gate2 = mod[2:3], mod[3:4], mod[4:5], mod[5:6]

    o = of_ref[0].astype(F32) + ob_ref[0].astype(F32)
    o_a = jnp.concatenate([_rms(o[:, h * HG_DIM:(h + 1) * HG_DIM]) for h in range(HG_HEADS)], axis=1)
    o_a = o_a * gn_ref[...] * sg_ref[0]

    (a1, l1), (a2, l2), (a3, l3) = [
        _load_natural(a_ref, l_ref, perm_ref, r)
        for (_, r), a_ref, l_ref, perm_ref in zip(DILATION_PATTERNS, (a1_ref, a2_ref, a3_ref),
                                                  (l1_ref, l2_ref, l3_ref), (None, p4t_ref, p16t_ref))]
    top = jnp.maximum(jnp.maximum(l1, l2), l3)
    w1, w2, w3 = jnp.exp2(l1 - top), jnp.exp2(l2 - top), jnp.exp2(l3 - top)
    o_b = (a1 * w1 + a2 * w2 + a3 * w3) / (w1 + w2 + w3)

    merged = (ga_ref[0] * _dot(o_a.astype(BF16), wa_ref[...])
              + gb_ref[0] * _dot(o_b.astype(BF16), wb_ref[...]))
    x1 = x_ref[0] + gate1 * _dot(merged.astype(BF16), wo_ref[...])

    h2 = (_rms(x1) * n2_ref[...] * (1.0 + scale2) + shift2).astype(BF16)
    hidden = wfo_ref.shape[0]
    for c0 in range(0, hidden, FFN_CHUNK):
        gt = _dot(h2, wfi_ref[:, c0:c0 + FFN_CHUNK])
        up = _dot(h2, wfi_ref[:, hidden + c0:hidden + c0 + FFN_CHUNK])
        act_ref[:, c0:c0 + FFN_CHUNK] = (gt * _sigmoid(gt) * up).astype(BF16)
    x2 = x1 + gate2 * _dot(act_ref[...], wfo_ref[...])
    y_ref[0] = _rms(x2) * fin_ref[...]


def _merge_call(x, mod, o_f, o_b, sg, att, ga, gb, gn, n2, fin, wa, wb, wo, wfi, wfo, tm):
    B, L, D = x.shape
    hidden = wfo.shape[0]
    assert L % tm == 0 and tm % PERM_ROWS == 0 and hidden % FFN_CHUNK == 0 and wfi.shape[1] == 2 * hidden
    row = lambda w, r=1: pl.BlockSpec((1, tm // r, r * w), lambda b, i: (b, i, 0))
    att_flat = [t for pair in att for t in pair]
    att_specs = [row(ATT_WIDTH, r) for (_, r) in DILATION_PATTERNS for _ in range(2)]
    perms_t = [_dilation_perm(PERM_ROWS, r).T for (_, r) in DILATION_PATTERNS[1:]]
    return pl.pallas_call(
        _merge_body,
        grid=(B, L // tm),
        in_specs=[row(D), pl.BlockSpec((1,) + mod.shape[1:], lambda b, i: (b, 0, 0))]
                 + [row(512)] * 3 + att_specs + [row(D), row(D)]
                 + [_resident(t.shape) for t in (gn, n2, fin, wa, wb, wo, wfi, wfo, *perms_t)],
        out_specs=row(D),
        out_shape=jax.ShapeDtypeStruct((B, L, D), F32),
        scratch_shapes=[pltpu.VMEM((tm, hidden), BF16)],
        compiler_params=pltpu.CompilerParams(
            dimension_semantics=("parallel", "parallel"), vmem_limit_bytes=VMEM_LIMIT),
        name="merge_ffn",
    )(x, mod, o_f, o_b, sg, *att_flat, ga, gb, gn, n2, fin, wa, wb, wo, wfi, wfo, *perms_t)


def _rope_tables(L):
    half = ATT_HDIM // 2
    lane = np.arange(2 * ATT_HDIM)
    inv = ROPE_THETA ** (-(lane % half).astype(np.float64) / half)
    sign = np.where(lane % ATT_HDIM < half, -1.0, 1.0)
    ang = np.arange(L, dtype=np.float64)[:, None] * inv[None, :]
    return jnp.asarray(np.cos(ang), dtype=F32), jnp.asarray(np.sin(ang) * sign[None, :], dtype=F32)


def _trunk(x, mod, lb, p, tm_in, tm_out, t_scan):
    B, L, D = x.shape
    cos, sin_signed = _rope_tables(L)
    (q_h, lf_f, k_f, lf_b, k_b, v_h, sg, ga, gb, *qkv) = _inproj_call(
        x, mod, p["g1"], p["w_in"], p["b_in"], lb, cos, sin_signed, tm_in)
    o_f, o_b = _hgrn_call(q_h, lf_f, k_f, lf_b, k_b, v_h, t_scan)
    att = [_attn_call(*qkv[3 * i:3 * i + 3], r) for i, (_, r) in enumerate(DILATION_PATTERNS)]
    return _merge_call(x, mod, o_f, o_b, sg, att, ga, gb, p["gn"], p["n2"], p["fin"],
                       p["wa"], p["wb"], p["wo"], p["wfi"], p["wfo"], tm_out)


def kernel(x_prompt, x_sample, c_prompt, c_sample, w_ada, b_ada, norm1_g, w_in, b_in, lb_logits, hg_norm_g, w_branch_a, w_branch_b, w_out, norm2_g, w_ffn_in, w_ffn_out, final_norm_g):
    assert w_ada.shape[0] == 1 and lb_logits.shape[0] == 2, "single-layer trunk"
    D = x_prompt.shape[-1]
    bp, bs = c_prompt.shape[0], c_sample.shape[0]
    c_all = jnp.concatenate([c_prompt, c_sample], axis=0)
    c_pad = jnp.pad(c_all, ((0, -(bp + bs) % 8), (0, 0)))
    mod = _mod_call(c_pad, w_ada[0], b_ada[0]).reshape(c_pad.shape[0], 6, D)
    lb = _lb_call(lb_logits)
    row = lambda v: v.reshape(1, -1).astype(F32)
    p = dict(g1=row(norm1_g[0]), w_in=w_in[0].astype(BF16), b_in=row(b_in[0]), gn=row(hg_norm_g[0]),
             n2=row(norm2_g[0]), fin=row(final_norm_g), wa=w_branch_a[0].astype(BF16),
             wb=w_branch_b[0].astype(BF16), wo=w_out[0].astype(BF16), wfi=w_ffn_in[0].astype(BF16),
             wfo=w_ffn_out[0].astype(BF16))
    y_prompt = _trunk(x_prompt, mod[:bp], lb, p, 512, 512, 2048)
    y_sample = _trunk(x_sample, mod[bp:bp + bs], lb, p, 512, 512, 2048)
    return (y_prompt, y_sample)
```

```python
import numpy as np
import jax
import jax.numpy as jnp
from jax import lax
from jax.experimental import pallas as pl
from jax.experimental.pallas import tpu as pltpu

F32 = jnp.float32
BF16 = jnp.bfloat16
LANES = 128
SUBLANES = 8

NORM_EPS = 1e-6
HG_HEADS = 4
HG_DIM = 128
HG_WIDTH = HG_HEADS * HG_DIM
ATT_HEADS = 8
ATT_HDIM = 64
ATT_WIDTH = ATT_HEADS * ATT_HDIM
DILATION_PATTERNS = ((128, 1), (512, 4), (2048, 16))
ATT_RADIUS = 64
ROPE_THETA = 10000.0
LOG2E = 1.4426950408889634
PROJ_BLOCK = 512
N_PROJ_BLOCKS = 12
PERM_ROWS = 256

HG_CHUNK = 64
HG_SUB = 16
HG_SUB_SHIFT = 4
HG_STEP_HEADS = 2
HG_EXP2_CLAMP = 110.0
EXP2_ZERO = -1e30
VMEM_LIMIT = 56 * 1024 * 1024


def _dot(a, b):
    return jnp.dot(a, b, preferred_element_type=F32)


def _dot_nt(a, b):
    return lax.dot_general(a, b, (((1,), (1,)), ((), ())), preferred_element_type=F32)


def _dot_tn(a, b):
    return lax.dot_general(a, b, (((0,), (0,)), ((), ())), preferred_element_type=F32)


def _sigmoid(x):
    return 1.0 / (1.0 + jnp.exp(-x))


def _rms(x):
    return x * lax.rsqrt(jnp.mean(x * x, axis=-1, keepdims=True) + NORM_EPS)


def _resident(shape):
    nd = len(shape)
    return pl.BlockSpec(shape, lambda *_: (0,) * nd, pipeline_mode=pl.Buffered(1))


def _mod_body(c_ref, w_ref, b_ref, o_ref):
    nrows, bn = c_ref.shape[0], w_ref.shape[1]
    outs = []
    for r in range(nrows):
        c = c_ref[r]
        a = c * _sigmoid(c)
        cols = [jnp.sum(a * w_ref[:, j * LANES:(j + 1) * LANES], axis=0, keepdims=True)
                for j in range(bn // LANES)]
        outs.append(jnp.concatenate(cols, axis=1) + b_ref[...])
    outs.append(jnp.zeros((o_ref.shape[0] - nrows, bn), F32))
    o_ref[...] = jnp.concatenate(outs, axis=0)


def _mod_call(c, w_ada, b_ada):
    nrows, d = c.shape
    n = w_ada.shape[1]
    bn = 1024
    rows_out = -(-nrows // SUBLANES) * SUBLANES
    c_rep = jnp.broadcast_to(c[:, :, None], (nrows, d, LANES))
    return pl.pallas_call(
        _mod_body,
        grid=(n // bn,),
        in_specs=[pl.BlockSpec((nrows, d, LANES), lambda j: (0, 0, 0)),
                  pl.BlockSpec((d, bn), lambda j: (0, j)),
                  pl.BlockSpec((1, bn), lambda j: (0, j))],
        out_specs=pl.BlockSpec((rows_out, bn), lambda j: (0, j)),
        out_shape=jax.ShapeDtypeStruct((rows_out, n), F32),
        name="adaln_mod",
    )(c_rep, w_ada, b_ada.reshape(1, n))


def _lb_body(l_ref, o_ref):
    l = l_ref[...]
    e = jnp.exp(l - jnp.max(l, axis=0, keepdims=True))
    o_ref[...] = e[0:1] / jnp.sum(e, axis=0, keepdims=True)


def _lb_call(lb_logits):
    n = lb_logits.shape[0]
    flat = lb_logits.reshape(n, -1).astype(F32)
    return pl.pallas_call(
        _lb_body,
        out_shape=jax.ShapeDtypeStruct((1, flat.shape[1]), F32),
        name="hgrn_lower_bounds",
    )(flat)


def _rope(p, cos, sin_signed):
    n = p.shape[-1]
    lane = lax.broadcasted_iota(jnp.int32, p.shape, 1)
    first_half = (lane & 63) < 32
    partner = jnp.where(first_half, pltpu.roll(p, n - 32, 1), pltpu.roll(p, 32, 1))
    reps = n // cos.shape[-1]
    return p * jnp.tile(cos, (1, reps)) + partner * jnp.tile(sin_signed, (1, reps))


def _store_dilated(t_bf, perm_ref, out_ref, r):
    if r == 1:
        out_ref[0] = t_bf
        return
    tm, w = t_bf.shape
    rows = PERM_ROWS // r
    for sub in range(tm // PERM_ROWS):
        grouped = _dot(perm_ref[...], t_bf[sub * PERM_ROWS:(sub + 1) * PERM_ROWS]).astype(BF16)
        for c in range(r):
            out_ref[0, sub * rows:(sub + 1) * rows, c * w:(c + 1) * w] = grouped[c * rows:(c + 1) * rows]


def _inproj_body(x_ref, mod_ref, g1_ref, w_ref, b_ref, lb_ref, cos_ref, sin_ref, p4_ref, p16_ref,
                 qh_ref, lff_ref, kf_ref, lfb_ref, kb_ref, vh_ref, sg_ref, ga_ref, gb_ref,
                 q1_ref, k1_ref, v1_ref, q4_ref, k4_ref, v4_ref, q16_ref, k16_ref, v16_ref):
    x = x_ref[0]
    mod = mod_ref[0]
    shift1, scale1 = mod[0:1], mod[1:2]
    h = _rms(x) * g1_ref[...]
    h = h * (1.0 + scale1) + shift1
    hb = h.astype(BF16)

    def proj(j):
        cols = slice(j * PROJ_BLOCK, (j + 1) * PROJ_BLOCK)
        return _dot(hb, w_ref[:, cols]) + b_ref[:, cols]

    p = proj(0)
    qh_ref[0] = p * _sigmoid(p) * (HG_DIM ** -0.5)

    def gates(p, lb, lf_ref, k_ref):
        f = lb + (1.0 - lb) * _sigmoid(p)
        lf_ref[0] = jnp.log2(f)
        k_ref[0] = 1.0 - f

    gates(proj(1), lb_ref[:, 0:HG_WIDTH], lff_ref, kf_ref)
    gates(proj(2), lb_ref[:, HG_WIDTH:2 * HG_WIDTH], lfb_ref, kb_ref)
    vh_ref[0] = proj(3).astype(BF16)
    p = proj(4)
    sg_ref[0] = (p * _sigmoid(p)).astype(sg_ref.dtype)

    cos, sin_signed = cos_ref[...], sin_ref[...]
    q_a = (_rope(proj(5), cos, sin_signed) * (ATT_HDIM ** -0.5 * LOG2E)).astype(BF16)
    k_a = _rope(proj(6), cos, sin_signed).astype(BF16)
    v_a = proj(7).astype(BF16)
    for t, refs in ((q_a, (q1_ref, q4_ref, q16_ref)), (k_a, (k1_ref, k4_ref, k16_ref)),
                    (v_a, (v1_ref, v4_ref, v16_ref))):
        for (_, r), perm_ref, out_ref in zip(DILATION_PATTERNS, (None, p4_ref, p16_ref), refs):
            _store_dilated(t, perm_ref, out_ref, r)
    ga_ref[0, :, 0:PROJ_BLOCK] = _sigmoid(proj(8)).astype(ga_ref.dtype)
    ga_ref[0, :, PROJ_BLOCK:2 * PROJ_BLOCK] = _sigmoid(proj(9)).astype(ga_ref.dtype)
    gb_ref[0, :, 0:PROJ_BLOCK] = _sigmoid(proj(10)).astype(gb_ref.dtype)
    gb_ref[0, :, PROJ_BLOCK:2 * PROJ_BLOCK] = _sigmoid(proj(11)).astype(gb_ref.dtype)


def _dilation_perm(tm, r):
    i = np.arange(tm)
    src = r * (i % (tm // r)) + i // (tm // r)
    return jnp.asarray(src[:, None] == np.arange(tm)[None, :], dtype=BF16)


def _inproj_call(x, mod, g1, w_in, b_in, lb, cos, sin_signed, tm):
    B, L, D = x.shape
    n_in = w_in.shape[1]
    assert n_in == N_PROJ_BLOCKS * PROJ_BLOCK and L % tm == 0 and tm % PERM_ROWS == 0
    row = lambda w, r=1: pl.BlockSpec((1, tm // r, r * w), lambda b, i: (b, i, 0))
    tab = pl.BlockSpec((tm, cos.shape[1]), lambda b, i: (i, 0))
    shp = lambda w, dt, r=1: jax.ShapeDtypeStruct((B, L // r, r * w), dt)
    perms = [_dilation_perm(PERM_ROWS, r) for (_, r) in DILATION_PATTERNS[1:]]
    att_specs = [row(ATT_WIDTH, r) for (_, r) in DILATION_PATTERNS for _ in range(3)]
    att_shapes = [shp(ATT_WIDTH, BF16, r) for (_, r) in DILATION_PATTERNS for _ in range(3)]
    return pl.pallas_call(
        _inproj_body,
        grid=(B, L // tm),
        in_specs=[row(D),
                  pl.BlockSpec((1,) + mod.shape[1:], lambda b, i: (b, 0, 0)),
                  _resident(g1.shape), _resident(w_in.shape), _resident(b_in.shape), _resident(lb.shape),
                  tab, tab] + [_resident(p.shape) for p in perms],
        out_specs=[row(512)] * 7 + [row(1024)] * 2 + att_specs,
        out_shape=[shp(512, F32), shp(512, F32), shp(512, F32), shp(512, F32), shp(512, F32),
                   shp(512, BF16), shp(512, BF16), shp(1024, BF16), shp(1024, BF16)] + att_shapes,
        compiler_params=pltpu.CompilerParams(
            dimension_semantics=("parallel", "parallel"), vmem_limit_bytes=VMEM_LIMIT),
        name="inproj",
    )(x, mod, g1, w_in, b_in, lb, cos, sin_signed, *perms)


def _bcast_row(c, group, r):
    T, w = c.shape
    c3 = c.reshape(T // group, group, w)
    return jnp.broadcast_to(c3[:, r:r + 1, :], c3.shape).reshape(T, w)


def _split2(g):
    g1 = g.astype(BF16)
    return g1, (g - g1.astype(F32)).astype(BF16)


def _hgrn_dir(q, k, v_bf, g, s_in, rev, o_ref, cols):
    T, dk = q.shape
    C, nch, nsb = HG_CHUNK, T // HG_CHUNK, HG_CHUNK // HG_SUB
    t = lax.broadcasted_iota(jnp.int32, (C, C), 0)
    s = lax.broadcasted_iota(jnp.int32, (C, C), 1)
    order = (s >= t) if rev else (s <= t)
    inner_mask = order & ((t >> HG_SUB_SHIFT) == (s >> HG_SUB_SHIFT))
    tri = order.astype(BF16)

    g_terms = jnp.concatenate(_split2(g), axis=1)
    c_parts = []
    for ci in range(nch):
        r = _dot(tri, g_terms[ci * C:(ci + 1) * C])
        c_parts.append(r[:, 0:dk] + r[:, dk:2 * dk])
    c = jnp.concatenate(c_parts, axis=0)
    yield

    far = 0 if rev else HG_SUB - 1
    edge = _bcast_row(c, HG_SUB, far)
    mid = _bcast_row(c, HG_SUB, HG_SUB // 2)
    c_end = _bcast_row(c, C, 0 if rev else C - 1)
    rowc = lax.broadcasted_iota(jnp.int32, (T, dk), 0) & (C - 1)

    k_edge = k * jnp.exp2(edge - c)
    q_cat, k_cat = [], []
    for J in (range(1, nsb) if rev else range(nsb - 1)):
        ref_j = _bcast_row(c, C, J * HG_SUB + far)
        queries = (rowc < J * HG_SUB) if rev else (rowc >= (J + 1) * HG_SUB)
        q_cat.append((q * jnp.exp2(jnp.where(queries, c - ref_j, EXP2_ZERO))).astype(BF16))
        in_j = (rowc >= J * HG_SUB) & (rowc < (J + 1) * HG_SUB)
        k_cat.append(jnp.where(in_j, k_edge, 0.0).astype(BF16))
    q_cat = jnp.concatenate(q_cat, axis=1)
    k_cat = jnp.concatenate(k_cat, axis=1)

    d_mid = jnp.clip(c - mid, -HG_EXP2_CLAMP, HG_EXP2_CLAMP)
    q_mid = (q * jnp.exp2(d_mid)).astype(BF16)
    k_mid = (k * jnp.exp2(-d_mid)).astype(BF16)
    q_dec = (q * jnp.exp2(c)).astype(BF16)
    k_end = (k * jnp.exp2(c_end - c)).astype(BF16)
    decay = jnp.exp2(c_end)
    yield

    o_intra, upd = [], []
    for ci in range(nch):
        rows = slice(ci * C, (ci + 1) * C)
        sc = _dot_nt(q_cat[rows], k_cat[rows])
        sc = sc + jnp.where(inner_mask, _dot_nt(q_mid[rows], k_mid[rows]), 0.0)
        o_intra.append(_dot(sc.astype(BF16), v_bf[rows]))
        upd.append(_dot_tn(v_bf[rows], k_end[rows]))
        yield

    s_t = s_in
    for ci in (reversed(range(nch)) if rev else range(nch)):
        rows = slice(ci * C, (ci + 1) * C)
        o_ref[0, rows, cols] = (o_intra[ci] + _dot_nt(q_dec[rows], s_t.astype(BF16))).astype(o_ref.dtype)
        s_t = s_t * decay[ci * C:ci * C + 1] + upd[ci]
        yield
    return s_t


def _hgrn_body(qf_ref, lff_ref, kf_ref, vf_ref, qb_ref, lfb_ref, kb_ref, vb_ref,
               of_ref, ob_ref, sf_ref, sb_ref):
    @pl.when(pl.program_id(2) == 0)
    def _():
        sf_ref[...] = jnp.zeros_like(sf_ref)
        sb_ref[...] = jnp.zeros_like(sb_ref)

    runs = []
    for h in range(HG_STEP_HEADS):
        cols = slice(h * HG_DIM, (h + 1) * HG_DIM)
        runs.append((_hgrn_dir(qf_ref[0, :, cols], kf_ref[0, :, cols], vf_ref[0, :, cols], lff_ref[0, :, cols],
                               sf_ref[h], False, of_ref, cols), sf_ref, h))
        runs.append((_hgrn_dir(qb_ref[0, :, cols], kb_ref[0, :, cols], vb_ref[0, :, cols], lfb_ref[0, :, cols],
                               sb_ref[h], True, ob_ref, cols), sb_ref, h))
    while runs:
        for run in list(runs):
            gen, state_ref, h = run
            try:
                next(gen)
            except StopIteration as done:
                state_ref[h] = done.value
                runs.remove(run)


def _hgrn_call(q, lf_f, k_f, lf_b, k_b, v, T):
    B, L, W = q.shape
    nb = L // T
    assert L % T == 0 and T % HG_CHUNK == 0 and W == HG_WIDTH
    width = HG_STEP_HEADS * HG_DIM
    fwd = pl.BlockSpec((1, T, width), lambda b, h, j: (b, j, h))
    bwd = pl.BlockSpec((1, T, width), lambda b, h, j: (b, nb - 1 - j, h))
    return pl.pallas_call(
        _hgrn_body,
        grid=(B, HG_HEADS // HG_STEP_HEADS, nb),
        in_specs=[fwd, fwd, fwd, fwd, bwd, bwd, bwd, bwd],
        out_specs=[fwd, bwd],
        out_shape=[jax.ShapeDtypeStruct((B, L, W), BF16)] * 2,
        scratch_shapes=[pltpu.VMEM((HG_STEP_HEADS, HG_DIM, HG_DIM), F32)] * 2,
        compiler_params=pltpu.CompilerParams(
            dimension_semantics=("parallel", "parallel", "arbitrary"), vmem_limit_bytes=VMEM_LIMIT),
        name="hgrn_scan",
    )(q, lf_f, k_f, v, q, lf_b, k_b, v)


ATT_QBLK = 2 * ATT_RADIUS
ATT_STEP_BLOCKS = 8


def _attn_body(q_ref, kp_ref, kc_ref, kn_ref, vp_ref, vc_ref, vn_ref, o_ref, lse_ref):
    n = pl.program_id(2)
    last = pl.num_programs(2) - 1
    kcat = jnp.concatenate([kp_ref[0], kc_ref[0], kn_ref[0]], axis=0)
    vcat = jnp.concatenate([vp_ref[0], vc_ref[0], vn_ref[0]], axis=0)
    nq, nk = ATT_QBLK, ATT_QBLK + 2 * ATT_RADIUS
    nblk = q_ref.shape[1] // nq
    i = lax.broadcasted_iota(jnp.int32, (nq, nk), 0)
    j = lax.broadcasted_iota(jnp.int32, (nq, nk), 1)
    band = (j >= i) & (j <= i + 2 * ATT_RADIUS)
    has_prev = (j >= ATT_RADIUS) | (n > 0)
    has_next = (j < nq + ATT_RADIUS) | (n < last)

    pair_w = 2 * ATT_HDIM
    npairs = ATT_HEADS // 2
    lane = lax.broadcasted_iota(jnp.int32, kcat.shape, 1) & (pair_w - 1)
    mask_a = jnp.where(lane < ATT_HDIM, 1.0, 0.0).astype(BF16)
    mask_b = jnp.where(lane < ATT_HDIM, 0.0, 1.0).astype(BF16)
    k_a, k_b, v_a, v_b = kcat * mask_a, kcat * mask_b, vcat * mask_a, vcat * mask_b
    lo_k = lax.broadcasted_iota(jnp.int32, (nk, pair_w), 1) < ATT_HDIM
    ones_a = jnp.where(lo_k, 1.0, 0.0).astype(BF16)
    ones_b = jnp.where(lo_k, 0.0, 1.0).astype(BF16)
    lo_q = lax.broadcasted_iota(jnp.int32, (nq, pair_w), 1) < ATT_HDIM

    for blk in range(nblk):
        valid = band
        if blk == 0:
            valid = valid & has_prev
        if blk == nblk - 1:
            valid = valid & has_next
        qrows = slice(blk * nq, (blk + 1) * nq)
        krows = slice(blk * nq, blk * nq + nk)
        scores = []
        for p in range(npairs):
            cols = slice(p * pair_w, (p + 1) * pair_w)
            q = q_ref[0, qrows, cols]
            scores.append(_dot_nt(q, k_a[krows, cols]))
            scores.append(_dot_nt(q, k_b[krows, cols]))
        scores = [jnp.where(valid, s, -jnp.inf) for s in scores]
        tops = [jnp.max(s, axis=-1, keepdims=True) for s in scores]
        probs = [jnp.exp2(s - m).astype(BF16) for s, m in zip(scores, tops)]
        for p in range(npairs):
            cols = slice(p * pair_w, (p + 1) * pair_w)
            res = (_dot(probs[2 * p], jnp.concatenate([v_a[krows, cols], ones_a], axis=1))
                   + _dot(probs[2 * p + 1], jnp.concatenate([v_b[krows, cols], ones_b], axis=1)))
            num, den = res[:, 0:pair_w], res[:, pair_w:2 * pair_w]
            o_ref[0, qrows, cols] = (num / den).astype(o_ref.dtype)
            lse_ref[0, qrows, cols] = jnp.where(lo_q, tops[2 * p], tops[2 * p + 1]) + jnp.log2(den)


def _attn_call(q, k, v, r):
    B, Lr, rW = q.shape
    W = rW // r
    step = ATT_QBLK * min(ATT_STEP_BLOCKS, Lr // ATT_QBLK)
    assert W == ATT_WIDTH and Lr % step == 0
    per_step = step // ATT_RADIUS
    nhalo = Lr // ATT_RADIUS
    center = pl.BlockSpec((1, step, W), lambda b, c, n: (b, n, c))
    prev = pl.BlockSpec((1, ATT_RADIUS, W), lambda b, c, n: (b, jnp.maximum(per_step * n - 1, 0), c))
    nxt = pl.BlockSpec((1, ATT_RADIUS, W), lambda b, c, n: (b, jnp.minimum(per_step * (n + 1), nhalo - 1), c))
    return pl.pallas_call(
        _attn_body,
        grid=(B, r, Lr // step),
        in_specs=[center, prev, center, nxt, prev, center, nxt],
        out_specs=[center, center],
        out_shape=[jax.ShapeDtypeStruct((B, Lr, rW), BF16), jax.ShapeDtypeStruct((B, Lr, rW), F32)],
        compiler_params=pltpu.CompilerParams(
            dimension_semantics=("parallel", "parallel", "parallel"), vmem_limit_bytes=VMEM_LIMIT),
        name=f"dilated_attn_r{r}",
    )(q, k, k, k, v, v, v)


FFN_CHUNK = 256


def _load_natural(a_ref, l_ref, perm_t_ref, r):
    if r == 1:
        return a_ref[0].astype(F32), l_ref[0]
    w = a_ref.shape[2] // r
    rows = PERM_ROWS // r
    a_nat, l_nat = [], []
    for sub in range(a_ref.shape[1] // rows):
        blk = slice(sub * rows, (sub + 1) * rows)
        a = jnp.concatenate([a_ref[0, blk, c * w:(c + 1) * w] for c in range(r)], axis=0)
        l = jnp.concatenate([l_ref[0, blk, c * w:(c + 1) * w] for c in range(r)], axis=0)
        l_hi = l.astype(BF16)
        l_lo = (l - l_hi.astype(F32)).astype(BF16)
        nat = _dot(perm_t_ref[...], jnp.concatenate([a, l_hi, l_lo], axis=1))
        a_nat.append(nat[:, 0:w])
        l_nat.append(nat[:, w:2 * w] + nat[:, 2 * w:3 * w])
    return jnp.concatenate(a_nat, axis=0), jnp.concatenate(l_nat, axis=0)


def _merge_body(x_ref, mod_ref, of_ref, ob_ref, sg_ref, a1_ref, l1_ref, a2_ref, l2_ref, a3_ref, l3_ref,
                ga_ref, gb_ref, gn_ref, n2_ref, fin_ref, wa_ref, wb_ref, wo_ref, wfi_ref, wfo_ref,
                p4t_ref, p16t_ref, y_ref, act_ref):
    mod = mod_ref[0]
    gate1, shift2, scale2, gate2 = mod[2:3], mod[3:4], mod[4:5], mod[5:6]

    o = of_ref[0].astype(F32) + ob_ref[0].astype(F32)
    o_a = jnp.concatenate([_rms(o[:, h * HG_DIM:(h + 1) * HG_DIM]) for h in range(HG_HEADS)], axis=1)
    o_a = o_a * gn_ref[...] * sg_ref[0]

    (a1, l1), (a2, l2), (a3, l3) = [
        _load_natural(a_ref, l_ref, perm_ref, r)
        for (_, r), a_ref, l_ref, perm_ref in zip(DILATION_PATTERNS, (a1_ref, a2_ref, a3_ref),
                                                  (l1_ref, l2_ref, l3_ref), (None, p4t_ref, p16t_ref))]
    top = jnp.maximum(jnp.maximum(l1, l2), l3)
    w1, w2, w3 = jnp.exp2(l1 - top), jnp.exp2(l2 - top), jnp.exp2(l3 - top)
    o_b = (a1 * w1 + a2 * w2 + a3 * w3) / (w1 + w2 + w3)

    merged = (ga_ref[0] * _dot(o_a.astype(BF16), wa_ref[...])
              + gb_ref[0] * _dot(o_b.astype(BF16), wb_ref[...]))
    x1 = x_ref[0] + gate1 * _dot(merged.astype(BF16), wo_ref[...])

    h2 = (_rms(x1) * n2_ref[...] * (1.0 + scale2) + shift2).astype(BF16)
    hidden = wfo_ref.shape[0]
    for c0 in range(0, hidden, FFN_CHUNK):
        gt = _dot(h2, wfi_ref[:, c0:c0 + FFN_CHUNK])
        up = _dot(h2, wfi_ref[:, hidden + c0:hidden + c0 + FFN_CHUNK])
        act_ref[:, c0:c0 + FFN_CHUNK] = (gt * _sigmoid(gt) * up).astype(BF16)
    x2 = x1 + gate2 * _dot(act_ref[...], wfo_ref[...])
    y_ref[0] = _rms(x2) * fin_ref[...]


def _merge_call(x, mod, o_f, o_b, sg, att, ga, gb, gn, n2, fin, wa, wb, wo, wfi, wfo, tm):
    B, L, D = x.shape
    hidden = wfo.shape[0]
    assert L % tm == 0 and tm % PERM_ROWS == 0 and hidden % FFN_CHUNK == 0 and wfi.shape[1] == 2 * hidden
    row = lambda w, r=1: pl.BlockSpec((1, tm // r, r * w), lambda b, i: (b, i, 0))
    att_flat = [t for pair in att for t in pair]
    att_specs = [row(ATT_WIDTH, r) for (_, r) in DILATION_PATTERNS for _ in range(2)]
    perms_t = [_dilation_perm(PERM_ROWS, r).T for (_, r) in DILATION_PATTERNS[1:]]
    return pl.pallas_call(
        _merge_body,
        grid=(B, L // tm),
        in_specs=[row(D), pl.BlockSpec((1,) + mod.shape[1:], lambda b, i: (b, 0, 0))]
                 + [row(512)] * 3 + att_specs + [row(D), row(D)]
                 + [_resident(t.shape) for t in (gn, n2, fin, wa, wb, wo, wfi, wfo, *perms_t)],
        out_specs=row(D),
        out_shape=jax.ShapeDtypeStruct((B, L, D), F32),
        scratch_shapes=[pltpu.VMEM((tm, hidden), BF16)],
        compiler_params=pltpu.CompilerParams(
            dimension_semantics=("parallel", "parallel"), vmem_limit_bytes=VMEM_LIMIT),
        name="merge_ffn",
    )(x, mod, o_f, o_b, sg, *att_flat, ga, gb, gn, n2, fin, wa, wb, wo, wfi, wfo, *perms_t)


def _rope_tables(L):
    half = ATT_HDIM // 2
    lane = np.arange(2 * ATT_HDIM)
    inv = ROPE_THETA ** (-(lane % half).astype(np.float64) / half)
    sign = np.where(lane % ATT_HDIM < half, -1.0, 1.0)
    ang = np.arange(L, dtype=np.float64)[:, None] * inv[None, :]
    return jnp.asarray(np.cos(ang), dtype=F32), jnp.asarray(np.sin(ang) * sign[None, :], dtype=F32)


def _trunk(x, mod, lb, p, tm_in, tm_out, t_scan):
    B, L, D = x.shape
    cos, sin_signed = _rope_tables(L)
    (q_h, lf_f, k_f, lf_b, k_b, v_h, sg, ga, gb, *qkv) = _inproj_call(
        x, mod, p["g1"], p["w_in"], p["b_in"], lb, cos, sin_signed, tm_in)
    o_f, o_b = _hgrn_call(q_h, lf_f, k_f, lf_b, k_b, v_h, t_scan)
    att = [_attn_call(*qkv[3 * i:3 * i + 3], r) for i, (_, r) in enumerate(DILATION_PATTERNS)]
    return _merge_call(x, mod, o_f, o_b, sg, att, ga, gb, p["gn"], p["n2"], p["fin"],
                       p["wa"], p["wb"], p["wo"], p["wfi"], p["wfo"], tm_out)


def kernel(x_prompt, x_sample, c_prompt, c_sample, w_ada, b_ada, norm1_g, w_in, b_in, lb_logits, hg_norm_g, w_branch_a, w_branch_b, w_out, norm2_g, w_ffn_in, w_ffn_out, final_norm_g):
    assert w_ada.shape[0] == 1 and lb_logits.shape[0] == 2, "single-layer trunk"
    D = x_prompt.shape[-1]
    bp, bs = c_prompt.shape[0], c_sample.shape[0]
    c_all = jnp.concatenate([c_prompt, c_sample], axis=0)
    mod = _mod_call(c_all, w_ada[0], b_ada[0])
    mod = mod.reshape(mod.shape[0], 6, D)
    lb = _lb_call(lb_logits)
    row = lambda v: v.reshape(1, -1).astype(F32)
    p = dict(g1=row(norm1_g[0]), w_in=w_in[0].astype(BF16), b_in=row(b_in[0]), gn=row(hg_norm_g[0]),
             n2=row(norm2_g[0]), fin=row(final_norm_g), wa=w_branch_a[0].astype(BF16),
             wb=w_branch_b[0].astype(BF16), wo=w_out[0].astype(BF16), wfi=w_ffn_in[0].astype(BF16),
             wfo=w_ffn_out[0].astype(BF16))
    y_prompt = _trunk(x_prompt, mod[:bp], lb, p, 512, 512, 2048)
    y_sample = _trunk(x_sample, mod[bp:bp + bs], lb, p, 512, 512, 2048)
    return (y_prompt, y_sample)
```

```python
import numpy as np
import jax
import jax.numpy as jnp
from jax import lax
from jax.experimental import pallas as pl
from jax.experimental.pallas import tpu as pltpu

F32 = jnp.float32
BF16 = jnp.bfloat16
LANES = 128
SUBLANES = 8

NORM_EPS = 1e-6
HG_HEADS = 4
HG_DIM = 128
HG_WIDTH = HG_HEADS * HG_DIM
ATT_HEADS = 8
ATT_HDIM = 64
ATT_WIDTH = ATT_HEADS * ATT_HDIM
DILATION_PATTERNS = ((128, 1), (512, 4), (2048, 16))
ATT_RADIUS = 64
ROPE_THETA = 10000.0
LOG2E = 1.4426950408889634
PROJ_BLOCK = 512
N_PROJ_BLOCKS = 12
N_BF16_BLOCKS = 8
FP8 = jnp.float8_e4m3fn
FP8_TOP = 240.0
PERM_ROWS = 256

HG_CHUNK = 64
HG_SUB = 16
HG_SUB_SHIFT = 4
HG_STEP_HEADS = 2
HG_EXP2_CLAMP = 110.0
EXP2_ZERO = -1e30
VMEM_LIMIT = 56 * 1024 * 1024


def _dot(a, b):
    return jnp.dot(a, b, preferred_element_type=F32)


def _dot_nt(a, b):
    return lax.dot_general(a, b, (((1,), (1,)), ((), ())), preferred_element_type=F32)


def _dot_tn(a, b):
    return lax.dot_general(a, b, (((0,), (0,)), ((), ())), preferred_element_type=F32)


def _sigmoid(x):
    return 1.0 / (1.0 + jnp.exp(-x))


def _rms(x):
    return x * lax.rsqrt(jnp.mean(x * x, axis=-1, keepdims=True) + NORM_EPS)


def _resident(shape):
    nd = len(shape)
    return pl.BlockSpec(shape, lambda *_: (0,) * nd, pipeline_mode=pl.Buffered(1))


def _mod_body(c_ref, w_ref, b_ref, o_ref):
    nrows, bn = c_ref.shape[0], w_ref.shape[1]
    outs = []
    for r in range(nrows):
        c = c_ref[r]
        a = c * _sigmoid(c)
        cols = [jnp.sum(a * w_ref[:, j * LANES:(j + 1) * LANES], axis=0, keepdims=True)
                for j in range(bn // LANES)]
        outs.append(jnp.concatenate(cols, axis=1) + b_ref[...])
    outs.append(jnp.zeros((o_ref.shape[0] - nrows, bn), F32))
    o_ref[...] = jnp.concatenate(outs, axis=0)


def _mod_call(c, w_ada, b_ada):
    nrows, d = c.shape
    n = w_ada.shape[1]
    bn = 1024
    rows_out = -(-nrows // SUBLANES) * SUBLANES
    c_rep = jnp.broadcast_to(c[:, :, None], (nrows, d, LANES))
    return pl.pallas_call(
        _mod_body,
        grid=(n // bn,),
        in_specs=[pl.BlockSpec((nrows, d, LANES), lambda j: (0, 0, 0)),
                  pl.BlockSpec((d, bn), lambda j: (0, j)),
                  pl.BlockSpec((1, bn), lambda j: (0, j))],
        out_specs=pl.BlockSpec((rows_out, bn), lambda j: (0, j)),
        out_shape=jax.ShapeDtypeStruct((rows_out, n), F32),
        name="adaln_mod",
    )(c_rep, w_ada, b_ada.reshape(1, n))


def _lb_body(l_ref, o_ref):
    l = l_ref[...]
    e = jnp.exp(l - jnp.max(l, axis=0, keepdims=True))
    o_ref[...] = e[0:1] / jnp.sum(e, axis=0, keepdims=True)


def _lb_call(lb_logits):
    n = lb_logits.shape[0]
    flat = lb_logits.reshape(n, -1).astype(F32)
    return pl.pallas_call(
        _lb_body,
        out_shape=jax.ShapeDtypeStruct((1, flat.shape[1]), F32),
        name="hgrn_lower_bounds",
    )(flat)


def _rope(p, cos, sin_signed):
    n = p.shape[-1]
    lane = lax.broadcasted_iota(jnp.int32, p.shape, 1)
    first_half = (lane & 63) < 32
    partner = jnp.where(first_half, pltpu.roll(p, n - 32, 1), pltpu.roll(p, 32, 1))
    reps = n // cos.shape[-1]
    return p * jnp.tile(cos, (1, reps)) + partner * jnp.tile(sin_signed, (1, reps))


def _store_dilated(t_bf, perm_ref, out_ref, r):
    if r == 1:
        out_ref[0] = t_bf
        return
    tm, w = t_bf.shape
    rows = PERM_ROWS // r
    for sub in range(tm // PERM_ROWS):
        grouped = _dot(perm_ref[...], t_bf[sub * PERM_ROWS:(sub + 1) * PERM_ROWS]).astype(BF16)
        for c in range(r):
            out_ref[0, sub * rows:(sub + 1) * rows, c * w:(c + 1) * w] = grouped[c * rows:(c + 1) * rows]


def _inproj_body(x_ref, mod_ref, g1_ref, w_ref, w8_ref, w8s_ref, b_ref, lb_ref, cos_ref, sin_ref, p4_ref, p16_ref,
                 qh_ref, lff_ref, kf_ref, lfb_ref, kb_ref, vh_ref, sg_ref, ga_ref, gb_ref,
                 q1_ref, k1_ref, v1_ref, q4_ref, k4_ref, v4_ref, q16_ref, k16_ref, v16_ref):
    x = x_ref[0]
    mod = mod_ref[0]
    shift1, scale1 = mod[0:1], mod[1:2]
    h = _rms(x) * g1_ref[...]
    h = h * (1.0 + scale1) + shift1
    hb = h.astype(BF16)
    h_top = jnp.max(jnp.abs(h), axis=-1, keepdims=True)
    h_top = jnp.where(h_top > 0.0, h_top, 1.0)
    h_unscale = h_top * (1.0 / FP8_TOP)
    hb8 = (h * (FP8_TOP / h_top)).astype(FP8)

    def proj8(j):
        cols = slice(j * PROJ_BLOCK, (j + 1) * PROJ_BLOCK)
        cols8 = slice((j - N_BF16_BLOCKS) * PROJ_BLOCK, (j - N_BF16_BLOCKS + 1) * PROJ_BLOCK)
        return _dot(hb8, w8_ref[:, cols8]) * h_unscale * w8s_ref[:, cols8] + b_ref[:, cols]

    def proj(j):
        cols = slice(j * PROJ_BLOCK, (j + 1) * PROJ_BLOCK)
        return _dot(hb, w_ref[:, cols]) + b_ref[:, cols]

    p = proj(0)
    qh_ref[0] = p * _sigmoid(p) * (HG_DIM ** -0.5)

    def gates(p, lb, lf_ref, k_ref):
        f = lb + (1.0 - lb) * _sigmoid(p)
        lf_ref[0] = jnp.log2(f)
        k_ref[0] = 1.0 - f

    gates(proj(1), lb_ref[:, 0:HG_WIDTH], lff_ref, kf_ref)
    gates(proj(2), lb_ref[:, HG_WIDTH:2 * HG_WIDTH], lfb_ref, kb_ref)
    vh_ref[0] = proj(3).astype(BF16)
    p = proj(4)
    sg_ref[0] = (p * _sigmoid(p)).astype(sg_ref.dtype)

    cos, sin_signed = cos_ref[...], sin_ref[...]
    q_a = (_rope(proj(5), cos, sin_signed) * (ATT_HDIM ** -0.5 * LOG2E)).astype(BF16)
    k_a = _rope(proj(6), cos, sin_signed).astype(BF16)
    v_a = proj(7).astype(BF16)
    for t, refs in ((q_a, (q1_ref, q4_ref, q16_ref)), (k_a, (k1_ref, k4_ref, k16_ref)),
                    (v_a, (v1_ref, v4_ref, v16_ref))):
        for (_, r), perm_ref, out_ref in zip(DILATION_PATTERNS, (None, p4_ref, p16_ref), refs):
            _store_dilated(t, perm_ref, out_ref, r)
    ga_ref[0, :, 0:PROJ_BLOCK] = _sigmoid(proj8(8)).astype(ga_ref.dtype)
    ga_ref[0, :, PROJ_BLOCK:2 * PROJ_BLOCK] = _sigmoid(proj8(9)).astype(ga_ref.dtype)
    gb_ref[0, :, 0:PROJ_BLOCK] = _sigmoid(proj8(10)).astype(gb_ref.dtype)
    gb_ref[0, :, PROJ_BLOCK:2 * PROJ_BLOCK] = _sigmoid(proj8(11)).astype(gb_ref.dtype)


def _dilation_perm(tm, r):
    i = np.arange(tm)
    src = r * (i % (tm // r)) + i // (tm // r)
    return jnp.asarray(src[:, None] == np.arange(tm)[None, :], dtype=BF16)


def _inproj_call(x, mod, g1, w_in, w_in8, w_in8_unscale, b_in, lb, cos, sin_signed, tm):
    B, L, D = x.shape
    n_in = w_in.shape[1] + w_in8.shape[1]
    assert n_in == N_PROJ_BLOCKS * PROJ_BLOCK and L % tm == 0 and tm % PERM_ROWS == 0
    row = lambda w, r=1: pl.BlockSpec((1, tm // r, r * w), lambda b, i: (b, i, 0))
    tab = pl.BlockSpec((tm, cos.shape[1]), lambda b, i: (i, 0))
    shp = lambda w, dt, r=1: jax.ShapeDtypeStruct((B, L // r, r * w), dt)
    perms = [_dilation_perm(PERM_ROWS, r) for (_, r) in DILATION_PATTERNS[1:]]
    att_specs = [row(ATT_WIDTH, r) for (_, r) in DILATION_PATTERNS for _ in range(3)]
    att_shapes = [shp(ATT_WIDTH, BF16, r) for (_, r) in DILATION_PATTERNS for _ in range(3)]
    return pl.pallas_call(
        _inproj_body,
        grid=(B, L // tm),
        in_specs=[row(D),
                  pl.BlockSpec((1,) + mod.shape[1:], lambda b, i: (b, 0, 0)),
                  _resident(g1.shape), _resident(w_in.shape), _resident(w_in8.shape), _resident(w_in8_unscale.shape),
                  _resident(b_in.shape),
                  _resident(lb.shape),
                  tab, tab] + [_resident(p.shape) for p in perms],
        out_specs=[row(512)] * 7 + [row(1024)] * 2 + att_specs,
        out_shape=[shp(512, F32), shp(512, F32), shp(512, F32), shp(512, F32), shp(512, F32),
                   shp(512, BF16), shp(512, BF16), shp(1024, BF16), shp(1024, BF16)] + att_shapes,
        compiler_params=pltpu.CompilerParams(
            dimension_semantics=("parallel", "parallel"), vmem_limit_bytes=VMEM_LIMIT),
        name="inproj",
    )(x, mod, g1, w_in, w_in8, w_in8_unscale, b_in, lb, cos, sin_signed, *perms)


def _bcast_row(c, group, r):
    T, w = c.shape
    c3 = c.reshape(T // group, group, w)
    return jnp.broadcast_to(c3[:, r:r + 1, :], c3.shape).reshape(T, w)


def _split2(g):
    g1 = g.astype(BF16)
    return g1, (g - g1.astype(F32)).astype(BF16)


def _hgrn_dir(q, k, v_bf, g, s_in, rev, o_ref, cols):
    T, dk = q.shape
    C, nch, nsb = HG_CHUNK, T // HG_CHUNK, HG_CHUNK // HG_SUB
    t = lax.broadcasted_iota(jnp.int32, (C, C), 0)
    s = lax.broadcasted_iota(jnp.int32, (C, C), 1)
    order = (s >= t) if rev else (s <= t)
    inner_mask = order & ((t >> HG_SUB_SHIFT) == (s >> HG_SUB_SHIFT))
    tri = order.astype(BF16)

    g_terms = jnp.concatenate(_split2(g), axis=1)
    c_parts = []
    for ci in range(nch):
        r = _dot(tri, g_terms[ci * C:(ci + 1) * C])
        c_parts.append(r[:, 0:dk] + r[:, dk:2 * dk])
    c = jnp.concatenate(c_parts, axis=0)
    yield

    far = 0 if rev else HG_SUB - 1
    edge = _bcast_row(c, HG_SUB, far)
    mid = _bcast_row(c, HG_SUB, HG_SUB // 2)
    c_end = _bcast_row(c, C, 0 if rev else C - 1)
    rowc = lax.broadcasted_iota(jnp.int32, (T, dk), 0) & (C - 1)

    k_edge = k * jnp.exp2(edge - c)
    q_cat, k_cat = [], []
    for J in (range(1, nsb) if rev else range(nsb - 1)):
        ref_j = _bcast_row(c, C, J * HG_SUB + far)
        queries = (rowc < J * HG_SUB) if rev else (rowc >= (J + 1) * HG_SUB)
        q_cat.append((q * jnp.exp2(jnp.where(queries, c - ref_j, EXP2_ZERO))).astype(BF16))
        in_j = (rowc >= J * HG_SUB) & (rowc < (J + 1) * HG_SUB)
        k_cat.append(jnp.where(in_j, k_edge, 0.0).astype(BF16))
    q_cat = jnp.concatenate(q_cat, axis=1)
    k_cat = jnp.concatenate(k_cat, axis=1)

    d_mid = jnp.clip(c - mid, -HG_EXP2_CLAMP, HG_EXP2_CLAMP)
    q_mid = (q * jnp.exp2(d_mid)).astype(BF16)
    k_mid = (k * jnp.exp2(-d_mid)).astype(BF16)
    q_dec = (q * jnp.exp2(c)).astype(BF16)
    k_end = (k * jnp.exp2(c_end - c)).astype(BF16)
    decay = jnp.exp2(c_end)
    yield

    o_intra, upd = [], []
    for ci in range(nch):
        rows = slice(ci * C, (ci + 1) * C)
        sc = _dot_nt(q_cat[rows], k_cat[rows])
        sc = sc + jnp.where(inner_mask, _dot_nt(q_mid[rows], k_mid[rows]), 0.0)
        o_intra.append(_dot(sc.astype(BF16), v_bf[rows]))
        upd.append(_dot_tn(v_bf[rows], k_end[rows]))
        yield

    s_t = s_in
    for ci in (reversed(range(nch)) if rev else range(nch)):
        rows = slice(ci * C, (ci + 1) * C)
        o_ref[0, rows, cols] = (o_intra[ci] + _dot_nt(q_dec[rows], s_t.astype(BF16))).astype(o_ref.dtype)
        s_t = s_t * decay[ci * C:ci * C + 1] + upd[ci]
        yield
    return s_t


def _hgrn_body(qf_ref, lff_ref, kf_ref, vf_ref, qb_ref, lfb_ref, kb_ref, vb_ref,
               of_ref, ob_ref, sf_ref, sb_ref):
    @pl.when(pl.program_id(2) == 0)
    def _():
        sf_ref[...] = jnp.zeros_like(sf_ref)
        sb_ref[...] = jnp.zeros_like(sb_ref)

    runs = []
    for h in range(HG_STEP_HEADS):
        cols = slice(h * HG_DIM, (h + 1) * HG_DIM)
        runs.append((_hgrn_dir(qf_ref[0, :, cols], kf_ref[0, :, cols], vf_ref[0, :, cols], lff_ref[0, :, cols],
                               sf_ref[h], False, of_ref, cols), sf_ref, h))
        runs.append((_hgrn_dir(qb_ref[0, :, cols], kb_ref[0, :, cols], vb_ref[0, :, cols], lfb_ref[0, :, cols],
                               sb_ref[h], True, ob_ref, cols), sb_ref, h))
    while runs:
        for run in list(runs):
            gen, state_ref, h = run
            try:
                next(gen)
            except StopIteration as done:
                state_ref[h] = done.value
                runs.remove(run)


def _hgrn_call(q, lf_f, k_f, lf_b, k_b, v, T):
    B, L, W = q.shape
    nb = L // T
    assert L % T == 0 and T % HG_CHUNK == 0 and W == HG_WIDTH
    width = HG_STEP_HEADS * HG_DIM
    fwd = pl.BlockSpec((1, T, width), lambda b, h, j: (b, j, h))
    bwd = pl.BlockSpec((1, T, width), lambda b, h, j: (b, nb - 1 - j, h))
    return pl.pallas_call(
        _hgrn_body,
        grid=(B, HG_HEADS // HG_STEP_HEADS, nb),
        in_specs=[fwd, fwd, fwd, fwd, bwd, bwd, bwd, bwd],
        out_specs=[fwd, bwd],
        out_shape=[jax.ShapeDtypeStruct((B, L, W), BF16)] * 2,
        scratch_shapes=[pltpu.VMEM((HG_STEP_HEADS, HG_DIM, HG_DIM), F32)] * 2,
        compiler_params=pltpu.CompilerParams(
            dimension_semantics=("parallel", "parallel", "arbitrary"), vmem_limit_bytes=VMEM_LIMIT),
        name="hgrn_scan",
    )(q, lf_f, k_f, v, q, lf_b, k_b, v)


ATT_QBLK = 2 * ATT_RADIUS
ATT_STEP_BLOCKS = 8


def _attn_body(q_ref, kp_ref, kc_ref, kn_ref, vp_ref, vc_ref, vn_ref, o_ref, lse_ref):
    n = pl.program_id(2)
    last = pl.num_programs(2) - 1
    kcat = jnp.concatenate([kp_ref[0], kc_ref[0], kn_ref[0]], axis=0)
    vcat = jnp.concatenate([vp_ref[0], vc_ref[0], vn_ref[0]], axis=0)
    nq, nk = ATT_QBLK, ATT_QBLK + 2 * ATT_RADIUS
    nblk = q_ref.shape[1] // nq
    i = lax.broadcasted_iota(jnp.int32, (nq, nk), 0)
    j = lax.broadcasted_iota(jnp.int32, (nq, nk), 1)
    band = (j >= i) & (j <= i + 2 * ATT_RADIUS)
    has_prev = (j >= ATT_RADIUS) | (n > 0)
    has_next = (j < nq + ATT_RADIUS) | (n < last)

    pair_w = 2 * ATT_HDIM
    npairs = ATT_HEADS // 2
    lane = lax.broadcasted_iota(jnp.int32, kcat.shape, 1) & (pair_w - 1)
    mask_a = jnp.where(lane < ATT_HDIM, 1.0, 0.0).astype(BF16)
    mask_b = jnp.where(lane < ATT_HDIM, 0.0, 1.0).astype(BF16)
    k_a, k_b, v_a, v_b = kcat * mask_a, kcat * mask_b, vcat * mask_a, vcat * mask_b
    lo_k = lax.broadcasted_iota(jnp.int32, (nk, pair_w), 1) < ATT_HDIM
    ones_a = jnp.where(lo_k, 1.0, 0.0).astype(BF16)
    ones_b = jnp.where(lo_k, 0.0, 1.0).astype(BF16)
    lo_q = lax.broadcasted_iota(jnp.int32, (nq, pair_w), 1) < ATT_HDIM

    for blk in range(nblk):
        valid = band
        if blk == 0:
            valid = valid & has_prev
        if blk == nblk - 1:
            valid = valid & has_next
        qrows = slice(blk * nq, (blk + 1) * nq)
        krows = slice(blk * nq, blk * nq + nk)
        scores = []
        for p in range(npairs):
            cols = slice(p * pair_w, (p + 1) * pair_w)
            q = q_ref[0, qrows, cols]
            scores.append(_dot_nt(q, k_a[krows, cols]))
            scores.append(_dot_nt(q, k_b[krows, cols]))
        scores = [jnp.where(valid, s, -jnp.inf) for s in scores]
        tops = [jnp.max(s, axis=-1, keepdims=True) for s in scores]
        probs = [jnp.exp2(s - m).astype(BF16) for s, m in zip(scores, tops)]
        for p in range(npairs):
            cols = slice(p * pair_w, (p + 1) * pair_w)
            res = (_dot(probs[2 * p], jnp.concatenate([v_a[krows, cols], ones_a], axis=1))
                   + _dot(probs[2 * p + 1], jnp.concatenate([v_b[krows, cols], ones_b], axis=1)))
            num, den = res[:, 0:pair_w], res[:, pair_w:2 * pair_w]
            o_ref[0, qrows, cols] = (num / den).astype(o_ref.dtype)
            lse_ref[0, qrows, cols] = jnp.where(lo_q, tops[2 * p], tops[2 * p + 1]) + jnp.log2(den)


def _attn_call(q, k, v, r):
    B, Lr, rW = q.shape
    W = rW // r
    step = ATT_QBLK * min(ATT_STEP_BLOCKS, Lr // ATT_QBLK)
    assert W == ATT_WIDTH and Lr % step == 0
    per_step = step // ATT_RADIUS
    nhalo = Lr // ATT_RADIUS
    center = pl.BlockSpec((1, step, W), lambda b, c, n: (b, n, c))
    prev = pl.BlockSpec((1, ATT_RADIUS, W), lambda b, c, n: (b, jnp.maximum(per_step * n - 1, 0), c))
    nxt = pl.BlockSpec((1, ATT_RADIUS, W), lambda b, c, n: (b, jnp.minimum(per_step * (n + 1), nhalo - 1), c))
    return pl.pallas_call(
        _attn_body,
        grid=(B, r, Lr // step),
        in_specs=[center, prev, center, nxt, prev, center, nxt],
        out_specs=[center, center],
        out_shape=[jax.ShapeDtypeStruct((B, Lr, rW), BF16), jax.ShapeDtypeStruct((B, Lr, rW), F32)],
        compiler_params=pltpu.CompilerParams(
            dimension_semantics=("parallel", "parallel", "parallel"), vmem_limit_bytes=VMEM_LIMIT),
        name=f"dilated_attn_r{r}",
    )(q, k, k, k, v, v, v)


FFN_CHUNK = 256


def _load_natural(a_ref, l_ref, perm_t_ref, r):
    if r == 1:
        return a_ref[0].astype(F32), l_ref[0]
    w = a_ref.shape[2] // r
    rows = PERM_ROWS // r
    a_nat, l_nat = [], []
    for sub in range(a_ref.shape[1] // rows):
        blk = slice(sub * rows, (sub + 1) * rows)
        a = jnp.concatenate([a_ref[0, blk, c * w:(c + 1) * w] for c in range(r)], axis=0)
        l = jnp.concatenate([l_ref[0, blk, c * w:(c + 1) * w] for c in range(r)], axis=0)
        l_hi = l.astype(BF16)
        l_lo = (l - l_hi.astype(F32)).astype(BF16)
        nat = _dot(perm_t_ref[...], jnp.concatenate([a, l_hi, l_lo], axis=1))
        a_nat.append(nat[:, 0:w])
        l_nat.append(nat[:, w:2 * w] + nat[:, 2 * w:3 * w])
    return jnp.concatenate(a_nat, axis=0), jnp.concatenate(l_nat, axis=0)


def _merge_body(x_ref, mod_ref, of_ref, ob_ref, sg_ref, a1_ref, l1_ref, a2_ref, l2_ref, a3_ref, l3_ref,
                ga_ref, gb_ref, gn_ref, n2_ref, fin_ref, wa_ref, wb_ref, wo_ref, wfi_ref, wfo_ref,
                p4t_ref, p16t_ref, y_ref, act_ref):
    mod = mod_ref[0]
    gate1, shift2, scale2, gate2 = mod[2:3], mod[3:4], mod[4:5], mod[5:6]

    o = of_ref[0].astype(F32) + ob_ref[0].astype(F32)
    o_a = jnp.concatenate([_rms(o[:, h * HG_DIM:(h + 1) * HG_DIM]) for h in range(HG_HEADS)], axis=1)
    o_a = o_a * gn_ref[...] * sg_ref[0]

    (a1, l1), (a2, l2), (a3, l3) = [
        _load_natural(a_ref, l_ref, perm_ref, r)
        for (_, r), a_ref, l_ref, perm_ref in zip(DILATION_PATTERNS, (a1_ref, a2_ref, a3_ref),
                                                  (l1_ref, l2_ref, l3_ref), (None, p4t_ref, p16t_ref))]
    top = jnp.maximum(jnp.maximum(l1, l2), l3)
    w1, w2, w3 = jnp.exp2(l1 - top), jnp.exp2(l2 - top), jnp.exp2(l3 - top)
    o_b = (a1 * w1 + a2 * w2 + a3 * w3) / (w1 + w2 + w3)

    merged = (ga_ref[0] * _dot(o_a.astype(BF16), wa_ref[...])
              + gb_ref[0] * _dot(o_b.astype(BF16), wb_ref[...]))
    x1 = x_ref[0] + gate1 * _dot(merged.astype(BF16), wo_ref[...])

    h2 = (_rms(x1) * n2_ref[...] * (1.0 + scale2) + shift2).astype(BF16)
    hidden = wfo_ref.shape[0]
    for c0 in range(0, hidden, FFN_CHUNK):
        gt = _dot(h2, wfi_ref[:, c0:c0 + FFN_CHUNK])
        up = _dot(h2, wfi_ref[:, hidden + c0:hidden + c0 + FFN_CHUNK])
        act_ref[:, c0:c0 + FFN_CHUNK] = (gt * _sigmoid(gt) * up).astype(BF16)
    x2 = x1 + gate2 * _dot(act_ref[...], wfo_ref[...])
    y_ref[0] = _rms(x2) * fin_ref[...]


def _merge_call(x, mod, o_f, o_b, sg, att, ga, gb, gn, n2, fin, wa, wb, wo, wfi, wfo, tm):
    B, L, D = x.shape
    hidden = wfo.shape[0]
    assert L % tm == 0 and tm % PERM_ROWS == 0 and hidden % FFN_CHUNK == 0 and wfi.shape[1] == 2 * hidden
    row = lambda w, r=1: pl.BlockSpec((1, tm // r, r * w), lambda b, i: (b, i, 0))
    att_flat = [t for pair in att for t in pair]
    att_specs = [row(ATT_WIDTH, r) for (_, r) in DILATION_PATTERNS for _ in range(2)]
    perms_t = [_dilation_perm(PERM_ROWS, r).T for (_, r) in DILATION_PATTERNS[1:]]
    return pl.pallas_call(
        _merge_body,
        grid=(B, L // tm),
        in_specs=[row(D), pl.BlockSpec((1,) + mod.shape[1:], lambda b, i: (b, 0, 0))]
                 + [row(512)] * 3 + att_specs + [row(D), row(D)]
                 + [_resident(t.shape) for t in (gn, n2, fin, wa, wb, wo, wfi, wfo, *perms_t)],
        out_specs=row(D),
        out_shape=jax.ShapeDtypeStruct((B, L, D), F32),
        scratch_shapes=[pltpu.VMEM((tm, hidden), BF16)],
        compiler_params=pltpu.CompilerParams(
            dimension_semantics=("parallel", "parallel"), vmem_limit_bytes=VMEM_LIMIT),
        name="merge_ffn",
    )(x, mod, o_f, o_b, sg, *att_flat, ga, gb, gn, n2, fin, wa, wb, wo, wfi, wfo, *perms_t)


def _rope_tables(L):
    half = ATT_HDIM // 2
    lane = np.arange(2 * ATT_HDIM)
    inv = ROPE_THETA ** (-(lane % half).astype(np.float64) / half)
    sign = np.where(lane % ATT_HDIM < half, -1.0, 1.0)
    ang = np.arange(L, dtype=np.float64)[:, None] * inv[None, :]
    return jnp.asarray(np.cos(ang), dtype=F32), jnp.asarray(np.sin(ang) * sign[None, :], dtype=F32)


def _trunk(x, mod, lb, p, tm_in, tm_out, t_scan):
    B, L, D = x.shape
    cos, sin_signed = _rope_tables(L)
    (q_h, lf_f, k_f, lf_b, k_b, v_h, sg, ga, gb, *qkv) = _inproj_call(
        x, mod, p["g1"], p["w_in"], p["w_in8"], p["w_in8_unscale"], p["b_in"], lb, cos, sin_signed, tm_in)
    o_f, o_b = _hgrn_call(q_h, lf_f, k_f, lf_b, k_b, v_h, t_scan)
    att = [_attn_call(*qkv[3 * i:3 * i + 3], r) for i, (_, r) in enumerate(DILATION_PATTERNS)]
    return _merge_call(x, mod, o_f, o_b, sg, att, ga, gb, p["gn"], p["n2"], p["fin"],
                       p["wa"], p["wb"], p["wo"], p["wfi"], p["wfo"], tm_out)


def kernel(x_prompt, x_sample, c_prompt, c_sample, w_ada, b_ada, norm1_g, w_in, b_in, lb_logits, hg_norm_g, w_branch_a, w_branch_b, w_out, norm2_g, w_ffn_in, w_ffn_out, final_norm_g):
    assert w_ada.shape[0] == 1 and lb_logits.shape[0] == 2, "single-layer trunk"
    D = x_prompt.shape[-1]
    bp, bs = c_prompt.shape[0], c_sample.shape[0]
    c_all = jnp.concatenate([c_prompt, c_sample], axis=0)
    mod = _mod_call(c_all, w_ada[0], b_ada[0])
    mod = mod.reshape(mod.shape[0], 6, D)
    lb = _lb_call(lb_logits)
    row = lambda v: v.reshape(1, -1).astype(F32)
    n16 = N_BF16_BLOCKS * PROJ_BLOCK
    w_gate = w_in[0, :, n16:]
    col_top = jnp.max(jnp.abs(w_gate), axis=0, keepdims=True)
    w_unscale = jnp.where(col_top > 0.0, col_top, 1.0) * (1.0 / FP8_TOP)
    p = dict(g1=row(norm1_g[0]), w_in=w_in[0, :, :n16].astype(BF16), w_in8=(w_gate / w_unscale).astype(FP8),
             w_in8_unscale=w_unscale, b_in=row(b_in[0]), gn=row(hg_norm_g[0]),
             n2=row(norm2_g[0]), fin=row(final_norm_g), wa=w_branch_a[0].astype(BF16),
             wb=w_branch_b[0].astype(BF16), wo=w_out[0].astype(BF16), wfi=w_ffn_in[0].astype(BF16),
             wfo=w_ffn_out[0].astype(BF16))
    y_prompt = _trunk(x_prompt, mod[:bp], lb, p, 512, 512, 2048)
    y_sample = _trunk(x_sample, mod[bp:bp + bs], lb, p, 512, 512, 2048)
    return (y_prompt, y_sample)
```

```python
import numpy as np
import jax
import jax.numpy as jnp
from jax import lax
from jax.experimental import pallas as pl
from jax.experimental.pallas import tpu as pltpu

F32 = jnp.float32
BF16 = jnp.bfloat16
LANES = 128
SUBLANES = 8

NORM_EPS = 1e-6
HG_HEADS = 4
HG_DIM = 128
HG_WIDTH = HG_HEADS * HG_DIM
ATT_HEADS = 8
ATT_HDIM = 64
ATT_WIDTH = ATT_HEADS * ATT_HDIM
DILATION_PATTERNS = ((128, 1), (512, 4), (2048, 16))
ATT_RADIUS = 64
ROPE_THETA = 10000.0
LOG2E = 1.4426950408889634
PROJ_BLOCK = 512
N_PROJ_BLOCKS = 12
N_BF16_BLOCKS = 8
FP8 = jnp.float8_e4m3fn
FP8_TOP = 240.0
PERM_ROWS = 256

HG_CHUNK = 64
HG_SUB = 16
HG_SUB_SHIFT = 4
HG_STEP_HEADS = 2
HG_SLOTS, HG_SLOT_FWD, HG_SLOT_BWD = 5, 1, 3
HG_EXP2_CLAMP = 110.0
EXP2_ZERO = -1e30
V7X_VMEM_BYTES = 64 * 1024 * 1024
VMEM_LIMIT = V7X_VMEM_BYTES - 8 * 1024 * 1024


def _dot(a, b):
    return jnp.dot(a, b, preferred_element_type=F32)


def _dot_nt(a, b):
    return lax.dot_general(a, b, (((1,), (1,)), ((), ())), preferred_element_type=F32)


def _dot_tn(a, b):
    return lax.dot_general(a, b, (((0,), (0,)), ((), ())), preferred_element_type=F32)


def _sigmoid(x):
    return 1.0 / (1.0 + jnp.exp(-x))


def _rms(x):
    return x * lax.rsqrt(jnp.mean(x * x, axis=-1, keepdims=True) + NORM_EPS)


def _resident(shape):
    nd = len(shape)
    return pl.BlockSpec(shape, lambda *_: (0,) * nd, pipeline_mode=pl.Buffered(1))


def _mod_body(c_ref, w_ref, b_ref, o_ref):
    nrows, bn = c_ref.shape[0], w_ref.shape[1]
    outs = []
    for r in range(nrows):
        c = c_ref[r]
        a = c * _sigmoid(c)
        cols = [jnp.sum(a * w_ref[:, j * LANES:(j + 1) * LANES], axis=0, keepdims=True)
                for j in range(bn // LANES)]
        outs.append(jnp.concatenate(cols, axis=1) + b_ref[...])
    outs.append(jnp.zeros((o_ref.shape[0] - nrows, bn), F32))
    o_ref[...] = jnp.concatenate(outs, axis=0)


def _mod_call(c, w_ada, b_ada):
    nrows, d = c.shape
    n = w_ada.shape[1]
    bn = 1024
    rows_out = -(-nrows // SUBLANES) * SUBLANES
    c_rep = jnp.broadcast_to(c[:, :, None], (nrows, d, LANES))
    return pl.pallas_call(
        _mod_body,
        grid=(n // bn,),
        in_specs=[pl.BlockSpec((nrows, d, LANES), lambda j: (0, 0, 0)),
                  pl.BlockSpec((d, bn), lambda j: (0, j)),
                  pl.BlockSpec((1, bn), lambda j: (0, j))],
        out_specs=pl.BlockSpec((rows_out, bn), lambda j: (0, j)),
        out_shape=jax.ShapeDtypeStruct((rows_out, n), F32),
        name="adaln_mod",
    )(c_rep, w_ada, b_ada.reshape(1, n))


def _lb_body(l_ref, o_ref):
    l = l_ref[...]
    e = jnp.exp(l - jnp.max(l, axis=0, keepdims=True))
    o_ref[...] = e[0:1] / jnp.sum(e, axis=0, keepdims=True)


def _lb_call(lb_logits):
    n = lb_logits.shape[0]
    flat = lb_logits.reshape(n, -1).astype(F32)
    return pl.pallas_call(
        _lb_body,
        out_shape=jax.ShapeDtypeStruct((1, flat.shape[1]), F32),
        name="hgrn_lower_bounds",
    )(flat)


def _rope(p, cos, sin_signed):
    n = p.shape[-1]
    lane = lax.broadcasted_iota(jnp.int32, p.shape, 1)
    first_half = (lane & 63) < 32
    partner = jnp.where(first_half, pltpu.roll(p, n - 32, 1), pltpu.roll(p, 32, 1))
    reps = n // cos.shape[-1]
    return p * jnp.tile(cos, (1, reps)) + partner * jnp.tile(sin_signed, (1, reps))


def _store_dilated(t_bf, perm_ref, out_ref, r):
    if r == 1:
        out_ref[0] = t_bf
        return
    tm, w = t_bf.shape
    rows = PERM_ROWS // r
    for sub in range(tm // PERM_ROWS):
        grouped = _dot(perm_ref[...], t_bf[sub * PERM_ROWS:(sub + 1) * PERM_ROWS]).astype(BF16)
        for c in range(r):
            out_ref[0, sub * rows:(sub + 1) * rows, c * w:(c + 1) * w] = grouped[c * rows:(c + 1) * rows]


def _inproj_body(x_ref, mod_ref, g1_ref, w_ref, w8_ref, w8s_ref, b_ref, lb_ref, cos_ref, sin_ref, p4_ref, p16_ref,
                 hg_ref, vh_ref, sg_ref, gate_ref,
                 q1_ref, k1_ref, v1_ref, q4_ref, k4_ref, v4_ref, q16_ref, k16_ref, v16_ref):
    x = x_ref[0]
    mod = mod_ref[0]
    shift1, scale1 = mod[0:1], mod[1:2]
    h = _rms(x) * g1_ref[...]
    h = h * (1.0 + scale1) + shift1
    hb = h.astype(BF16)
    h_top = jnp.max(jnp.abs(h), axis=-1, keepdims=True)
    h_top = jnp.where(h_top > 0.0, h_top, 1.0)
    h_unscale = h_top * (1.0 / FP8_TOP)
    hb8 = (h * (FP8_TOP / h_top)).astype(FP8)

    def proj8(j):
        cols = slice(j * PROJ_BLOCK, (j + 1) * PROJ_BLOCK)
        cols8 = slice((j - N_BF16_BLOCKS) * PROJ_BLOCK, (j - N_BF16_BLOCKS + 1) * PROJ_BLOCK)
        return _dot(hb8, w8_ref[:, cols8]) * h_unscale * w8s_ref[:, cols8] + b_ref[:, cols]

    def proj(j):
        cols = slice(j * PROJ_BLOCK, (j + 1) * PROJ_BLOCK)
        return _dot(hb, w_ref[:, cols]) + b_ref[:, cols]

    p = proj(0)
    hg_ref[0, :, 0:HG_WIDTH] = p * _sigmoid(p) * (HG_DIM ** -0.5)

    def gates(p, lb, slot):
        f = lb + (1.0 - lb) * _sigmoid(p)
        hg_ref[0, :, slot * HG_WIDTH:(slot + 1) * HG_WIDTH] = jnp.log2(f)
        hg_ref[0, :, (slot + 1) * HG_WIDTH:(slot + 2) * HG_WIDTH] = 1.0 - f

    gates(proj(1), lb_ref[:, 0:HG_WIDTH], HG_SLOT_FWD)
    gates(proj(2), lb_ref[:, HG_WIDTH:2 * HG_WIDTH], HG_SLOT_BWD)
    vh_ref[0] = proj(3).astype(BF16)
    p = proj(4)
    sg_ref[0] = (p * _sigmoid(p)).astype(sg_ref.dtype)

    cos, sin_signed = cos_ref[...], sin_ref[...]
    q_a = (_rope(proj(5), cos, sin_signed) * (ATT_HDIM ** -0.5 * LOG2E)).astype(BF16)
    k_a = _rope(proj(6), cos, sin_signed).astype(BF16)
    v_a = proj(7).astype(BF16)
    for t, refs in ((q_a, (q1_ref, q4_ref, q16_ref)), (k_a, (k1_ref, k4_ref, k16_ref)),
                    (v_a, (v1_ref, v4_ref, v16_ref))):
        for (_, r), perm_ref, out_ref in zip(DILATION_PATTERNS, (None, p4_ref, p16_ref), refs):
            _store_dilated(t, perm_ref, out_ref, r)
    for j in range(N_BF16_BLOCKS, N_PROJ_BLOCKS):
        cols = slice((j - N_BF16_BLOCKS) * PROJ_BLOCK, (j - N_BF16_BLOCKS + 1) * PROJ_BLOCK)
        gate_ref[0, :, cols] = _sigmoid(proj8(j)).astype(gate_ref.dtype)


def _dilation_perm(tm, r):
    i = np.arange(tm)
    src = r * (i % (tm // r)) + i // (tm // r)
    return jnp.asarray(src[:, None] == np.arange(tm)[None, :], dtype=BF16)


def _inproj_call(x, mod, g1, w_in, w_in8, w_in8_unscale, b_in, lb, cos, sin_signed, tm):
    B, L, D = x.shape
    n_in = w_in.shape[1] + w_in8.shape[1]
    assert n_in == N_PROJ_BLOCKS * PROJ_BLOCK and L % tm == 0 and tm % PERM_ROWS == 0
    row = lambda w, r=1: pl.BlockSpec((1, tm // r, r * w), lambda b, i: (b, i, 0))
    tab = pl.BlockSpec((tm, cos.shape[1]), lambda b, i: (i, 0))
    shp = lambda w, dt, r=1: jax.ShapeDtypeStruct((B, L // r, r * w), dt)
    perms = [_dilation_perm(PERM_ROWS, r) for (_, r) in DILATION_PATTERNS[1:]]
    att_specs = [row(ATT_WIDTH, r) for (_, r) in DILATION_PATTERNS for _ in range(3)]
    att_shapes = [shp(ATT_WIDTH, BF16, r) for (_, r) in DILATION_PATTERNS for _ in range(3)]
    return pl.pallas_call(
        _inproj_body,
        grid=(B, L // tm),
        in_specs=[row(D),
                  pl.BlockSpec((1,) + mod.shape[1:], lambda b, i: (b, 0, 0)),
                  _resident(g1.shape), _resident(w_in.shape), _resident(w_in8.shape), _resident(w_in8_unscale.shape),
                  _resident(b_in.shape),
                  _resident(lb.shape),
                  tab, tab] + [_resident(p.shape) for p in perms],
        out_specs=[row(HG_SLOTS * HG_WIDTH)] + [row(HG_WIDTH)] * 2 + [row(2 * D)] + att_specs,
        out_shape=[shp(HG_SLOTS * HG_WIDTH, F32)] + [shp(HG_WIDTH, BF16)] * 2 + [shp(2 * D, BF16)] + att_shapes,
        compiler_params=pltpu.CompilerParams(
            dimension_semantics=("parallel", "parallel"), vmem_limit_bytes=VMEM_LIMIT),
        name="inproj",
    )(x, mod, g1, w_in, w_in8, w_in8_unscale, b_in, lb, cos, sin_signed, *perms)


def _bcast_row(c, group, r):
    T, w = c.shape
    c3 = c.reshape(T // group, group, w)
    return jnp.broadcast_to(c3[:, r:r + 1, :], c3.shape).reshape(T, w)


def _split2(g):
    g1 = g.astype(BF16)
    return g1, (g - g1.astype(F32)).astype(BF16)


def _hgrn_dir(q, k, v_bf, g, s_in, rev, o_ref, cols):
    T, dk = q.shape
    C, nch, nsb = HG_CHUNK, T // HG_CHUNK, HG_CHUNK // HG_SUB
    t = lax.broadcasted_iota(jnp.int32, (C, C), 0)
    s = lax.broadcasted_iota(jnp.int32, (C, C), 1)
    order = (s >= t) if rev else (s <= t)
    inner_mask = order & ((t >> HG_SUB_SHIFT) == (s >> HG_SUB_SHIFT))
    tri = order.astype(BF16)

    g_terms = jnp.concatenate(_split2(g), axis=1)
    c_parts = []
    for ci in range(nch):
        r = _dot(tri, g_terms[ci * C:(ci + 1) * C])
        c_parts.append(r[:, 0:dk] + r[:, dk:2 * dk])
    c = jnp.concatenate(c_parts, axis=0)
    yield

    far = 0 if rev else HG_SUB - 1
    edge = _bcast_row(c, HG_SUB, far)
    mid = _bcast_row(c, HG_SUB, HG_SUB // 2)
    c_end = _bcast_row(c, C, 0 if rev else C - 1)
    rowc = lax.broadcasted_iota(jnp.int32, (T, dk), 0) & (C - 1)

    k_edge = k * jnp.exp2(edge - c)
    q_cat, k_cat = [], []
    for J in (range(1, nsb) if rev else range(nsb - 1)):
        ref_j = _bcast_row(c, C, J * HG_SUB + far)
        queries = (rowc < J * HG_SUB) if rev else (rowc >= (J + 1) * HG_SUB)
        q_cat.append((q * jnp.exp2(jnp.where(queries, c - ref_j, EXP2_ZERO))).astype(BF16))
        in_j = (rowc >= J * HG_SUB) & (rowc < (J + 1) * HG_SUB)
        k_cat.append(jnp.where(in_j, k_edge, 0.0).astype(BF16))
    q_cat = jnp.concatenate(q_cat, axis=1)
    k_cat = jnp.concatenate(k_cat, axis=1)

    d_mid = jnp.clip(c - mid, -HG_EXP2_CLAMP, HG_EXP2_CLAMP)
    q_mid = (q * jnp.exp2(d_mid)).astype(BF16)
    k_mid = (k * jnp.exp2(-d_mid)).astype(BF16)
    q_dec = (q * jnp.exp2(c)).astype(BF16)
    k_end = (k * jnp.exp2(c_end - c)).astype(BF16)
    decay = jnp.exp2(c_end)
    yield

    o_intra, upd = [], []
    for ci in range(nch):
        rows = slice(ci * C, (ci + 1) * C)
        sc = _dot_nt(q_cat[rows], k_cat[rows])
        sc = sc + jnp.where(inner_mask, _dot_nt(q_mid[rows], k_mid[rows]), 0.0)
        o_intra.append(_dot(sc.astype(BF16), v_bf[rows]))
        upd.append(_dot_tn(v_bf[rows], k_end[rows]))
        yield

    s_t = s_in
    for ci in (reversed(range(nch)) if rev else range(nch)):
        rows = slice(ci * C, (ci + 1) * C)
        o_ref[0, rows, cols] = (o_intra[ci] + _dot_nt(q_dec[rows], s_t.astype(BF16))).astype(o_ref.dtype)
        s_t = s_t * decay[ci * C:ci * C + 1] + upd[ci]
        yield
    return s_t


def _hgrn_body(qf_ref, lff_ref, kf_ref, vf_ref, qb_ref, lfb_ref, kb_ref, vb_ref,
               of_ref, ob_ref, sf_ref, sb_ref):
    @pl.when(pl.program_id(2) == 0)
    def _():
        sf_ref[...] = jnp.zeros_like(sf_ref)
        sb_ref[...] = jnp.zeros_like(sb_ref)

    runs = []
    for h in range(HG_STEP_HEADS):
        cols = slice(h * HG_DIM, (h + 1) * HG_DIM)
        runs.append((_hgrn_dir(qf_ref[0, :, cols], kf_ref[0, :, cols], vf_ref[0, :, cols], lff_ref[0, :, cols],
                               sf_ref[h], False, of_ref, cols), sf_ref, h))
        runs.append((_hgrn_dir(qb_ref[0, :, cols], kb_ref[0, :, cols], vb_ref[0, :, cols], lfb_ref[0, :, cols],
                               sb_ref[h], True, ob_ref, cols), sb_ref, h))
    while runs:
        for run in list(runs):
            gen, state_ref, h = run
            try:
                next(gen)
            except StopIteration as done:
                state_ref[h] = done.value
                runs.remove(run)


def _hgrn_call(hg, v, T):
    B, L, W = v.shape
    nb = L // T
    assert L % T == 0 and T % HG_CHUNK == 0 and W == HG_WIDTH and hg.shape[2] == HG_SLOTS * W
    width = HG_STEP_HEADS * HG_DIM
    per_slot = W // width
    fwd = pl.BlockSpec((1, T, width), lambda b, h, j: (b, j, h))
    bwd = pl.BlockSpec((1, T, width), lambda b, h, j: (b, nb - 1 - j, h))
    fwd_slot = lambda s: pl.BlockSpec((1, T, width), lambda b, h, j: (b, j, s * per_slot + h))
    bwd_slot = lambda s: pl.BlockSpec((1, T, width), lambda b, h, j: (b, nb - 1 - j, s * per_slot + h))
    return pl.pallas_call(
        _hgrn_body,
        grid=(B, HG_HEADS // HG_STEP_HEADS, nb),
        in_specs=[fwd_slot(0), fwd_slot(HG_SLOT_FWD), fwd_slot(HG_SLOT_FWD + 1), fwd,
                  bwd_slot(0), bwd_slot(HG_SLOT_BWD), bwd_slot(HG_SLOT_BWD + 1), bwd],
        out_specs=[fwd, bwd],
        out_shape=[jax.ShapeDtypeStruct((B, L, W), BF16)] * 2,
        scratch_shapes=[pltpu.VMEM((HG_STEP_HEADS, HG_DIM, HG_DIM), F32)] * 2,
        compiler_params=pltpu.CompilerParams(
            dimension_semantics=("parallel", "parallel", "arbitrary"), vmem_limit_bytes=VMEM_LIMIT),
        name="hgrn_scan",
    )(hg, hg, hg, v, hg, hg, hg, v)


ATT_QBLK = 2 * ATT_RADIUS
ATT_STEP_BLOCKS = 8


def _attn_body(q_ref, kp_ref, kc_ref, kn_ref, vp_ref, vc_ref, vn_ref, o_ref, lse_ref):
    n = pl.program_id(2)
    last = pl.num_programs(2) - 1
    kcat = jnp.concatenate([kp_ref[0], kc_ref[0], kn_ref[0]], axis=0)
    vcat = jnp.concatenate([vp_ref[0], vc_ref[0], vn_ref[0]], axis=0)
    nq, nk = ATT_QBLK, ATT_QBLK + 2 * ATT_RADIUS
    nblk = q_ref.shape[1] // nq
    i = lax.broadcasted_iota(jnp.int32, (nq, nk), 0)
    j = lax.broadcasted_iota(jnp.int32, (nq, nk), 1)
    band = (j >= i) & (j <= i + 2 * ATT_RADIUS)
    has_prev = (j >= ATT_RADIUS) | (n > 0)
    has_next = (j < nq + ATT_RADIUS) | (n < last)

    pair_w = 2 * ATT_HDIM
    npairs = ATT_HEADS // 2
    lane = lax.broadcasted_iota(jnp.int32, kcat.shape, 1) & (pair_w - 1)
    mask_a = jnp.where(lane < ATT_HDIM, 1.0, 0.0).astype(BF16)
    mask_b = jnp.where(lane < ATT_HDIM, 0.0, 1.0).astype(BF16)
    k_a, k_b, v_a, v_b = kcat * mask_a, kcat * mask_b, vcat * mask_a, vcat * mask_b
    lo_k = lax.broadcasted_iota(jnp.int32, (nk, pair_w), 1) < ATT_HDIM
    ones_a = jnp.where(lo_k, 1.0, 0.0).astype(BF16)
    ones_b = jnp.where(lo_k, 0.0, 1.0).astype(BF16)
    lo_q = lax.broadcasted_iota(jnp.int32, (nq, pair_w), 1) < ATT_HDIM

    for blk in range(nblk):
        valid = band
        if blk == 0:
            valid = valid & has_prev
        if blk == nblk - 1:
            valid = valid & has_next
        qrows = slice(blk * nq, (blk + 1) * nq)
        krows = slice(blk * nq, blk * nq + nk)
        scores = []
        for p in range(npairs):
            cols = slice(p * pair_w, (p + 1) * pair_w)
            q = q_ref[0, qrows, cols]
            scores.append(_dot_nt(q, k_a[krows, cols]))
            scores.append(_dot_nt(q, k_b[krows, cols]))
        scores = [jnp.where(valid, s, -jnp.inf) for s in scores]
        tops = [jnp.max(s, axis=-1, keepdims=True) for s in scores]
        probs = [jnp.exp2(s - m).astype(BF16) for s, m in zip(scores, tops)]
        for p in range(npairs):
            cols = slice(p * pair_w, (p + 1) * pair_w)
            res = (_dot(probs[2 * p], jnp.concatenate([v_a[krows, cols], ones_a], axis=1))
                   + _dot(probs[2 * p + 1], jnp.concatenate([v_b[krows, cols], ones_b], axis=1)))
            num, den = res[:, 0:pair_w], res[:, pair_w:2 * pair_w]
            o_ref[0, qrows, cols] = (num / den).astype(o_ref.dtype)
            lse_ref[0, qrows, cols] = jnp.where(lo_q, tops[2 * p], tops[2 * p + 1]) + jnp.log2(den)


def _attn_call(q, k, v, r):
    B, Lr, rW = q.shape
    W = rW // r
    step = ATT_QBLK * min(ATT_STEP_BLOCKS, Lr // ATT_QBLK)
    assert W == ATT_WIDTH and Lr % step == 0
    per_step = step // ATT_RADIUS
    nhalo = Lr // ATT_RADIUS
    center = pl.BlockSpec((1, step, W), lambda b, c, n: (b, n, c))
    prev = pl.BlockSpec((1, ATT_RADIUS, W), lambda b, c, n: (b, jnp.maximum(per_step * n - 1, 0), c))
    nxt = pl.BlockSpec((1, ATT_RADIUS, W), lambda b, c, n: (b, jnp.minimum(per_step * (n + 1), nhalo - 1), c))
    return pl.pallas_call(
        _attn_body,
        grid=(B, r, Lr // step),
        in_specs=[center, prev, center, nxt, prev, center, nxt],
        out_specs=[center, center],
        out_shape=[jax.ShapeDtypeStruct((B, Lr, rW), BF16), jax.ShapeDtypeStruct((B, Lr, rW), F32)],
        compiler_params=pltpu.CompilerParams(
            dimension_semantics=("parallel", "parallel", "parallel"), vmem_limit_bytes=VMEM_LIMIT),
        name=f"dilated_attn_r{r}",
    )(q, k, k, k, v, v, v)


FFN_CHUNK = 256


def _load_natural(a_ref, l_ref, perm_t_ref, r):
    if r == 1:
        return a_ref[0].astype(F32), l_ref[0]
    w = a_ref.shape[2] // r
    rows = PERM_ROWS // r
    a_nat, l_nat = [], []
    for sub in range(a_ref.shape[1] // rows):
        blk = slice(sub * rows, (sub + 1) * rows)
        a = jnp.concatenate([a_ref[0, blk, c * w:(c + 1) * w] for c in range(r)], axis=0)
        l = jnp.concatenate([l_ref[0, blk, c * w:(c + 1) * w] for c in range(r)], axis=0)
        l_hi = l.astype(BF16)
        l_lo = (l - l_hi.astype(F32)).astype(BF16)
        nat = _dot(perm_t_ref[...], jnp.concatenate([a, l_hi, l_lo], axis=1))
        a_nat.append(nat[:, 0:w])
        l_nat.append(nat[:, w:2 * w] + nat[:, 2 * w:3 * w])
    return jnp.concatenate(a_nat, axis=0), jnp.concatenate(l_nat, axis=0)


def _merge_body(x_ref, mod_ref, of_ref, ob_ref, sg_ref, a1_ref, l1_ref, a2_ref, l2_ref, a3_ref, l3_ref,
                gate_ref, gn_ref, n2_ref, fin_ref, wa_ref, wb_ref, wo_ref, wfi_ref, wfo_ref,
                p4t_ref, p16t_ref, y_ref, act_ref):
    mod = mod_ref[0]
    gate1, shift2, scale2, gate2 = mod[2:3], mod[3:4], mod[4:5], mod[5:6]

    o = of_ref[0].astype(F32) + ob_ref[0].astype(F32)
    o_a = jnp.concatenate([_rms(o[:, h * HG_DIM:(h + 1) * HG_DIM]) for h in range(HG_HEADS)], axis=1)
    o_a = o_a * gn_ref[...] * sg_ref[0]

    (a1, l1), (a2, l2), (a3, l3) = [
        _load_natural(a_ref, l_ref, perm_ref, r)
        for (_, r), a_ref, l_ref, perm_ref in zip(DILATION_PATTERNS, (a1_ref, a2_ref, a3_ref),
                                                  (l1_ref, l2_ref, l3_ref), (None, p4t_ref, p16t_ref))]
    top = jnp.maximum(jnp.maximum(l1, l2), l3)
    w1, w2, w3 = jnp.exp2(l1 - top), jnp.exp2(l2 - top), jnp.exp2(l3 - top)
    o_b = (a1 * w1 + a2 * w2 + a3 * w3) / (w1 + w2 + w3)

    d_model = x_ref.shape[2]
    merged = (gate_ref[0, :, 0:d_model] * _dot(o_a.astype(BF16), wa_ref[...])
              + gate_ref[0, :, d_model:2 * d_model] * _dot(o_b.astype(BF16), wb_ref[...]))
    x1 = x_ref[0] + gate1 * _dot(merged.astype(BF16), wo_ref[...])

    h2 = (_rms(x1) * n2_ref[...] * (1.0 + scale2) + shift2).astype(BF16)
    hidden = wfo_ref.shape[0]
    for c0 in range(0, hidden, FFN_CHUNK):
        gt = _dot(h2, wfi_ref[:, c0:c0 + FFN_CHUNK])
        up = _dot(h2, wfi_ref[:, hidden + c0:hidden + c0 + FFN_CHUNK])
        act_ref[:, c0:c0 + FFN_CHUNK] = (gt * _sigmoid(gt) * up).astype(BF16)
    x2 = x1 + gate2 * _dot(act_ref[...], wfo_ref[...])
    y_ref[0] = _rms(x2) * fin_ref[...]


def _merge_call(x, mod, o_f, o_b, sg, att, gates, gn, n2, fin, wa, wb, wo, wfi, wfo, tm):
    B, L, D = x.shape
    hidden = wfo.shape[0]
    assert L % tm == 0 and tm % PERM_ROWS == 0 and hidden % FFN_CHUNK == 0 and wfi.shape[1] == 2 * hidden
    row = lambda w, r=1: pl.BlockSpec((1, tm // r, r * w), lambda b, i: (b, i, 0))
    att_flat = [t for pair in att for t in pair]
    att_specs = [row(ATT_WIDTH, r) for (_, r) in DILATION_PATTERNS for _ in range(2)]
    perms_t = [_dilation_perm(PERM_ROWS, r).T for (_, r) in DILATION_PATTERNS[1:]]
    return pl.pallas_call(
        _merge_body,
        grid=(B, L // tm),
        in_specs=[row(D), pl.BlockSpec((1,) + mod.shape[1:], lambda b, i: (b, 0, 0))]
                 + [row(HG_WIDTH)] * 3 + att_specs + [row(2 * D)]
                 + [_resident(t.shape) for t in (gn, n2, fin, wa, wb, wo, wfi, wfo, *perms_t)],
        out_specs=row(D),
        out_shape=jax.ShapeDtypeStruct((B, L, D), F32),
        scratch_shapes=[pltpu.VMEM((tm, hidden), BF16)],
        compiler_params=pltpu.CompilerParams(
            dimension_semantics=("parallel", "parallel"), vmem_limit_bytes=VMEM_LIMIT),
        name="merge_ffn",
    )(x, mod, o_f, o_b, sg, *att_flat, gates, gn, n2, fin, wa, wb, wo, wfi, wfo, *perms_t)


def _rope_tables(L):
    half = ATT_HDIM // 2
    lane = np.arange(2 * ATT_HDIM)
    inv = ROPE_THETA ** (-(lane % half).astype(np.float64) / half)
    sign = np.where(lane % ATT_HDIM < half, -1.0, 1.0)
    ang = np.arange(L, dtype=np.float64)[:, None] * inv[None, :]
    return jnp.asarray(np.cos(ang), dtype=F32), jnp.asarray(np.sin(ang) * sign[None, :], dtype=F32)


def _trunk(x, mod, lb, p, tm_in, tm_out, t_scan):
    B, L, D = x.shape
    cos, sin_signed = _rope_tables(L)
    (hg, v_h, sg, gates, *qkv) = _inproj_call(
        x, mod, p["g1"], p["w_in"], p["w_in8"], p["w_in8_unscale"], p["b_in"], lb, cos, sin_signed, tm_in)
    o_f, o_b = _hgrn_call(hg, v_h, t_scan)
    att = [_attn_call(*qkv[3 * i:3 * i + 3], r) for i, (_, r) in enumerate(DILATION_PATTERNS)]
    return _merge_call(x, mod, o_f, o_b, sg, att, gates, p["gn"], p["n2"], p["fin"],
                       p["wa"], p["wb"], p["wo"], p["wfi"], p["wfo"], tm_out)


def kernel(x_prompt, x_sample, c_prompt, c_sample, w_ada, b_ada, norm1_g, w_in, b_in, lb_logits, hg_norm_g, w_branch_a, w_branch_b, w_out, norm2_g, w_ffn_in, w_ffn_out, final_norm_g):
    assert w_ada.shape[0] == 1 and lb_logits.shape[0] == 2, "single-layer trunk"
    D = x_prompt.shape[-1]
    bp, bs = c_prompt.shape[0], c_sample.shape[0]
    c_all = jnp.concatenate([c_prompt, c_sample], axis=0)
    mod = _mod_call(c_all, w_ada[0], b_ada[0])
    mod = mod.reshape(mod.shape[0], 6, D)
    lb = _lb_call(lb_logits)
    row = lambda v: v.reshape(1, -1).astype(F32)
    n16 = N_BF16_BLOCKS * PROJ_BLOCK
    w_gate = w_in[0, :, n16:]
    col_top = jnp.max(jnp.abs(w_gate), axis=0, keepdims=True)
    w_unscale = jnp.where(col_top > 0.0, col_top, 1.0) * (1.0 / FP8_TOP)
    p = dict(g1=row(norm1_g[0]), w_in=w_in[0, :, :n16].astype(BF16), w_in8=(w_gate / w_unscale).astype(FP8),
             w_in8_unscale=w_unscale, b_in=row(b_in[0]), gn=row(hg_norm_g[0]),
             n2=row(norm2_g[0]), fin=row(final_norm_g), wa=w_branch_a[0].astype(BF16),
             wb=w_branch_b[0].astype(BF16), wo=w_out[0].astype(BF16), wfi=w_ffn_in[0].astype(BF16),
             wfo=w_ffn_out[0].astype(BF16))
    y_prompt = _trunk(x_prompt, mod[:bp], lb, p, 512, 512, 2048)
    y_sample = _trunk(x_sample, mod[bp:bp + bs], lb, p, 512, 512, 2048)
    return (y_prompt, y_sample)
```

```python
import numpy as np
import jax
import jax.numpy as jnp
from jax import lax
from jax.experimental import pallas as pl
from jax.experimental.pallas import tpu as pltpu

F32 = jnp.float32
BF16 = jnp.bfloat16
LANES = 128
SUBLANES = 8

NORM_EPS = 1e-6
HG_HEADS = 4
HG_DIM = 128
HG_WIDTH = HG_HEADS * HG_DIM
ATT_HEADS = 8
ATT_HDIM = 64
ATT_WIDTH = ATT_HEADS * ATT_HDIM
DILATION_PATTERNS = ((128, 1), (512, 4), (2048, 16))
ATT_RADIUS = 64
ROPE_THETA = 10000.0
LOG2E = 1.4426950408889634
PROJ_BLOCK = 512
N_PROJ_BLOCKS = 12
N_BF16_BLOCKS = 8
FP8 = jnp.float8_e4m3fn
FP8_TOP = 240.0
PERM_ROWS = 256

HG_CHUNK = 64
HG_SUB = 16
HG_SUB_SHIFT = 4
HG_STEP_HEADS = 2
HG_SLOTS, HG_SLOT_FWD, HG_SLOT_BWD = 5, 1, 3
HG_EXP2_CLAMP = 110.0
EXP2_ZERO = -1e30
V7X_VMEM_BYTES = 64 * 1024 * 1024
VMEM_LIMIT = V7X_VMEM_BYTES - 8 * 1024 * 1024


def _dot(a, b):
    return jnp.dot(a, b, preferred_element_type=F32)


def _dot_nt(a, b):
    return lax.dot_general(a, b, (((1,), (1,)), ((), ())), preferred_element_type=F32)


def _dot_tn(a, b):
    return lax.dot_general(a, b, (((0,), (0,)), ((), ())), preferred_element_type=F32)


def _sigmoid(x):
    return 1.0 / (1.0 + jnp.exp(-x))


def _rms(x):
    return x * lax.rsqrt(jnp.mean(x * x, axis=-1, keepdims=True) + NORM_EPS)


def _resident(shape):
    nd = len(shape)
    return pl.BlockSpec(shape, lambda *_: (0,) * nd, pipeline_mode=pl.Buffered(1))


def _mod_body(c_ref, w_ref, b_ref, o_ref):
    nrows, bn = c_ref.shape[0], w_ref.shape[1]
    outs = []
    for r in range(nrows):
        c = c_ref[r]
        a = c * _sigmoid(c)
        cols = [jnp.sum(a * w_ref[:, j * LANES:(j + 1) * LANES], axis=0, keepdims=True)
                for j in range(bn // LANES)]
        outs.append(jnp.concatenate(cols, axis=1) + b_ref[...])
    outs.append(jnp.zeros((o_ref.shape[0] - nrows, bn), F32))
    o_ref[...] = jnp.concatenate(outs, axis=0)


def _mod_call(c, w_ada, b_ada):
    nrows, d = c.shape
    n = w_ada.shape[1]
    bn = 1024
    rows_out = -(-nrows // SUBLANES) * SUBLANES
    c_rep = jnp.broadcast_to(c[:, :, None], (nrows, d, LANES))
    return pl.pallas_call(
        _mod_body,
        grid=(n // bn,),
        in_specs=[pl.BlockSpec((nrows, d, LANES), lambda j: (0, 0, 0)),
                  pl.BlockSpec((d, bn), lambda j: (0, j)),
                  pl.BlockSpec((1, bn), lambda j: (0, j))],
        out_specs=pl.BlockSpec((rows_out, bn), lambda j: (0, j)),
        out_shape=jax.ShapeDtypeStruct((rows_out, n), F32),
        name="adaln_mod",
    )(c_rep, w_ada, b_ada.reshape(1, n))


def _lb_body(l_ref, o_ref):
    l = l_ref[...]
    e = jnp.exp(l - jnp.max(l, axis=0, keepdims=True))
    o_ref[...] = e[0:1] / jnp.sum(e, axis=0, keepdims=True)


def _lb_call(lb_logits):
    n = lb_logits.shape[0]
    flat = lb_logits.reshape(n, -1).astype(F32)
    return pl.pallas_call(
        _lb_body,
        out_shape=jax.ShapeDtypeStruct((1, flat.shape[1]), F32),
        name="hgrn_lower_bounds",
    )(flat)


def _rope(p, cos, sin_signed):
    n = p.shape[-1]
    lane = lax.broadcasted_iota(jnp.int32, p.shape, 1)
    first_half = (lane & 63) < 32
    partner = jnp.where(first_half, pltpu.roll(p, n - 32, 1), pltpu.roll(p, 32, 1))
    reps = n // cos.shape[-1]
    return p * jnp.tile(cos, (1, reps)) + partner * jnp.tile(sin_signed, (1, reps))


def _store_dilated(t_bf, perm_ref, out_ref, r):
    if r == 1:
        out_ref[0] = t_bf
        return
    tm, w = t_bf.shape
    rows = PERM_ROWS // r
    for sub in range(tm // PERM_ROWS):
        grouped = _dot(perm_ref[...], t_bf[sub * PERM_ROWS:(sub + 1) * PERM_ROWS]).astype(BF16)
        for c in range(r):
            out_ref[0, sub * rows:(sub + 1) * rows, c * w:(c + 1) * w] = grouped[c * rows:(c + 1) * rows]


def _inproj_body(x_ref, mod_ref, g1_ref, w_ref, w8_ref, w8s_ref, b_ref, lb_ref, cos_ref, sin_ref, p4_ref, p16_ref,
                 hg_ref, vh_ref, sg_ref, gate_ref,
                 q1_ref, k1_ref, v1_ref, q4_ref, k4_ref, v4_ref, q16_ref, k16_ref, v16_ref):
    x = x_ref[0]
    mod = mod_ref[0]
    shift1, scale1 = mod[0:1], mod[1:2]
    h = _rms(x) * g1_ref[...]
    h = h * (1.0 + scale1) + shift1
    hb = h.astype(BF16)
    h_top = jnp.max(jnp.abs(h), axis=-1, keepdims=True)
    h_top = jnp.where(h_top > 0.0, h_top, 1.0)
    h_unscale = h_top * (1.0 / FP8_TOP)
    hb8 = (h * (FP8_TOP / h_top)).astype(FP8)

    def proj8(j):
        cols = slice(j * PROJ_BLOCK, (j + 1) * PROJ_BLOCK)
        cols8 = slice((j - N_BF16_BLOCKS) * PROJ_BLOCK, (j - N_BF16_BLOCKS + 1) * PROJ_BLOCK)
        return _dot(hb8, w8_ref[:, cols8]) * h_unscale * w8s_ref[:, cols8] + b_ref[:, cols]

    def proj(j):
        cols = slice(j * PROJ_BLOCK, (j + 1) * PROJ_BLOCK)
        return _dot(hb, w_ref[:, cols]) + b_ref[:, cols]

    p = proj(0)
    hg_ref[0, :, 0:HG_WIDTH] = p * _sigmoid(p) * (HG_DIM ** -0.5)

    def gates(p, lb, slot):
        f = lb + (1.0 - lb) * _sigmoid(p)
        hg_ref[0, :, slot * HG_WIDTH:(slot + 1) * HG_WIDTH] = jnp.log2(f)
        hg_ref[0, :, (slot + 1) * HG_WIDTH:(slot + 2) * HG_WIDTH] = 1.0 - f

    gates(proj(1), lb_ref[:, 0:HG_WIDTH], HG_SLOT_FWD)
    gates(proj(2), lb_ref[:, HG_WIDTH:2 * HG_WIDTH], HG_SLOT_BWD)
    vh_ref[0] = proj(3).astype(BF16)
    p = proj(4)
    sg_ref[0] = (p * _sigmoid(p)).astype(sg_ref.dtype)

    cos, sin_signed = cos_ref[...], sin_ref[...]
    q_a = (_rope(proj(5), cos, sin_signed) * (ATT_HDIM ** -0.5 * LOG2E)).astype(BF16)
    k_a = _rope(proj(6), cos, sin_signed).astype(BF16)
    v_a = proj(7).astype(BF16)
    for t, refs in ((q_a, (q1_ref, q4_ref, q16_ref)), (k_a, (k1_ref, k4_ref, k16_ref)),
                    (v_a, (v1_ref, v4_ref, v16_ref))):
        for (_, r), perm_ref, out_ref in zip(DILATION_PATTERNS, (None, p4_ref, p16_ref), refs):
            _store_dilated(t, perm_ref, out_ref, r)
    for j in range(N_BF16_BLOCKS, N_PROJ_BLOCKS):
        cols = slice((j - N_BF16_BLOCKS) * PROJ_BLOCK, (j - N_BF16_BLOCKS + 1) * PROJ_BLOCK)
        gate_ref[0, :, cols] = _sigmoid(proj8(j)).astype(gate_ref.dtype)


def _dilation_perm(tm, r):
    i = np.arange(tm)
    src = r * (i % (tm // r)) + i // (tm // r)
    return jnp.asarray(src[:, None] == np.arange(tm)[None, :], dtype=BF16)


def _inproj_call(x, mod, g1, w_in, w_in8, w_in8_unscale, b_in, lb, cos, sin_signed, tm):
    B, L, D = x.shape
    n_in = w_in.shape[1] + w_in8.shape[1]
    assert n_in == N_PROJ_BLOCKS * PROJ_BLOCK and L % tm == 0 and tm % PERM_ROWS == 0
    row = lambda w, r=1: pl.BlockSpec((1, tm // r, r * w), lambda b, i: (b, i, 0))
    tab = pl.BlockSpec((tm, cos.shape[1]), lambda b, i: (i, 0))
    shp = lambda w, dt, r=1: jax.ShapeDtypeStruct((B, L // r, r * w), dt)
    perms = [_dilation_perm(PERM_ROWS, r) for (_, r) in DILATION_PATTERNS[1:]]
    att_specs = [row(ATT_WIDTH, r) for (_, r) in DILATION_PATTERNS for _ in range(3)]
    att_shapes = [shp(ATT_WIDTH, BF16, r) for (_, r) in DILATION_PATTERNS for _ in range(3)]
    return pl.pallas_call(
        _inproj_body,
        grid=(B, L // tm),
        in_specs=[row(D),
                  pl.BlockSpec((1,) + mod.shape[1:], lambda b, i: (b, 0, 0)),
                  _resident(g1.shape), _resident(w_in.shape), _resident(w_in8.shape), _resident(w_in8_unscale.shape),
                  _resident(b_in.shape),
                  _resident(lb.shape),
                  tab, tab] + [_resident(p.shape) for p in perms],
        out_specs=[row(HG_SLOTS * HG_WIDTH)] + [row(HG_WIDTH)] * 2 + [row(2 * D)] + att_specs,
        out_shape=[shp(HG_SLOTS * HG_WIDTH, F32)] + [shp(HG_WIDTH, BF16)] * 2 + [shp(2 * D, BF16)] + att_shapes,
        compiler_params=pltpu.CompilerParams(
            dimension_semantics=("parallel", "parallel"), vmem_limit_bytes=VMEM_LIMIT),
        name="inproj",
    )(x, mod, g1, w_in, w_in8, w_in8_unscale, b_in, lb, cos, sin_signed, *perms)


def _bcast_row(c, group, r):
    T, w = c.shape
    c3 = c.reshape(T // group, group, w)
    return jnp.broadcast_to(c3[:, r:r + 1, :], c3.shape).reshape(T, w)


def _split2(g):
    g1 = g.astype(BF16)
    return g1, (g - g1.astype(F32)).astype(BF16)


def _hgrn_dir(q, k, v_bf, g, s_in, rev, o_ref, cols):
    T, dk = q.shape
    C, nch, nsb = HG_CHUNK, T // HG_CHUNK, HG_CHUNK // HG_SUB
    t = lax.broadcasted_iota(jnp.int32, (C, C), 0)
    s = lax.broadcasted_iota(jnp.int32, (C, C), 1)
    order = (s >= t) if rev else (s <= t)
    inner_mask = order & ((t >> HG_SUB_SHIFT) == (s >> HG_SUB_SHIFT))
    tri = order.astype(BF16)

    g_terms = jnp.concatenate(_split2(g), axis=1)
    c_parts = []
    for ci in range(nch):
        r = _dot(tri, g_terms[ci * C:(ci + 1) * C])
        c_parts.append(r[:, 0:dk] + r[:, dk:2 * dk])
    c = jnp.concatenate(c_parts, axis=0)
    yield

    far = 0 if rev else HG_SUB - 1
    edge = _bcast_row(c, HG_SUB, far)
    mid = _bcast_row(c, HG_SUB, HG_SUB // 2)
    c_end = _bcast_row(c, C, 0 if rev else C - 1)
    rowc = lax.broadcasted_iota(jnp.int32, (T, dk), 0) & (C - 1)

    k_edge = k * jnp.exp2(edge - c)
    q_cat, k_cat = [], []
    for J in (range(1, nsb) if rev else range(nsb - 1)):
        ref_j = _bcast_row(c, C, J * HG_SUB + far)
        queries = (rowc < J * HG_SUB) if rev else (rowc >= (J + 1) * HG_SUB)
        q_cat.append((q * jnp.exp2(jnp.where(queries, c - ref_j, EXP2_ZERO))).astype(BF16))
        in_j = (rowc >= J * HG_SUB) & (rowc < (J + 1) * HG_SUB)
        k_cat.append(jnp.where(in_j, k_edge, 0.0).astype(BF16))
    q_cat = jnp.concatenate(q_cat, axis=1)
    k_cat = jnp.concatenate(k_cat, axis=1)

    d_mid = jnp.clip(c - mid, -HG_EXP2_CLAMP, HG_EXP2_CLAMP)
    q_mid = (q * jnp.exp2(d_mid)).astype(BF16)
    k_mid = (k * jnp.exp2(-d_mid)).astype(BF16)
    q_dec = (q * jnp.exp2(c)).astype(BF16)
    k_end = (k * jnp.exp2(c_end - c)).astype(BF16)
    decay = jnp.exp2(c_end)
    yield

    o_intra, upd = [], []
    for ci in range(nch):
        rows = slice(ci * C, (ci + 1) * C)
        sc = _dot_nt(q_cat[rows], k_cat[rows])
        sc = sc + jnp.where(inner_mask, _dot_nt(q_mid[rows], k_mid[rows]), 0.0)
        o_intra.append(_dot(sc.astype(BF16), v_bf[rows]))
        upd.append(_dot_tn(v_bf[rows], k_end[rows]))
        yield

    s_t = s_in
    for ci in (reversed(range(nch)) if rev else range(nch)):
        rows = slice(ci * C, (ci + 1) * C)
        o_ref[0, rows, cols] = (o_intra[ci] + _dot_nt(q_dec[rows], s_t.astype(BF16))).astype(o_ref.dtype)
        s_t = s_t * decay[ci * C:ci * C + 1] + upd[ci]
        yield
    return s_t


def _hgrn_body(qf_ref, lff_ref, kf_ref, vf_ref, qb_ref, lfb_ref, kb_ref, vb_ref,
               of_ref, ob_ref, sf_ref, sb_ref):
    @pl.when(pl.program_id(2) == 0)
    def _():
        sf_ref[...] = jnp.zeros_like(sf_ref)
        sb_ref[...] = jnp.zeros_like(sb_ref)

    runs = []
    for h in range(HG_STEP_HEADS):
        cols = slice(h * HG_DIM, (h + 1) * HG_DIM)
        runs.append((_hgrn_dir(qf_ref[0, :, cols], kf_ref[0, :, cols], vf_ref[0, :, cols], lff_ref[0, :, cols],
                               sf_ref[h], False, of_ref, cols), sf_ref, h))
        runs.append((_hgrn_dir(qb_ref[0, :, cols], kb_ref[0, :, cols], vb_ref[0, :, cols], lfb_ref[0, :, cols],
                               sb_ref[h], True, ob_ref, cols), sb_ref, h))
    while runs:
        for run in list(runs):
            gen, state_ref, h = run
            try:
                next(gen)
            except StopIteration as done:
                state_ref[h] = done.value
                runs.remove(run)


def _hgrn_call(hg, v, T):
    B, L, W = v.shape
    nb = L // T
    assert L % T == 0 and T % HG_CHUNK == 0 and W == HG_WIDTH and hg.shape[2] == HG_SLOTS * W
    width = HG_STEP_HEADS * HG_DIM
    per_slot = W // width
    fwd = pl.BlockSpec((1, T, width), lambda b, h, j: (b, j, h))
    bwd = pl.BlockSpec((1, T, width), lambda b, h, j: (b, nb - 1 - j, h))
    fwd_slot = lambda s: pl.BlockSpec((1, T, width), lambda b, h, j: (b, j, s * per_slot + h))
    bwd_slot = lambda s: pl.BlockSpec((1, T, width), lambda b, h, j: (b, nb - 1 - j, s * per_slot + h))
    return pl.pallas_call(
        _hgrn_body,
        grid=(B, HG_HEADS // HG_STEP_HEADS, nb),
        in_specs=[fwd_slot(0), fwd_slot(HG_SLOT_FWD), fwd_slot(HG_SLOT_FWD + 1), fwd,
                  bwd_slot(0), bwd_slot(HG_SLOT_BWD), bwd_slot(HG_SLOT_BWD + 1), bwd],
        out_specs=[fwd, bwd],
        out_shape=[jax.ShapeDtypeStruct((B, L, W), BF16)] * 2,
        scratch_shapes=[pltpu.VMEM((HG_STEP_HEADS, HG_DIM, HG_DIM), F32)] * 2,
        compiler_params=pltpu.CompilerParams(
            dimension_semantics=("parallel", "parallel", "arbitrary"), vmem_limit_bytes=VMEM_LIMIT),
        name="hgrn_scan",
    )(hg, hg, hg, v, hg, hg, hg, v)


ATT_QBLK = 2 * ATT_RADIUS
ATT_STEP_BLOCKS = 8


def _attn_body(q_ref, kp_ref, kc_ref, kn_ref, vp_ref, vc_ref, vn_ref, o_ref, lse_ref):
    n = pl.program_id(2)
    last = pl.num_programs(2) - 1
    kcat = jnp.concatenate([kp_ref[0], kc_ref[0], kn_ref[0]], axis=0)
    vcat = jnp.concatenate([vp_ref[0], vc_ref[0], vn_ref[0]], axis=0)
    nq, nk = ATT_QBLK, ATT_QBLK + 2 * ATT_RADIUS
    nblk = q_ref.shape[1] // nq
    i = lax.broadcasted_iota(jnp.int32, (nq, nk), 0)
    j = lax.broadcasted_iota(jnp.int32, (nq, nk), 1)
    band = (j >= i) & (j <= i + 2 * ATT_RADIUS)
    has_prev = (j >= ATT_RADIUS) | (n > 0)
    has_next = (j < nq + ATT_RADIUS) | (n < last)

    pair_w = 2 * ATT_HDIM
    npairs = q_ref.shape[2] // pair_w
    lane = lax.broadcasted_iota(jnp.int32, kcat.shape, 1) & (pair_w - 1)
    mask_a = jnp.where(lane < ATT_HDIM, 1.0, 0.0).astype(BF16)
    mask_b = jnp.where(lane < ATT_HDIM, 0.0, 1.0).astype(BF16)
    k_a, k_b, v_a, v_b = kcat * mask_a, kcat * mask_b, vcat * mask_a, vcat * mask_b
    lo_k = lax.broadcasted_iota(jnp.int32, (nk, pair_w), 1) < ATT_HDIM
    ones_a = jnp.where(lo_k, 1.0, 0.0).astype(BF16)
    ones_b = jnp.where(lo_k, 0.0, 1.0).astype(BF16)
    lo_q = lax.broadcasted_iota(jnp.int32, (nq, pair_w), 1) < ATT_HDIM

    for blk in range(nblk):
        valid = band
        if blk == 0:
            valid = valid & has_prev
        if blk == nblk - 1:
            valid = valid & has_next
        qrows = slice(blk * nq, (blk + 1) * nq)
        krows = slice(blk * nq, blk * nq + nk)
        scores = []
        for p in range(npairs):
            cols = slice(p * pair_w, (p + 1) * pair_w)
            q = q_ref[0, qrows, cols]
            scores.append(_dot_nt(q, k_a[krows, cols]))
            scores.append(_dot_nt(q, k_b[krows, cols]))
        scores = [jnp.where(valid, s, -jnp.inf) for s in scores]
        tops = [jnp.max(s, axis=-1, keepdims=True) for s in scores]
        probs = [jnp.exp2(s - m).astype(BF16) for s, m in zip(scores, tops)]
        for p in range(npairs):
            cols = slice(p * pair_w, (p + 1) * pair_w)
            res = (_dot(probs[2 * p], jnp.concatenate([v_a[krows, cols], ones_a], axis=1))
                   + _dot(probs[2 * p + 1], jnp.concatenate([v_b[krows, cols], ones_b], axis=1)))
            num, den = res[:, 0:pair_w], res[:, pair_w:2 * pair_w]
            o_ref[0, qrows, cols] = (num / den).astype(o_ref.dtype)
            lse_ref[0, qrows, cols] = jnp.where(lo_q, tops[2 * p], tops[2 * p + 1]) + jnp.log2(den)


def _attn_call(q, k, v, r):
    B, Lr, rW = q.shape
    W = rW // r
    blocks = min(ATT_STEP_BLOCKS, Lr // ATT_QBLK)
    step = ATT_QBLK * blocks
    classes = min(r, ATT_STEP_BLOCKS // blocks)
    width = classes * W
    assert W == ATT_WIDTH and Lr % step == 0 and r % classes == 0
    per_step = step // ATT_RADIUS
    nhalo = Lr // ATT_RADIUS
    center = pl.BlockSpec((1, step, width), lambda b, c, n: (b, n, c))
    prev = pl.BlockSpec((1, ATT_RADIUS, width), lambda b, c, n: (b, jnp.maximum(per_step * n - 1, 0), c))
    nxt = pl.BlockSpec((1, ATT_RADIUS, width), lambda b, c, n: (b, jnp.minimum(per_step * (n + 1), nhalo - 1), c))
    return pl.pallas_call(
        _attn_body,
        grid=(B, r // classes, Lr // step),
        in_specs=[center, prev, center, nxt, prev, center, nxt],
        out_specs=[center, center],
        out_shape=[jax.ShapeDtypeStruct((B, Lr, rW), BF16), jax.ShapeDtypeStruct((B, Lr, rW), F32)],
        compiler_params=pltpu.CompilerParams(
            dimension_semantics=("parallel", "parallel", "parallel"), vmem_limit_bytes=VMEM_LIMIT),
        name=f"dilated_attn_r{r}",
    )(q, k, k, k, v, v, v)


FFN_CHUNK = 256


def _load_natural(a_ref, l_ref, perm_t_ref, r):
    if r == 1:
        return a_ref[0].astype(F32), l_ref[0]
    w = a_ref.shape[2] // r
    rows = PERM_ROWS // r
    a_nat, l_nat = [], []
    for sub in range(a_ref.shape[1] // rows):
        blk = slice(sub * rows, (sub + 1) * rows)
        a = jnp.concatenate([a_ref[0, blk, c * w:(c + 1) * w] for c in range(r)], axis=0)
        l = jnp.concatenate([l_ref[0, blk, c * w:(c + 1) * w] for c in range(r)], axis=0)
        l_hi = l.astype(BF16)
        l_lo = (l - l_hi.astype(F32)).astype(BF16)
        nat = _dot(perm_t_ref[...], jnp.concatenate([a, l_hi, l_lo], axis=1))
        a_nat.append(nat[:, 0:w])
        l_nat.append(nat[:, w:2 * w] + nat[:, 2 * w:3 * w])
    return jnp.concatenate(a_nat, axis=0), jnp.concatenate(l_nat, axis=0)


def _merge_body(x_ref, mod_ref, of_ref, ob_ref, sg_ref, a1_ref, l1_ref, a2_ref, l2_ref, a3_ref, l3_ref,
                gate_ref, gn_ref, n2_ref, fin_ref, wa_ref, wb_ref, wo_ref, wfi_ref, wfo_ref,
                p4t_ref, p16t_ref, y_ref, act_ref):
    mod = mod_ref[0]
    gate1, shift2, scale2, gate2 = mod[2:3], mod[3:4], mod[4:5], mod[5:6]

    o = of_ref[0].astype(F32) + ob_ref[0].astype(F32)
    o_a = jnp.concatenate([_rms(o[:, h * HG_DIM:(h + 1) * HG_DIM]) for h in range(HG_HEADS)], axis=1)
    o_a = o_a * gn_ref[...] * sg_ref[0]

    (a1, l1), (a2, l2), (a3, l3) = [
        _load_natural(a_ref, l_ref, perm_ref, r)
        for (_, r), a_ref, l_ref, perm_ref in zip(DILATION_PATTERNS, (a1_ref, a2_ref, a3_ref),
                                                  (l1_ref, l2_ref, l3_ref), (None, p4t_ref, p16t_ref))]
    top = jnp.maximum(jnp.maximum(l1, l2), l3)
    w1, w2, w3 = jnp.exp2(l1 - top), jnp.exp2(l2 - top), jnp.exp2(l3 - top)
    o_b = (a1 * w1 + a2 * w2 + a3 * w3) / (w1 + w2 + w3)

    d_model = x_ref.shape[2]
    merged = (gate_ref[0, :, 0:d_model] * _dot(o_a.astype(BF16), wa_ref[...])
              + gate_ref[0, :, d_model:2 * d_model] * _dot(o_b.astype(BF16), wb_ref[...]))
    x1 = x_ref[0] + gate1 * _dot(merged.astype(BF16), wo_ref[...])

    h2 = (_rms(x1) * n2_ref[...] * (1.0 + scale2) + shift2).astype(BF16)
    hidden = wfo_ref.shape[0]
    for c0 in range(0, hidden, FFN_CHUNK):
        gt = _dot(h2, wfi_ref[:, c0:c0 + FFN_CHUNK])
        up = _dot(h2, wfi_ref[:, hidden + c0:hidden + c0 + FFN_CHUNK])
        act_ref[:, c0:c0 + FFN_CHUNK] = (gt * _sigmoid(gt) * up).astype(BF16)
    x2 = x1 + gate2 * _dot(act_ref[...], wfo_ref[...])
    y_ref[0] = _rms(x2) * fin_ref[...]


def _merge_call(x, mod, o_f, o_b, sg, att, gates, gn, n2, fin, wa, wb, wo, wfi, wfo, tm):
    B, L, D = x.shape
    hidden = wfo.shape[0]
    assert L % tm == 0 and tm % PERM_ROWS == 0 and hidden % FFN_CHUNK == 0 and wfi.shape[1] == 2 * hidden
    row = lambda w, r=1: pl.BlockSpec((1, tm // r, r * w), lambda b, i: (b, i, 0))
    att_flat = [t for pair in att for t in pair]
    att_specs = [row(ATT_WIDTH, r) for (_, r) in DILATION_PATTERNS for _ in range(2)]
    perms_t = [_dilation_perm(PERM_ROWS, r).T for (_, r) in DILATION_PATTERNS[1:]]
    return pl.pallas_call(
        _merge_body,
        grid=(B, L // tm),
        in_specs=[row(D), pl.BlockSpec((1,) + mod.shape[1:], lambda b, i: (b, 0, 0))]
                 + [row(HG_WIDTH)] * 3 + att_specs + [row(2 * D)]
                 + [_resident(t.shape) for t in (gn, n2, fin, wa, wb, wo, wfi, wfo, *perms_t)],
        out_specs=row(D),
        out_shape=jax.ShapeDtypeStruct((B, L, D), F32),
        scratch_shapes=[pltpu.VMEM((tm, hidden), BF16)],
        compiler_params=pltpu.CompilerParams(
            dimension_semantics=("parallel", "parallel"), vmem_limit_bytes=VMEM_LIMIT),
        name="merge_ffn",
    )(x, mod, o_f, o_b, sg, *att_flat, gates, gn, n2, fin, wa, wb, wo, wfi, wfo, *perms_t)


def _rope_tables(L):
    half = ATT_HDIM // 2
    lane = np.arange(2 * ATT_HDIM)
    inv = ROPE_THETA ** (-(lane % half).astype(np.float64) / half)
    sign = np.where(lane % ATT_HDIM < half, -1.0, 1.0)
    ang = np.arange(L, dtype=np.float64)[:, None] * inv[None, :]
    return jnp.asarray(np.cos(ang), dtype=F32), jnp.asarray(np.sin(ang) * sign[None, :], dtype=F32)


def _trunk(x, mod, lb, p, tm_in, tm_out, t_scan):
    B, L, D = x.shape
    cos, sin_signed = _rope_tables(L)
    (hg, v_h, sg, gates, *qkv) = _inproj_call(
        x, mod, p["g1"], p["w_in"], p["w_in8"], p["w_in8_unscale"], p["b_in"], lb, cos, sin_signed, tm_in)
    o_f, o_b = _hgrn_call(hg, v_h, t_scan)
    att = [_attn_call(*qkv[3 * i:3 * i + 3], r) for i, (_, r) in enumerate(DILATION_PATTERNS)]
    return _merge_call(x, mod, o_f, o_b, sg, att, gates, p["gn"], p["n2"], p["fin"],
                       p["wa"], p["wb"], p["wo"], p["wfi"], p["wfo"], tm_out)


def kernel(x_prompt, x_sample, c_prompt, c_sample, w_ada, b_ada, norm1_g, w_in, b_in, lb_logits, hg_norm_g, w_branch_a, w_branch_b, w_out, norm2_g, w_ffn_in, w_ffn_out, final_norm_g):
    assert w_ada.shape[0] == 1 and lb_logits.shape[0] == 2, "single-layer trunk"
    D = x_prompt.shape[-1]
    bp, bs = c_prompt.shape[0], c_sample.shape[0]
    c_all = jnp.concatenate([c_prompt, c_sample], axis=0)
    mod = _mod_call(c_all, w_ada[0], b_ada[0])
    mod = mod.reshape(mod.shape[0], 6, D)
    lb = _lb_call(lb_logits)
    row = lambda v: v.reshape(1, -1).astype(F32)
    n16 = N_BF16_BLOCKS * PROJ_BLOCK
    w_gate = w_in[0, :, n16:]
    col_top = jnp.max(jnp.abs(w_gate), axis=0, keepdims=True)
    w_unscale = jnp.where(col_top > 0.0, col_top, 1.0) * (1.0 / FP8_TOP)
    p = dict(g1=row(norm1_g[0]), w_in=w_in[0, :, :n16].astype(BF16), w_in8=(w_gate / w_unscale).astype(FP8),
             w_in8_unscale=w_unscale, b_in=row(b_in[0]), gn=row(hg_norm_g[0]),
             n2=row(norm2_g[0]), fin=row(final_norm_g), wa=w_branch_a[0].astype(BF16),
             wb=w_branch_b[0].astype(BF16), wo=w_out[0].astype(BF16), wfi=w_ffn_in[0].astype(BF16),
             wfo=w_ffn_out[0].astype(BF16))
    y_prompt = _trunk(x_prompt, mod[:bp], lb, p, 512, 512, 2048)
    y_sample = _trunk(x_sample, mod[bp:bp + bs], lb, p, 512, 512, 2048)
    return (y_prompt, y_sample)
```

```python
import numpy as np
import jax
import jax.numpy as jnp
from jax import lax
from jax.experimental import pallas as pl
from jax.experimental.pallas import tpu as pltpu

F32 = jnp.float32
BF16 = jnp.bfloat16
LANES = 128
SUBLANES = 8

NORM_EPS = 1e-6
HG_HEADS = 4
HG_DIM = 128
HG_WIDTH = HG_HEADS * HG_DIM
ATT_HEADS = 8
ATT_HDIM = 64
ATT_WIDTH = ATT_HEADS * ATT_HDIM
DILATION_PATTERNS = ((128, 1), (512, 4), (2048, 16))
ATT_RADIUS = 64
ROPE_THETA = 10000.0
LOG2E = 1.4426950408889634
PROJ_BLOCK = 512
N_PROJ_BLOCKS = 12
N_BF16_BLOCKS = 8
FP8 = jnp.float8_e4m3fn
FP8_TOP = 240.0
PERM_ROWS = 256

HG_CHUNK = 64
HG_SUB = 16
HG_SUB_SHIFT = 4
HG_STEP_HEADS = 2
HG_SLOTS, HG_SLOT_FWD, HG_SLOT_BWD = 5, 1, 3
HG_EXP2_CLAMP = 110.0
EXP2_ZERO = -1e30
V7X_VMEM_BYTES = 64 * 1024 * 1024
VMEM_LIMIT = V7X_VMEM_BYTES - 8 * 1024 * 1024


def _dot(a, b):
    return jnp.dot(a, b, preferred_element_type=F32)


def _dot_nt(a, b):
    return lax.dot_general(a, b, (((1,), (1,)), ((), ())), preferred_element_type=F32)


def _dot_tn(a, b):
    return lax.dot_general(a, b, (((0,), (0,)), ((), ())), preferred_element_type=F32)


def _sigmoid(x):
    return 1.0 / (1.0 + jnp.exp(-x))


def _rms(x):
    return x * lax.rsqrt(jnp.mean(x * x, axis=-1, keepdims=True) + NORM_EPS)


def _resident(shape):
    nd = len(shape)
    return pl.BlockSpec(shape, lambda *_: (0,) * nd, pipeline_mode=pl.Buffered(1))


def _mod_body(c_ref, w_ref, b_ref, o_ref):
    nrows, bn = c_ref.shape[0], w_ref.shape[1]
    outs = []
    for r in range(nrows):
        c = c_ref[r]
        a = c * _sigmoid(c)
        cols = [jnp.sum(a * w_ref[:, j * LANES:(j + 1) * LANES], axis=0, keepdims=True)
                for j in range(bn // LANES)]
        outs.append(jnp.concatenate(cols, axis=1) + b_ref[...])
    outs.append(jnp.zeros((o_ref.shape[0] - nrows, bn), F32))
    o_ref[...] = jnp.concatenate(outs, axis=0)


def _mod_call(c, w_ada, b_ada):
    nrows, d = c.shape
    n = w_ada.shape[1]
    bn = 1024
    rows_out = -(-nrows // SUBLANES) * SUBLANES
    c_rep = jnp.broadcast_to(c[:, :, None], (nrows, d, LANES))
    return pl.pallas_call(
        _mod_body,
        grid=(n // bn,),
        in_specs=[pl.BlockSpec((nrows, d, LANES), lambda j: (0, 0, 0)),
                  pl.BlockSpec((d, bn), lambda j: (0, j)),
                  pl.BlockSpec((1, bn), lambda j: (0, j))],
        out_specs=pl.BlockSpec((rows_out, bn), lambda j: (0, j)),
        out_shape=jax.ShapeDtypeStruct((rows_out, n), F32),
        name="adaln_mod",
    )(c_rep, w_ada, b_ada.reshape(1, n))


def _lb_body(l_ref, o_ref):
    l = l_ref[...]
    e = jnp.exp(l - jnp.max(l, axis=0, keepdims=True))
    o_ref[...] = e[0:1] / jnp.sum(e, axis=0, keepdims=True)


def _lb_call(lb_logits):
    n = lb_logits.shape[0]
    flat = lb_logits.reshape(n, -1).astype(F32)
    return pl.pallas_call(
        _lb_body,
        out_shape=jax.ShapeDtypeStruct((1, flat.shape[1]), F32),
        name="hgrn_lower_bounds",
    )(flat)


def _rope(p, cos, sin_signed):
    n = p.shape[-1]
    lane = lax.broadcasted_iota(jnp.int32, p.shape, 1)
    first_half = (lane & 63) < 32
    partner = jnp.where(first_half, pltpu.roll(p, n - 32, 1), pltpu.roll(p, 32, 1))
    reps = n // cos.shape[-1]
    return p * jnp.tile(cos, (1, reps)) + partner * jnp.tile(sin_signed, (1, reps))


def _store_dilated(t_bf, perm_ref, out_ref, r):
    if r == 1:
        out_ref[0] = t_bf
        return
    tm, w = t_bf.shape
    rows = PERM_ROWS // r
    for sub in range(tm // PERM_ROWS):
        grouped = _dot(perm_ref[...], t_bf[sub * PERM_ROWS:(sub + 1) * PERM_ROWS]).astype(BF16)
        for c in range(r):
            out_ref[0, sub * rows:(sub + 1) * rows, c * w:(c + 1) * w] = grouped[c * rows:(c + 1) * rows]


def _inproj_body(x_ref, mod_ref, g1_ref, w_ref, w8_ref, w8s_ref, b_ref, lb_ref, cos_ref, sin_ref, p4_ref, p16_ref,
                 hg_ref, vh_ref, sg_ref, gate_ref,
                 q1_ref, k1_ref, v1_ref, q4_ref, k4_ref, v4_ref, q16_ref, k16_ref, v16_ref):
    x = x_ref[0]
    mod = mod_ref[0]
    shift1, scale1 = mod[0:1], mod[1:2]
    h = _rms(x) * g1_ref[...]
    h = h * (1.0 + scale1) + shift1
    hb = h.astype(BF16)
    h_top = jnp.max(jnp.abs(h), axis=-1, keepdims=True)
    h_top = jnp.where(h_top > 0.0, h_top, 1.0)
    h_unscale = h_top * (1.0 / FP8_TOP)
    hb8 = (h * (FP8_TOP / h_top)).astype(FP8)

    def proj8(j):
        cols = slice(j * PROJ_BLOCK, (j + 1) * PROJ_BLOCK)
        cols8 = slice((j - N_BF16_BLOCKS) * PROJ_BLOCK, (j - N_BF16_BLOCKS + 1) * PROJ_BLOCK)
        return _dot(hb8, w8_ref[:, cols8]) * h_unscale * w8s_ref[:, cols8] + b_ref[:, cols]

    def proj(j):
        cols = slice(j * PROJ_BLOCK, (j + 1) * PROJ_BLOCK)
        return _dot(hb, w_ref[:, cols]) + b_ref[:, cols]

    p = proj(0)
    hg_ref[0, :, 0:HG_WIDTH] = p * _sigmoid(p) * (HG_DIM ** -0.5)

    def gates(p, lb, slot):
        f = lb + (1.0 - lb) * _sigmoid(p)
        hg_ref[0, :, slot * HG_WIDTH:(slot + 1) * HG_WIDTH] = jnp.log2(f)
        hg_ref[0, :, (slot + 1) * HG_WIDTH:(slot + 2) * HG_WIDTH] = 1.0 - f

    gates(proj(1), lb_ref[:, 0:HG_WIDTH], HG_SLOT_FWD)
    gates(proj(2), lb_ref[:, HG_WIDTH:2 * HG_WIDTH], HG_SLOT_BWD)
    vh_ref[0] = proj(3).astype(BF16)
    p = proj(4)
    sg_ref[0] = (p * _sigmoid(p)).astype(sg_ref.dtype)

    cos, sin_signed = cos_ref[...], sin_ref[...]
    q_a = (_rope(proj(5), cos, sin_signed) * (ATT_HDIM ** -0.5 * LOG2E)).astype(BF16)
    k_a = _rope(proj(6), cos, sin_signed).astype(BF16)
    v_a = proj(7).astype(BF16)
    for t, refs in ((q_a, (q1_ref, q4_ref, q16_ref)), (k_a, (k1_ref, k4_ref, k16_ref)),
                    (v_a, (v1_ref, v4_ref, v16_ref))):
        for (_, r), perm_ref, out_ref in zip(DILATION_PATTERNS, (None, p4_ref, p16_ref), refs):
            _store_dilated(t, perm_ref, out_ref, r)
    for j in range(N_BF16_BLOCKS, N_PROJ_BLOCKS):
        cols = slice((j - N_BF16_BLOCKS) * PROJ_BLOCK, (j - N_BF16_BLOCKS + 1) * PROJ_BLOCK)
        gate_ref[0, :, cols] = _sigmoid(proj8(j)).astype(gate_ref.dtype)


def _dilation_perm(tm, r):
    i = np.arange(tm)
    src = r * (i % (tm // r)) + i // (tm // r)
    return jnp.asarray(src[:, None] == np.arange(tm)[None, :], dtype=BF16)


def _inproj_call(x, mod, g1, w_in, w_in8, w_in8_unscale, b_in, lb, cos, sin_signed, tm):
    B, L, D = x.shape
    n_in = w_in.shape[1] + w_in8.shape[1]
    assert n_in == N_PROJ_BLOCKS * PROJ_BLOCK and L % tm == 0 and tm % PERM_ROWS == 0
    row = lambda w, r=1: pl.BlockSpec((1, tm // r, r * w), lambda b, i: (b, i, 0))
    tab = pl.BlockSpec((tm, cos.shape[1]), lambda b, i: (i, 0))
    shp = lambda w, dt, r=1: jax.ShapeDtypeStruct((B, L // r, r * w), dt)
    perms = [_dilation_perm(PERM_ROWS, r) for (_, r) in DILATION_PATTERNS[1:]]
    att_specs = [row(ATT_WIDTH, r) for (_, r) in DILATION_PATTERNS for _ in range(3)]
    att_shapes = [shp(ATT_WIDTH, BF16, r) for (_, r) in DILATION_PATTERNS for _ in range(3)]
    return pl.pallas_call(
        _inproj_body,
        grid=(B, L // tm),
        in_specs=[row(D),
                  pl.BlockSpec((1,) + mod.shape[1:], lambda b, i: (b, 0, 0)),
                  _resident(g1.shape), _resident(w_in.shape), _resident(w_in8.shape), _resident(w_in8_unscale.shape),
                  _resident(b_in.shape),
                  _resident(lb.shape),
                  tab, tab] + [_resident(p.shape) for p in perms],
        out_specs=[row(HG_SLOTS * HG_WIDTH)] + [row(HG_WIDTH)] * 2 + [row(2 * D)] + att_specs,
        out_shape=[shp(HG_SLOTS * HG_WIDTH, F32)] + [shp(HG_WIDTH, BF16)] * 2 + [shp(2 * D, BF16)] + att_shapes,
        compiler_params=pltpu.CompilerParams(
            dimension_semantics=("parallel", "parallel"), vmem_limit_bytes=VMEM_LIMIT),
        name="inproj",
    )(x, mod, g1, w_in, w_in8, w_in8_unscale, b_in, lb, cos, sin_signed, *perms)


def _bcast_row(c, group, r):
    T, w = c.shape
    c3 = c.reshape(T // group, group, w)
    return jnp.broadcast_to(c3[:, r:r + 1, :], c3.shape).reshape(T, w)


def _split2(g):
    g1 = g.astype(BF16)
    return g1, (g - g1.astype(F32)).astype(BF16)


def _hgrn_dir(q, k, v_bf, g, s_in, rev, o_ref, cols):
    T, dk = q.shape
    C, nch, nsb = HG_CHUNK, T // HG_CHUNK, HG_CHUNK // HG_SUB
    t = lax.broadcasted_iota(jnp.int32, (C, C), 0)
    s = lax.broadcasted_iota(jnp.int32, (C, C), 1)
    order = (s >= t) if rev else (s <= t)
    inner_mask = order & ((t >> HG_SUB_SHIFT) == (s >> HG_SUB_SHIFT))
    tri = order.astype(BF16)

    g_terms = jnp.concatenate(_split2(g), axis=1)
    c_parts = []
    for ci in range(nch):
        r = _dot(tri, g_terms[ci * C:(ci + 1) * C])
        c_parts.append(r[:, 0:dk] + r[:, dk:2 * dk])
    c = jnp.concatenate(c_parts, axis=0)
    yield

    far = 0 if rev else HG_SUB - 1
    edge = _bcast_row(c, HG_SUB, far)
    mid = _bcast_row(c, HG_SUB, HG_SUB // 2)
    c_end = _bcast_row(c, C, 0 if rev else C - 1)
    rowc = lax.broadcasted_iota(jnp.int32, (T, dk), 0) & (C - 1)

    k_edge = k * jnp.exp2(edge - c)
    q_cat, k_cat = [], []
    for J in (range(1, nsb) if rev else range(nsb - 1)):
        ref_j = _bcast_row(c, C, J * HG_SUB + far)
        queries = (rowc < J * HG_SUB) if rev else (rowc >= (J + 1) * HG_SUB)
        q_cat.append((q * jnp.exp2(jnp.where(queries, c - ref_j, EXP2_ZERO))).astype(BF16))
        in_j = (rowc >= J * HG_SUB) & (rowc < (J + 1) * HG_SUB)
        k_cat.append(jnp.where(in_j, k_edge, 0.0).astype(BF16))
    q_cat = jnp.concatenate(q_cat, axis=1)
    k_cat = jnp.concatenate(k_cat, axis=1)

    d_mid = jnp.clip(c - mid, -HG_EXP2_CLAMP, HG_EXP2_CLAMP)
    q_mid = (q * jnp.exp2(d_mid)).astype(BF16)
    k_mid = (k * jnp.exp2(-d_mid)).astype(BF16)
    q_dec = (q * jnp.exp2(c)).astype(BF16)
    k_end = (k * jnp.exp2(c_end - c)).astype(BF16)
    decay = jnp.exp2(c_end)
    yield

    scores, upd = [], []
    for ci in range(nch):
        rows = slice(ci * C, (ci + 1) * C)
        sc = _dot_nt(q_cat[rows], k_cat[rows])
        sc = sc + jnp.where(inner_mask, _dot_nt(q_mid[rows], k_mid[rows]), 0.0)
        scores.append(sc.astype(BF16))
        upd.append(_dot_tn(v_bf[rows], k_end[rows]))
        yield

    s_t = s_in
    for ci in (reversed(range(nch)) if rev else range(nch)):
        rows = slice(ci * C, (ci + 1) * C)
        lhs = jnp.concatenate([q_dec[rows], scores[ci]], axis=1)
        rhs = jnp.concatenate([s_t.T.astype(BF16), v_bf[rows]], axis=0)
        o_ref[0, rows, cols] = _dot(lhs, rhs).astype(o_ref.dtype)
        s_t = s_t * decay[ci * C:ci * C + 1] + upd[ci]
        yield
    return s_t


def _hgrn_body(qf_ref, lff_ref, kf_ref, vf_ref, qb_ref, lfb_ref, kb_ref, vb_ref,
               of_ref, ob_ref, sf_ref, sb_ref):
    @pl.when(pl.program_id(2) == 0)
    def _():
        sf_ref[...] = jnp.zeros_like(sf_ref)
        sb_ref[...] = jnp.zeros_like(sb_ref)

    runs = []
    for h in range(HG_STEP_HEADS):
        cols = slice(h * HG_DIM, (h + 1) * HG_DIM)
        runs.append((_hgrn_dir(qf_ref[0, :, cols], kf_ref[0, :, cols], vf_ref[0, :, cols], lff_ref[0, :, cols],
                               sf_ref[h], False, of_ref, cols), sf_ref, h))
        runs.append((_hgrn_dir(qb_ref[0, :, cols], kb_ref[0, :, cols], vb_ref[0, :, cols], lfb_ref[0, :, cols],
                               sb_ref[h], True, ob_ref, cols), sb_ref, h))
    while runs:
        for run in list(runs):
            gen, state_ref, h = run
            try:
                next(gen)
            except StopIteration as done:
                state_ref[h] = done.value
                runs.remove(run)


def _hgrn_call(hg, v, T):
    B, L, W = v.shape
    nb = L // T
    assert L % T == 0 and T % HG_CHUNK == 0 and W == HG_WIDTH and hg.shape[2] == HG_SLOTS * W
    width = HG_STEP_HEADS * HG_DIM
    per_slot = W // width
    fwd = pl.BlockSpec((1, T, width), lambda b, h, j: (b, j, h))
    bwd = pl.BlockSpec((1, T, width), lambda b, h, j: (b, nb - 1 - j, h))
    fwd_slot = lambda s: pl.BlockSpec((1, T, width), lambda b, h, j: (b, j, s * per_slot + h))
    bwd_slot = lambda s: pl.BlockSpec((1, T, width), lambda b, h, j: (b, nb - 1 - j, s * per_slot + h))
    return pl.pallas_call(
        _hgrn_body,
        grid=(B, HG_HEADS // HG_STEP_HEADS, nb),
        in_specs=[fwd_slot(0), fwd_slot(HG_SLOT_FWD), fwd_slot(HG_SLOT_FWD + 1), fwd,
                  bwd_slot(0), bwd_slot(HG_SLOT_BWD), bwd_slot(HG_SLOT_BWD + 1), bwd],
        out_specs=[fwd, bwd],
        out_shape=[jax.ShapeDtypeStruct((B, L, W), BF16)] * 2,
        scratch_shapes=[pltpu.VMEM((HG_STEP_HEADS, HG_DIM, HG_DIM), F32)] * 2,
        compiler_params=pltpu.CompilerParams(
            dimension_semantics=("parallel", "parallel", "arbitrary"), vmem_limit_bytes=VMEM_LIMIT),
        name="hgrn_scan",
    )(hg, hg, hg, v, hg, hg, hg, v)


ATT_QBLK = 2 * ATT_RADIUS
ATT_STEP_BLOCKS = 8


def _attn_body(q_ref, kp_ref, kc_ref, kn_ref, vp_ref, vc_ref, vn_ref, o_ref, lse_ref):
    n = pl.program_id(2)
    last = pl.num_programs(2) - 1
    kcat = jnp.concatenate([kp_ref[0], kc_ref[0], kn_ref[0]], axis=0)
    vcat = jnp.concatenate([vp_ref[0], vc_ref[0], vn_ref[0]], axis=0)
    nq, nk = ATT_QBLK, ATT_QBLK + 2 * ATT_RADIUS
    nblk = q_ref.shape[1] // nq
    i = lax.broadcasted_iota(jnp.int32, (nq, nk), 0)
    j = lax.broadcasted_iota(jnp.int32, (nq, nk), 1)
    band = (j >= i) & (j <= i + 2 * ATT_RADIUS)
    has_prev = (j >= ATT_RADIUS) | (n > 0)
    has_next = (j < nq + ATT_RADIUS) | (n < last)

    pair_w = 2 * ATT_HDIM
    npairs = q_ref.shape[2] // pair_w
    lane = lax.broadcasted_iota(jnp.int32, kcat.shape, 1) & (pair_w - 1)
    mask_a = jnp.where(lane < ATT_HDIM, 1.0, 0.0).astype(BF16)
    mask_b = jnp.where(lane < ATT_HDIM, 0.0, 1.0).astype(BF16)
    k_a, k_b, v_a, v_b = kcat * mask_a, kcat * mask_b, vcat * mask_a, vcat * mask_b
    lo_k = lax.broadcasted_iota(jnp.int32, (nk, pair_w), 1) < ATT_HDIM
    ones_a = jnp.where(lo_k, 1.0, 0.0).astype(BF16)
    ones_b = jnp.where(lo_k, 0.0, 1.0).astype(BF16)
    lo_q = lax.broadcasted_iota(jnp.int32, (nq, pair_w), 1) < ATT_HDIM

    for blk in range(nblk):
        valid = band
        if blk == 0:
            valid = valid & has_prev
        if blk == nblk - 1:
            valid = valid & has_next
        qrows = slice(blk * nq, (blk + 1) * nq)
        krows = slice(blk * nq, blk * nq + nk)
        scores = []
        for p in range(npairs):
            cols = slice(p * pair_w, (p + 1) * pair_w)
            q = q_ref[0, qrows, cols]
            scores.append(_dot_nt(q, k_a[krows, cols]))
            scores.append(_dot_nt(q, k_b[krows, cols]))
        scores = [jnp.where(valid, s, -jnp.inf) for s in scores]
        tops = [jnp.max(s, axis=-1, keepdims=True) for s in scores]
        probs = [jnp.exp2(s - m).astype(BF16) for s, m in zip(scores, tops)]
        for p in range(npairs):
            cols = slice(p * pair_w, (p + 1) * pair_w)
            res = (_dot(probs[2 * p], jnp.concatenate([v_a[krows, cols], ones_a], axis=1))
                   + _dot(probs[2 * p + 1], jnp.concatenate([v_b[krows, cols], ones_b], axis=1)))
            num, den = res[:, 0:pair_w], res[:, pair_w:2 * pair_w]
            o_ref[0, qrows, cols] = (num / den).astype(o_ref.dtype)
            lse_ref[0, qrows, cols] = jnp.where(lo_q, tops[2 * p], tops[2 * p + 1]) + jnp.log2(den)


def _attn_call(q, k, v, r):
    B, Lr, rW = q.shape
    W = rW // r
    blocks = min(ATT_STEP_BLOCKS, Lr // ATT_QBLK)
    step = ATT_QBLK * blocks
    classes = min(r, ATT_STEP_BLOCKS // blocks)
    width = classes * W
    assert W == ATT_WIDTH and Lr % step == 0 and r % classes == 0
    per_step = step // ATT_RADIUS
    nhalo = Lr // ATT_RADIUS
    center = pl.BlockSpec((1, step, width), lambda b, c, n: (b, n, c))
    prev = pl.BlockSpec((1, ATT_RADIUS, width), lambda b, c, n: (b, jnp.maximum(per_step * n - 1, 0), c))
    nxt = pl.BlockSpec((1, ATT_RADIUS, width), lambda b, c, n: (b, jnp.minimum(per_step * (n + 1), nhalo - 1), c))
    return pl.pallas_call(
        _attn_body,
        grid=(B, r // classes, Lr // step),
        in_specs=[center, prev, center, nxt, prev, center, nxt],
        out_specs=[center, center],
        out_shape=[jax.ShapeDtypeStruct((B, Lr, rW), BF16), jax.ShapeDtypeStruct((B, Lr, rW), F32)],
        compiler_params=pltpu.CompilerParams(
            dimension_semantics=("parallel", "parallel", "parallel"), vmem_limit_bytes=VMEM_LIMIT),
        name=f"dilated_attn_r{r}",
    )(q, k, k, k, v, v, v)


FFN_CHUNK = 256


def _load_natural(a_ref, l_ref, perm_t_ref, r):
    if r == 1:
        return a_ref[0].astype(F32), l_ref[0]
    w = a_ref.shape[2] // r
    rows = PERM_ROWS // r
    a_nat, l_nat = [], []
    for sub in range(a_ref.shape[1] // rows):
        blk = slice(sub * rows, (sub + 1) * rows)
        a = jnp.concatenate([a_ref[0, blk, c * w:(c + 1) * w] for c in range(r)], axis=0)
        l = jnp.concatenate([l_ref[0, blk, c * w:(c + 1) * w] for c in range(r)], axis=0)
        l_hi = l.astype(BF16)
        l_lo = (l - l_hi.astype(F32)).astype(BF16)
        nat = _dot(perm_t_ref[...], jnp.concatenate([a, l_hi, l_lo], axis=1))
        a_nat.append(nat[:, 0:w])
        l_nat.append(nat[:, w:2 * w] + nat[:, 2 * w:3 * w])
    return jnp.concatenate(a_nat, axis=0), jnp.concatenate(l_nat, axis=0)


def _merge_body(x_ref, mod_ref, of_ref, ob_ref, sg_ref, a1_ref, l1_ref, a2_ref, l2_ref, a3_ref, l3_ref,
                gate_ref, gn_ref, n2_ref, fin_ref, wa_ref, wb_ref, wo_ref, wfi_ref, wfo_ref,
                p4t_ref, p16t_ref, y_ref, act_ref):
    mod = mod_ref[0]
    gate1, shift2, scale2, gate2 = mod[2:3], mod[3:4], mod[4:5], mod[5:6]

    o = of_ref[0].astype(F32) + ob_ref[0].astype(F32)
    o_a = jnp.concatenate([_rms(o[:, h * HG_DIM:(h + 1) * HG_DIM]) for h in range(HG_HEADS)], axis=1)
    o_a = o_a * gn_ref[...] * sg_ref[0]

    (a1, l1), (a2, l2), (a3, l3) = [
        _load_natural(a_ref, l_ref, perm_ref, r)
        for (_, r), a_ref, l_ref, perm_ref in zip(DILATION_PATTERNS, (a1_ref, a2_ref, a3_ref),
                                                  (l1_ref, l2_ref, l3_ref), (None, p4t_ref, p16t_ref))]
    top = jnp.maximum(jnp.maximum(l1, l2), l3)
    w1, w2, w3 = jnp.exp2(l1 - top), jnp.exp2(l2 - top), jnp.exp2(l3 - top)
    o_b = (a1 * w1 + a2 * w2 + a3 * w3) / (w1 + w2 + w3)

    d_model = x_ref.shape[2]
    merged = (gate_ref[0, :, 0:d_model] * _dot(o_a.astype(BF16), wa_ref[...])
              + gate_ref[0, :, d_model:2 * d_model] * _dot(o_b.astype(BF16), wb_ref[...]))
    x1 = x_ref[0] + gate1 * _dot(merged.astype(BF16), wo_ref[...])

    h2 = (_rms(x1) * n2_ref[...] * (1.0 + scale2) + shift2).astype(BF16)
    hidden = wfo_ref.shape[0]
    for c0 in range(0, hidden, FFN_CHUNK):
        gt = _dot(h2, wfi_ref[:, c0:c0 + FFN_CHUNK])
        up = _dot(h2, wfi_ref[:, hidden + c0:hidden + c0 + FFN_CHUNK])
        act_ref[:, c0:c0 + FFN_CHUNK] = (gt * _sigmoid(gt) * up).astype(BF16)
    x2 = x1 + gate2 * _dot(act_ref[...], wfo_ref[...])
    y_ref[0] = _rms(x2) * fin_ref[...]


def _merge_call(x, mod, o_f, o_b, sg, att, gates, gn, n2, fin, wa, wb, wo, wfi, wfo, tm):
    B, L, D = x.shape
    hidden = wfo.shape[0]
    assert L % tm == 0 and tm % PERM_ROWS == 0 and hidden % FFN_CHUNK == 0 and wfi.shape[1] == 2 * hidden
    row = lambda w, r=1: pl.BlockSpec((1, tm // r, r * w), lambda b, i: (b, i, 0))
    att_flat = [t for pair in att for t in pair]
    att_specs = [row(ATT_WIDTH, r) for (_, r) in DILATION_PATTERNS for _ in range(2)]
    perms_t = [_dilation_perm(PERM_ROWS, r).T for (_, r) in DILATION_PATTERNS[1:]]
    return pl.pallas_call(
        _merge_body,
        grid=(B, L // tm),
        in_specs=[row(D), pl.BlockSpec((1,) + mod.shape[1:], lambda b, i: (b, 0, 0))]
                 + [row(HG_WIDTH)] * 3 + att_specs + [row(2 * D)]
                 + [_resident(t.shape) for t in (gn, n2, fin, wa, wb, wo, wfi, wfo, *perms_t)],
        out_specs=row(D),
        out_shape=jax.ShapeDtypeStruct((B, L, D), F32),
        scratch_shapes=[pltpu.VMEM((tm, hidden), BF16)],
        compiler_params=pltpu.CompilerParams(
            dimension_semantics=("parallel", "parallel"), vmem_limit_bytes=VMEM_LIMIT),
        name="merge_ffn",
    )(x, mod, o_f, o_b, sg, *att_flat, gates, gn, n2, fin, wa, wb, wo, wfi, wfo, *perms_t)


def _rope_tables(L):
    half = ATT_HDIM // 2
    lane = np.arange(2 * ATT_HDIM)
    inv = ROPE_THETA ** (-(lane % half).astype(np.float64) / half)
    sign = np.where(lane % ATT_HDIM < half, -1.0, 1.0)
    ang = np.arange(L, dtype=np.float64)[:, None] * inv[None, :]
    return jnp.asarray(np.cos(ang), dtype=F32), jnp.asarray(np.sin(ang) * sign[None, :], dtype=F32)


def _trunk(x, mod, lb, p, tm_in, tm_out, t_scan):
    B, L, D = x.shape
    cos, sin_signed = _rope_tables(L)
    (hg, v_h, sg, gates, *qkv) = _inproj_call(
        x, mod, p["g1"], p["w_in"], p["w_in8"], p["w_in8_unscale"], p["b_in"], lb, cos, sin_signed, tm_in)
    o_f, o_b = _hgrn_call(hg, v_h, t_scan)
    att = [_attn_call(*qkv[3 * i:3 * i + 3], r) for i, (_, r) in enumerate(DILATION_PATTERNS)]
    return _merge_call(x, mod, o_f, o_b, sg, att, gates, p["gn"], p["n2"], p["fin"],
                       p["wa"], p["wb"], p["wo"], p["wfi"], p["wfo"], tm_out)


def kernel(x_prompt, x_sample, c_prompt, c_sample, w_ada, b_ada, norm1_g, w_in, b_in, lb_logits, hg_norm_g, w_branch_a, w_branch_b, w_out, norm2_g, w_ffn_in, w_ffn_out, final_norm_g):
    assert w_ada.shape[0] == 1 and lb_logits.shape[0] == 2, "single-layer trunk"
    D = x_prompt.shape[-1]
    bp, bs = c_prompt.shape[0], c_sample.shape[0]
    c_all = jnp.concatenate([c_prompt, c_sample], axis=0)
    mod = _mod_call(c_all, w_ada[0], b_ada[0])
    mod = mod.reshape(mod.shape[0], 6, D)
    lb = _lb_call(lb_logits)
    row = lambda v: v.reshape(1, -1).astype(F32)
    n16 = N_BF16_BLOCKS * PROJ_BLOCK
    w_gate = w_in[0, :, n16:]
    col_top = jnp.max(jnp.abs(w_gate), axis=0, keepdims=True)
    w_unscale = jnp.where(col_top > 0.0, col_top, 1.0) * (1.0 / FP8_TOP)
    p = dict(g1=row(norm1_g[0]), w_in=w_in[0, :, :n16].astype(BF16), w_in8=(w_gate / w_unscale).astype(FP8),
             w_in8_unscale=w_unscale, b_in=row(b_in[0]), gn=row(hg_norm_g[0]),
             n2=row(norm2_g[0]), fin=row(final_norm_g), wa=w_branch_a[0].astype(BF16),
             wb=w_branch_b[0].astype(BF16), wo=w_out[0].astype(BF16), wfi=w_ffn_in[0].astype(BF16),
             wfo=w_ffn_out[0].astype(BF16))
    y_prompt = _trunk(x_prompt, mod[:bp], lb, p, 512, 512, 2048)
    y_sample = _trunk(x_sample, mod[bp:bp + bs], lb, p, 512, 512, 2048)
    return (y_prompt, y_sample)
```

```python
import numpy as np
import jax
import jax.numpy as jnp
from jax import lax
from jax.experimental import pallas as pl
from jax.experimental.pallas import tpu as pltpu

F32 = jnp.float32
BF16 = jnp.bfloat16
LANES = 128
SUBLANES = 8

NORM_EPS = 1e-6
HG_HEADS = 4
HG_DIM = 128
HG_WIDTH = HG_HEADS * HG_DIM
ATT_HEADS = 8
ATT_HDIM = 64
ATT_WIDTH = ATT_HEADS * ATT_HDIM
DILATION_PATTERNS = ((128, 1), (512, 4), (2048, 16))
ATT_RADIUS = 64
ROPE_THETA = 10000.0
LOG2E = 1.4426950408889634
PROJ_BLOCK = 512
N_PROJ_BLOCKS = 12
N_BF16_BLOCKS = 8
FP8 = jnp.float8_e4m3fn
FP8_TOP = 240.0
PERM_ROWS = 256

HG_CHUNK = 64
HG_SUB = 16
HG_SUB_SHIFT = 4
HG_STEP_HEADS = 2
HG_SLOTS, HG_SLOT_FWD, HG_SLOT_BWD = 5, 1, 3
HG_EXP2_CLAMP = 110.0
EXP2_ZERO = -1e30
V7X_VMEM_BYTES = 64 * 1024 * 1024
VMEM_LIMIT = V7X_VMEM_BYTES - 8 * 1024 * 1024


def _dot(a, b):
    return jnp.dot(a, b, preferred_element_type=F32)


def _dot_nt(a, b):
    return lax.dot_general(a, b, (((1,), (1,)), ((), ())), preferred_element_type=F32)


def _dot_tn(a, b):
    return lax.dot_general(a, b, (((0,), (0,)), ((), ())), preferred_element_type=F32)


def _sigmoid(x):
    return 1.0 / (1.0 + jnp.exp(-x))


def _rms(x):
    return x * lax.rsqrt(jnp.mean(x * x, axis=-1, keepdims=True) + NORM_EPS)


def _resident(shape):
    nd = len(shape)
    return pl.BlockSpec(shape, lambda *_: (0,) * nd, pipeline_mode=pl.Buffered(1))


def _mod_body(c_ref, w_ref, b_ref, o_ref):
    nrows, bn = c_ref.shape[0], w_ref.shape[1]
    outs = []
    for r in range(nrows):
        c = c_ref[r]
        a = c * _sigmoid(c)
        cols = [jnp.sum(a * w_ref[:, j * LANES:(j + 1) * LANES], axis=0, keepdims=True)
                for j in range(bn // LANES)]
        outs.append(jnp.concatenate(cols, axis=1) + b_ref[...])
    outs.append(jnp.zeros((o_ref.shape[0] - nrows, bn), F32))
    o_ref[...] = jnp.concatenate(outs, axis=0)


def _mod_call(c, w_ada, b_ada):
    nrows, d = c.shape
    n = w_ada.shape[1]
    bn = 1024
    rows_out = -(-nrows // SUBLANES) * SUBLANES
    c_rep = jnp.broadcast_to(c[:, :, None], (nrows, d, LANES))
    return pl.pallas_call(
        _mod_body,
        grid=(n // bn,),
        in_specs=[pl.BlockSpec((nrows, d, LANES), lambda j: (0, 0, 0)),
                  pl.BlockSpec((d, bn), lambda j: (0, j)),
                  pl.BlockSpec((1, bn), lambda j: (0, j))],
        out_specs=pl.BlockSpec((rows_out, bn), lambda j: (0, j)),
        out_shape=jax.ShapeDtypeStruct((rows_out, n), F32),
        name="adaln_mod",
    )(c_rep, w_ada, b_ada.reshape(1, n))


def _lb_body(l_ref, o_ref):
    l = l_ref[...]
    e = jnp.exp(l - jnp.max(l, axis=0, keepdims=True))
    o_ref[...] = e[0:1] / jnp.sum(e, axis=0, keepdims=True)


def _lb_call(lb_logits):
    n = lb_logits.shape[0]
    flat = lb_logits.reshape(n, -1).astype(F32)
    return pl.pallas_call(
        _lb_body,
        out_shape=jax.ShapeDtypeStruct((1, flat.shape[1]), F32),
        name="hgrn_lower_bounds",
    )(flat)


def _rope(p, cos, sin_signed):
    n = p.shape[-1]
    lane = lax.broadcasted_iota(jnp.int32, p.shape, 1)
    first_half = (lane & 63) < 32
    partner = jnp.where(first_half, pltpu.roll(p, n - 32, 1), pltpu.roll(p, 32, 1))
    reps = n // cos.shape[-1]
    return p * jnp.tile(cos, (1, reps)) + partner * jnp.tile(sin_signed, (1, reps))


def _store_dilated(t_bf, perm_ref, out_ref, r):
    if r == 1:
        out_ref[0] = t_bf
        return
    tm, w = t_bf.shape
    rows = PERM_ROWS // r
    for sub in range(tm // PERM_ROWS):
        grouped = _dot(perm_ref[...], t_bf[sub * PERM_ROWS:(sub + 1) * PERM_ROWS]).astype(BF16)
        for c in range(r):
            out_ref[0, sub * rows:(sub + 1) * rows, c * w:(c + 1) * w] = grouped[c * rows:(c + 1) * rows]


def _inproj_body(x_ref, mod_ref, g1_ref, w_ref, w8_ref, w8s_ref, b_ref, lb_ref, cos_ref, sin_ref, p4_ref, p16_ref,
                 hg_ref, vh_ref, sg_ref, gate_ref,
                 q1_ref, k1_ref, v1_ref, q4_ref, k4_ref, v4_ref, q16_ref, k16_ref, v16_ref):
    x = x_ref[0]
    mod = mod_ref[0]
    shift1, scale1 = mod[0:1], mod[1:2]
    h = _rms(x) * g1_ref[...]
    h = h * (1.0 + scale1) + shift1
    hb = h.astype(BF16)
    h_top = jnp.max(jnp.abs(h), axis=-1, keepdims=True)
    h_top = jnp.where(h_top > 0.0, h_top, 1.0)
    h_unscale = h_top * (1.0 / FP8_TOP)
    hb8 = (h * (FP8_TOP / h_top)).astype(FP8)

    def proj8(j):
        cols = slice(j * PROJ_BLOCK, (j + 1) * PROJ_BLOCK)
        cols8 = slice((j - N_BF16_BLOCKS) * PROJ_BLOCK, (j - N_BF16_BLOCKS + 1) * PROJ_BLOCK)
        return _dot(hb8, w8_ref[:, cols8]) * h_unscale * w8s_ref[:, cols8] + b_ref[:, cols]

    def proj(j):
        cols = slice(j * PROJ_BLOCK, (j + 1) * PROJ_BLOCK)
        return _dot(hb, w_ref[:, cols]) + b_ref[:, cols]

    p = proj(0)
    hg_ref[0, :, 0:HG_WIDTH] = p * _sigmoid(p) * (HG_DIM ** -0.5)

    def gates(p, lb, slot):
        f = lb + (1.0 - lb) * _sigmoid(p)
        hg_ref[0, :, slot * HG_WIDTH:(slot + 1) * HG_WIDTH] = jnp.log2(f)
        hg_ref[0, :, (slot + 1) * HG_WIDTH:(slot + 2) * HG_WIDTH] = 1.0 - f

    gates(proj(1), lb_ref[:, 0:HG_WIDTH], HG_SLOT_FWD)
    gates(proj(2), lb_ref[:, HG_WIDTH:2 * HG_WIDTH], HG_SLOT_BWD)
    vh_ref[0] = proj(3).astype(BF16)
    p = proj(4)
    sg_ref[0] = (p * _sigmoid(p)).astype(sg_ref.dtype)

    cos, sin_signed = cos_ref[...], sin_ref[...]
    q_a = (_rope(proj(5), cos, sin_signed) * (ATT_HDIM ** -0.5 * LOG2E)).astype(BF16)
    k_a = _rope(proj(6), cos, sin_signed).astype(BF16)
    v_a = proj(7).astype(BF16)
    for t, refs in ((q_a, (q1_ref, q4_ref, q16_ref)), (k_a, (k1_ref, k4_ref, k16_ref)),
                    (v_a, (v1_ref, v4_ref, v16_ref))):
        for (_, r), perm_ref, out_ref in zip(DILATION_PATTERNS, (None, p4_ref, p16_ref), refs):
            _store_dilated(t, perm_ref, out_ref, r)
    for j in range(N_BF16_BLOCKS, N_PROJ_BLOCKS):
        cols = slice((j - N_BF16_BLOCKS) * PROJ_BLOCK, (j - N_BF16_BLOCKS + 1) * PROJ_BLOCK)
        gate_ref[0, :, cols] = _sigmoid(proj8(j)).astype(gate_ref.dtype)


def _dilation_perm(tm, r):
    i = np.arange(tm)
    src = r * (i % (tm // r)) + i // (tm // r)
    return jnp.asarray(src[:, None] == np.arange(tm)[None, :], dtype=BF16)


def _inproj_call(x, mod, g1, w_in, w_in8, w_in8_unscale, b_in, lb, cos, sin_signed, tm):
    B, L, D = x.shape
    n_in = w_in.shape[1] + w_in8.shape[1]
    assert n_in == N_PROJ_BLOCKS * PROJ_BLOCK and L % tm == 0 and tm % PERM_ROWS == 0
    row = lambda w, r=1: pl.BlockSpec((1, tm // r, r * w), lambda b, i: (b, i, 0))
    tab = pl.BlockSpec((tm, cos.shape[1]), lambda b, i: (i, 0))
    shp = lambda w, dt, r=1: jax.ShapeDtypeStruct((B, L // r, r * w), dt)
    perms = [_dilation_perm(PERM_ROWS, r) for (_, r) in DILATION_PATTERNS[1:]]
    att_specs = [row(ATT_WIDTH, r) for (_, r) in DILATION_PATTERNS for _ in range(3)]
    att_shapes = [shp(ATT_WIDTH, BF16, r) for (_, r) in DILATION_PATTERNS for _ in range(3)]
    return pl.pallas_call(
        _inproj_body,
        grid=(B, L // tm),
        in_specs=[row(D),
                  pl.BlockSpec((1,) + mod.shape[1:], lambda b, i: (b, 0, 0)),
                  _resident(g1.shape), _resident(w_in.shape), _resident(w_in8.shape), _resident(w_in8_unscale.shape),
                  _resident(b_in.shape),
                  _resident(lb.shape),
                  tab, tab] + [_resident(p.shape) for p in perms],
        out_specs=[row(HG_SLOTS * HG_WIDTH)] + [row(HG_WIDTH)] * 2 + [row(2 * D)] + att_specs,
        out_shape=[shp(HG_SLOTS * HG_WIDTH, F32)] + [shp(HG_WIDTH, BF16)] * 2 + [shp(2 * D, BF16)] + att_shapes,
        compiler_params=pltpu.CompilerParams(
            dimension_semantics=("parallel", "parallel"), vmem_limit_bytes=VMEM_LIMIT),
        name="inproj",
    )(x, mod, g1, w_in, w_in8, w_in8_unscale, b_in, lb, cos, sin_signed, *perms)


def _bcast_row(c, group, r):
    T, w = c.shape
    c3 = c.reshape(T // group, group, w)
    return jnp.broadcast_to(c3[:, r:r + 1, :], c3.shape).reshape(T, w)


def _split2(g):
    g1 = g.astype(BF16)
    return g1, (g - g1.astype(F32)).astype(BF16)


def _hgrn_dir(q, k, v_bf, g, s_in, rev, o_ref, cols):
    T, dk = q.shape
    C, nch, nsb = HG_CHUNK, T // HG_CHUNK, HG_CHUNK // HG_SUB
    t = lax.broadcasted_iota(jnp.int32, (C, C), 0)
    s = lax.broadcasted_iota(jnp.int32, (C, C), 1)
    order = (s >= t) if rev else (s <= t)
    inner_mask = order & ((t >> HG_SUB_SHIFT) == (s >> HG_SUB_SHIFT))
    tri = order.astype(BF16)

    g_terms = jnp.concatenate(_split2(g), axis=1)
    c_parts = []
    for ci in range(nch):
        r = _dot(tri, g_terms[ci * C:(ci + 1) * C])
        c_parts.append(r[:, 0:dk] + r[:, dk:2 * dk])
    c = jnp.concatenate(c_parts, axis=0)
    yield

    far = 0 if rev else HG_SUB - 1
    edge = _bcast_row(c, HG_SUB, far)
    mid = _bcast_row(c, HG_SUB, HG_SUB // 2)
    c_end = _bcast_row(c, C, 0 if rev else C - 1)
    rowc = lax.broadcasted_iota(jnp.int32, (T, dk), 0) & (C - 1)

    k_edge = k * jnp.exp2(edge - c)
    q_cat, k_cat = [], []
    for J in (range(1, nsb) if rev else range(nsb - 1)):
        ref_j = _bcast_row(c, C, J * HG_SUB + far)
        queries = (rowc < J * HG_SUB) if rev else (rowc >= (J + 1) * HG_SUB)
        q_cat.append((q * jnp.exp2(jnp.where(queries, c - ref_j, EXP2_ZERO))).astype(BF16))
        in_j = (rowc >= J * HG_SUB) & (rowc < (J + 1) * HG_SUB)
        k_cat.append(jnp.where(in_j, k_edge, 0.0).astype(BF16))
    q_cat = jnp.concatenate(q_cat, axis=1)
    k_cat = jnp.concatenate(k_cat, axis=1)

    d_mid = jnp.clip(c - mid, -HG_EXP2_CLAMP, HG_EXP2_CLAMP)
    q_mid = (q * jnp.exp2(d_mid)).astype(BF16)
    k_mid = (k * jnp.exp2(-d_mid)).astype(BF16)
    q_dec = (q * jnp.exp2(c)).astype(BF16)
    k_end = (k * jnp.exp2(c_end - c)).astype(BF16)
    decay = jnp.exp2(c_end)
    yield

    scores, upd = [], []
    for ci in range(nch):
        rows = slice(ci * C, (ci + 1) * C)
        sc = _dot_nt(q_cat[rows], k_cat[rows])
        sc = sc + jnp.where(inner_mask, _dot_nt(q_mid[rows], k_mid[rows]), 0.0)
        scores.append(sc.astype(BF16))
        upd.append(_dot_tn(v_bf[rows], k_end[rows]))
        yield

    s_t = s_in
    for ci in (reversed(range(nch)) if rev else range(nch)):
        rows = slice(ci * C, (ci + 1) * C)
        lhs = jnp.concatenate([q_dec[rows], scores[ci]], axis=1)
        rhs = jnp.concatenate([s_t.T.astype(BF16), v_bf[rows]], axis=0)
        o_ref[0, rows, cols] = _dot(lhs, rhs).astype(o_ref.dtype)
        s_t = s_t * decay[ci * C:ci * C + 1] + upd[ci]
        yield
    return s_t


def _hgrn_body(qf_ref, lff_ref, kf_ref, vf_ref, qb_ref, lfb_ref, kb_ref, vb_ref,
               of_ref, ob_ref, sf_ref, sb_ref):
    @pl.when(pl.program_id(2) == 0)
    def _():
        sf_ref[...] = jnp.zeros_like(sf_ref)
        sb_ref[...] = jnp.zeros_like(sb_ref)

    runs = []
    for h in range(HG_STEP_HEADS):
        cols = slice(h * HG_DIM, (h + 1) * HG_DIM)
        runs.append((_hgrn_dir(qf_ref[0, :, cols], kf_ref[0, :, cols], vf_ref[0, :, cols], lff_ref[0, :, cols],
                               sf_ref[h], False, of_ref, cols), sf_ref, h))
        runs.append((_hgrn_dir(qb_ref[0, :, cols], kb_ref[0, :, cols], vb_ref[0, :, cols], lfb_ref[0, :, cols],
                               sb_ref[h], True, ob_ref, cols), sb_ref, h))
    while runs:
        for run in list(runs):
            gen, state_ref, h = run
            try:
                next(gen)
            except StopIteration as done:
                state_ref[h] = done.value
                runs.remove(run)


def _hgrn_call(hg, v, T):
    B, L, W = v.shape
    nb = L // T
    assert L % T == 0 and T % HG_CHUNK == 0 and W == HG_WIDTH and hg.shape[2] == HG_SLOTS * W
    width = HG_STEP_HEADS * HG_DIM
    per_slot = W // width
    fwd = pl.BlockSpec((1, T, width), lambda b, h, j: (b, j, h))
    bwd = pl.BlockSpec((1, T, width), lambda b, h, j: (b, nb - 1 - j, h))
    fwd_slot = lambda s: pl.BlockSpec((1, T, width), lambda b, h, j: (b, j, s * per_slot + h))
    bwd_slot = lambda s: pl.BlockSpec((1, T, width), lambda b, h, j: (b, nb - 1 - j, s * per_slot + h))
    return pl.pallas_call(
        _hgrn_body,
        grid=(B, HG_HEADS // HG_STEP_HEADS, nb),
        in_specs=[fwd_slot(0), fwd_slot(HG_SLOT_FWD), fwd_slot(HG_SLOT_FWD + 1), fwd,
                  bwd_slot(0), bwd_slot(HG_SLOT_BWD), bwd_slot(HG_SLOT_BWD + 1), bwd],
        out_specs=[fwd, bwd],
        out_shape=[jax.ShapeDtypeStruct((B, L, W), BF16)] * 2,
        scratch_shapes=[pltpu.VMEM((HG_STEP_HEADS, HG_DIM, HG_DIM), F32)] * 2,
        compiler_params=pltpu.CompilerParams(
            dimension_semantics=("parallel", "parallel", "arbitrary"), vmem_limit_bytes=VMEM_LIMIT),
        name="hgrn_scan",
    )(hg, hg, hg, v, hg, hg, hg, v)


ATT_QBLK = 2 * ATT_RADIUS
ATT_STEP_BLOCKS = 8


def _attn_body(q_ref, kp_ref, kc_ref, kn_ref, vp_ref, vc_ref, vn_ref, o_ref, lse_ref):
    n = pl.program_id(2)
    last = pl.num_programs(2) - 1
    kcat = jnp.concatenate([kp_ref[0], kc_ref[0], kn_ref[0]], axis=0)
    vcat = jnp.concatenate([vp_ref[0], vc_ref[0], vn_ref[0]], axis=0)
    nq, nk = ATT_QBLK, ATT_QBLK + 2 * ATT_RADIUS
    nblk = q_ref.shape[1] // nq
    i = lax.broadcasted_iota(jnp.int32, (nq, nk), 0)
    j = lax.broadcasted_iota(jnp.int32, (nq, nk), 1)
    band = (j >= i) & (j <= i + 2 * ATT_RADIUS)
    has_prev = (j >= ATT_RADIUS) | (n > 0)
    has_next = (j < nq + ATT_RADIUS) | (n < last)

    pair_w = 2 * ATT_HDIM
    npairs = q_ref.shape[2] // pair_w
    lane = lax.broadcasted_iota(jnp.int32, kcat.shape, 1) & (pair_w - 1)
    mask_a = jnp.where(lane < ATT_HDIM, 1.0, 0.0).astype(BF16)
    mask_b = jnp.where(lane < ATT_HDIM, 0.0, 1.0).astype(BF16)
    k_a, k_b, v_a, v_b = kcat * mask_a, kcat * mask_b, vcat * mask_a, vcat * mask_b
    lo_k = lax.broadcasted_iota(jnp.int32, (nk, pair_w), 1) < ATT_HDIM
    ones_a = jnp.where(lo_k, 1.0, 0.0).astype(BF16)
    ones_b = jnp.where(lo_k, 0.0, 1.0).astype(BF16)
    lo_q = lax.broadcasted_iota(jnp.int32, (nq, pair_w), 1) < ATT_HDIM

    for blk in range(nblk):
        valid = band
        if blk == 0:
            valid = valid & has_prev
        if blk == nblk - 1:
            valid = valid & has_next
        qrows = slice(blk * nq, (blk + 1) * nq)
        krows = slice(blk * nq, blk * nq + nk)
        scores = []
        for p in range(npairs):
            cols = slice(p * pair_w, (p + 1) * pair_w)
            q = q_ref[0, qrows, cols]
            scores.append(_dot_nt(q, k_a[krows, cols]))
            scores.append(_dot_nt(q, k_b[krows, cols]))
        scores = [jnp.where(valid, s, -jnp.inf) for s in scores]
        tops = [jnp.max(s, axis=-1, keepdims=True) for s in scores]
        probs = [jnp.exp2(s - m).astype(BF16) for s, m in zip(scores, tops)]
        for p in range(npairs):
            cols = slice(p * pair_w, (p + 1) * pair_w)
            res = _dot(jnp.concatenate([probs[2 * p], probs[2 * p + 1]], axis=1),
                       jnp.concatenate([jnp.concatenate([v_a[krows, cols], ones_a], axis=1),
                                        jnp.concatenate([v_b[krows, cols], ones_b], axis=1)], axis=0))
            num, den = res[:, 0:pair_w], res[:, pair_w:2 * pair_w]
            o_ref[0, qrows, cols] = (num / den).astype(o_ref.dtype)
            lse_ref[0, qrows, cols] = jnp.where(lo_q, tops[2 * p], tops[2 * p + 1]) + jnp.log2(den)


def _attn_call(q, k, v, r):
    B, Lr, rW = q.shape
    W = rW // r
    blocks = min(ATT_STEP_BLOCKS, Lr // ATT_QBLK)
    step = ATT_QBLK * blocks
    classes = min(r, ATT_STEP_BLOCKS // blocks)
    width = classes * W
    assert W == ATT_WIDTH and Lr % step == 0 and r % classes == 0
    per_step = step // ATT_RADIUS
    nhalo = Lr // ATT_RADIUS
    center = pl.BlockSpec((1, step, width), lambda b, c, n: (b, n, c))
    prev = pl.BlockSpec((1, ATT_RADIUS, width), lambda b, c, n: (b, jnp.maximum(per_step * n - 1, 0), c))
    nxt = pl.BlockSpec((1, ATT_RADIUS, width), lambda b, c, n: (b, jnp.minimum(per_step * (n + 1), nhalo - 1), c))
    return pl.pallas_call(
        _attn_body,
        grid=(B, r // classes, Lr // step),
        in_specs=[center, prev, center, nxt, prev, center, nxt],
        out_specs=[center, center],
        out_shape=[jax.ShapeDtypeStruct((B, Lr, rW), BF16), jax.ShapeDtypeStruct((B, Lr, rW), F32)],
        compiler_params=pltpu.CompilerParams(
            dimension_semantics=("parallel", "parallel", "parallel"), vmem_limit_bytes=VMEM_LIMIT),
        name=f"dilated_attn_r{r}",
    )(q, k, k, k, v, v, v)


FFN_CHUNK = 256


def _load_natural(a_ref, l_ref, perm_t_ref, r):
    if r == 1:
        return a_ref[0].astype(F32), l_ref[0]
    w = a_ref.shape[2] // r
    rows = PERM_ROWS // r
    a_nat, l_nat = [], []
    for sub in range(a_ref.shape[1] // rows):
        blk = slice(sub * rows, (sub + 1) * rows)
        a = jnp.concatenate([a_ref[0, blk, c * w:(c + 1) * w] for c in range(r)], axis=0)
        l = jnp.concatenate([l_ref[0, blk, c * w:(c + 1) * w] for c in range(r)], axis=0)
        l_hi = l.astype(BF16)
        l_lo = (l - l_hi.astype(F32)).astype(BF16)
        nat = _dot(perm_t_ref[...], jnp.concatenate([a, l_hi, l_lo], axis=1))
        a_nat.append(nat[:, 0:w])
        l_nat.append(nat[:, w:2 * w] + nat[:, 2 * w:3 * w])
    return jnp.concatenate(a_nat, axis=0), jnp.concatenate(l_nat, axis=0)


def _merge_body(x_ref, mod_ref, of_ref, ob_ref, sg_ref, a1_ref, l1_ref, a2_ref, l2_ref, a3_ref, l3_ref,
                gate_ref, gn_ref, n2_ref, fin_ref, wa_ref, wb_ref, wo_ref, wfi_ref, wfo_ref,
                p4t_ref, p16t_ref, y_ref, act_ref):
    mod = mod_ref[0]
    gate1, shift2, scale2, gate2 = mod[2:3], mod[3:4], mod[4:5], mod[5:6]

    o = of_ref[0].astype(F32) + ob_ref[0].astype(F32)
    o_a = jnp.concatenate([_rms(o[:, h * HG_DIM:(h + 1) * HG_DIM]) for h in range(HG_HEADS)], axis=1)
    o_a = o_a * gn_ref[...] * sg_ref[0]

    (a1, l1), (a2, l2), (a3, l3) = [
        _load_natural(a_ref, l_ref, perm_ref, r)
        for (_, r), a_ref, l_ref, perm_ref in zip(DILATION_PATTERNS, (a1_ref, a2_ref, a3_ref),
                                                  (l1_ref, l2_ref, l3_ref), (None, p4t_ref, p16t_ref))]
    top = jnp.maximum(jnp.maximum(l1, l2), l3)
    w1, w2, w3 = jnp.exp2(l1 - top), jnp.exp2(l2 - top), jnp.exp2(l3 - top)
    o_b = (a1 * w1 + a2 * w2 + a3 * w3) / (w1 + w2 + w3)

    d_model = x_ref.shape[2]
    merged = (gate_ref[0, :, 0:d_model] * _dot(o_a.astype(BF16), wa_ref[...])
              + gate_ref[0, :, d_model:2 * d_model] * _dot(o_b.astype(BF16), wb_ref[...]))
    x1 = x_ref[0] + gate1 * _dot(merged.astype(BF16), wo_ref[...])

    h2 = (_rms(x1) * n2_ref[...] * (1.0 + scale2) + shift2).astype(BF16)
    hidden = wfo_ref.shape[0]
    for c0 in range(0, hidden, FFN_CHUNK):
        gt = _dot(h2, wfi_ref[:, c0:c0 + FFN_CHUNK])
        up = _dot(h2, wfi_ref[:, hidden + c0:hidden + c0 + FFN_CHUNK])
        act_ref[:, c0:c0 + FFN_CHUNK] = (gt * _sigmoid(gt) * up).astype(BF16)
    x2 = x1 + gate2 * _dot(act_ref[...], wfo_ref[...])
    y_ref[0] = _rms(x2) * fin_ref[...]


def _merge_call(x, mod, o_f, o_b, sg, att, gates, gn, n2, fin, wa, wb, wo, wfi, wfo, tm):
    B, L, D = x.shape
    hidden = wfo.shape[0]
    assert L % tm == 0 and tm % PERM_ROWS == 0 and hidden % FFN_CHUNK == 0 and wfi.shape[1] == 2 * hidden
    row = lambda w, r=1: pl.BlockSpec((1, tm // r, r * w), lambda b, i: (b, i, 0))
    att_flat = [t for pair in att for t in pair]
    att_specs = [row(ATT_WIDTH, r) for (_, r) in DILATION_PATTERNS for _ in range(2)]
    perms_t = [_dilation_perm(PERM_ROWS, r).T for (_, r) in DILATION_PATTERNS[1:]]
    return pl.pallas_call(
        _merge_body,
        grid=(B, L // tm),
        in_specs=[row(D), pl.BlockSpec((1,) + mod.shape[1:], lambda b, i: (b, 0, 0))]
                 + [row(HG_WIDTH)] * 3 + att_specs + [row(2 * D)]
                 + [_resident(t.shape) for t in (gn, n2, fin, wa, wb, wo, wfi, wfo, *perms_t)],
        out_specs=row(D),
        out_shape=jax.ShapeDtypeStruct((B, L, D), F32),
        scratch_shapes=[pltpu.VMEM((tm, hidden), BF16)],
        compiler_params=pltpu.CompilerParams(
            dimension_semantics=("parallel", "parallel"), vmem_limit_bytes=VMEM_LIMIT),
        name="merge_ffn",
    )(x, mod, o_f, o_b, sg, *att_flat, gates, gn, n2, fin, wa, wb, wo, wfi, wfo, *perms_t)


def _rope_tables(L):
    half = ATT_HDIM // 2
    lane = np.arange(2 * ATT_HDIM)
    inv = ROPE_THETA ** (-(lane % half).astype(np.float64) / half)
    sign = np.where(lane % ATT_HDIM < half, -1.0, 1.0)
    ang = np.arange(L, dtype=np.float64)[:, None] * inv[None, :]
    return jnp.asarray(np.cos(ang), dtype=F32), jnp.asarray(np.sin(ang) * sign[None, :], dtype=F32)


def _trunk(x, mod, lb, p, tm_in, tm_out, t_scan):
    B, L, D = x.shape
    cos, sin_signed = _rope_tables(L)
    (hg, v_h, sg, gates, *qkv) = _inproj_call(
        x, mod, p["g1"], p["w_in"], p["w_in8"], p["w_in8_unscale"], p["b_in"], lb, cos, sin_signed, tm_in)
    o_f, o_b = _hgrn_call(hg, v_h, t_scan)
    att = [_attn_call(*qkv[3 * i:3 * i + 3], r) for i, (_, r) in enumerate(DILATION_PATTERNS)]
    return _merge_call(x, mod, o_f, o_b, sg, att, gates, p["gn"], p["n2"], p["fin"],
                       p["wa"], p["wb"], p["wo"], p["wfi"], p["wfo"], tm_out)


def kernel(x_prompt, x_sample, c_prompt, c_sample, w_ada, b_ada, norm1_g, w_in, b_in, lb_logits, hg_norm_g, w_branch_a, w_branch_b, w_out, norm2_g, w_ffn_in, w_ffn_out, final_norm_g):
    assert w_ada.shape[0] == 1 and lb_logits.shape[0] == 2, "single-layer trunk"
    D = x_prompt.shape[-1]
    bp, bs = c_prompt.shape[0], c_sample.shape[0]
    c_all = jnp.concatenate([c_prompt, c_sample], axis=0)
    mod = _mod_call(c_all, w_ada[0], b_ada[0])
    mod = mod.reshape(mod.shape[0], 6, D)
    lb = _lb_call(lb_logits)
    row = lambda v: v.reshape(1, -1).astype(F32)
    n16 = N_BF16_BLOCKS * PROJ_BLOCK
    w_gate = w_in[0, :, n16:]
    col_top = jnp.max(jnp.abs(w_gate), axis=0, keepdims=True)
    w_unscale = jnp.where(col_top > 0.0, col_top, 1.0) * (1.0 / FP8_TOP)
    p = dict(g1=row(norm1_g[0]), w_in=w_in[0, :, :n16].astype(BF16), w_in8=(w_gate / w_unscale).astype(FP8),
             w_in8_unscale=w_unscale, b_in=row(b_in[0]), gn=row(hg_norm_g[0]),
             n2=row(norm2_g[0]), fin=row(final_norm_g), wa=w_branch_a[0].astype(BF16),
             wb=w_branch_b[0].astype(BF16), wo=w_out[0].astype(BF16), wfi=w_ffn_in[0].astype(BF16),
             wfo=w_ffn_out[0].astype(BF16))
    y_prompt = _trunk(x_prompt, mod[:bp], lb, p, 512, 512, 2048)
    y_sample = _trunk(x_sample, mod[bp:bp + bs], lb, p, 512, 512, 2048)
    return (y_prompt, y_sample)
```
